```python
import jax, jax.numpy as jnp
from jax import lax
import numpy as np

D_MODEL = 1024
BATCH = 16
SEQ = 2048
DEPTH = 1
DEC_BATCH = 32
DEC_SEQ = 16
PAST_LEN = 1024

CHUNK = 64
HEAD_DIM = 64
N_HEADS_A = D_MODEL // (2 * HEAD_DIM)
N_HEADS_B = D_MODEL // (2 * HEAD_DIM)
IDX_HEADS = 8
IDX_DIM = 32
TOPK_MAX = 256
QBLK = 128
BAND_PREV = 8
BAND_PAST = BAND_PREV * CHUNK
BAND_W = BAND_PAST + CHUNK
REL_CLIP = 128
PEER_HEADS = 8
PEER_DQ = 256
N_KEYS = 128
N_EXPERTS = N_KEYS * N_KEYS
PEER_TOPK = 16
PEER_BLK = 128
ROPE_THETA = 10000.0
EPS = 1e-6
NEG = -1e30
COL_SIZES = (N_HEADS_A * HEAD_DIM, N_HEADS_A * HEAD_DIM, N_HEADS_A * HEAD_DIM,
             IDX_HEADS * IDX_DIM, IDX_DIM, IDX_HEADS,
             N_HEADS_B * HEAD_DIM, N_HEADS_B * HEAD_DIM, N_HEADS_B * HEAD_DIM,
             2 * D_MODEL)
D_IN = sum(COL_SIZES)

kernel_name = 'chunk_hybrid_dsa_band_peer_step'


def _rmsnorm(x, g):
    xf = x.astype(jnp.float32)
    y = xf * lax.rsqrt(jnp.mean(xf * xf, axis=-1, keepdims=True) + EPS)
    return (y * g.astype(jnp.float32)).astype(x.dtype)


def _rope(x, pos):
    half = x.shape[-1] // 2
    inv = jnp.power(ROPE_THETA, -jnp.arange(half, dtype=jnp.float32) / half)
    ang = pos.astype(jnp.float32)[:, None] * inv[None, :]
    cos = jnp.cos(ang)[:, None, :]
    sin = jnp.sin(ang)[:, None, :]
    xf = x.astype(jnp.float32)
    x1, x2 = xf[..., :half], xf[..., half:]
    return jnp.concatenate([x1 * cos - x2 * sin, x2 * cos + x1 * sin], axis=-1).astype(x.dtype)


def _dsa_one(q, k, v, iq, ik, iw, qpos, kpos):
    T, H, d = q.shape
    L = k.shape[0]
    n_sel = min(TOPK_MAX, L // 4)
    blk = min(QBLK, T)
    nb = -(-T // blk)
    pad = nb * blk - T
    qb = jnp.pad(q, ((0, pad), (0, 0), (0, 0))).reshape(nb, blk, H, d)
    iqb = jnp.pad(iq, ((0, pad), (0, 0), (0, 0))).reshape(nb, blk, IDX_HEADS, IDX_DIM)
    iwb = jnp.pad(iw, ((0, pad), (0, 0))).reshape(nb, blk, IDX_HEADS)
    pb = (qpos[0] + jnp.arange(nb * blk)).reshape(nb, blk)
    kchunk = kpos // CHUNK
    scale = HEAD_DIM ** -0.5

    def block(args):
        qx, iqx, iwx, px = args
        rel = jax.nn.relu(jnp.einsum('qhd,sd->qhs', iqx, ik).astype(jnp.float32))
        score = jnp.einsum('qhs,qh->qs', rel, iwx.astype(jnp.float32))
        adm = kchunk[None, :] <= (px // CHUNK)[:, None]
        score = jnp.where(adm, score, NEG)
        _, sel = lax.top_k(score, n_sel)
        ok = jnp.take_along_axis(adm, sel, axis=1)
        kg = k[sel]
        vg = v[sel]
        logits = jnp.einsum('qhd,qkhd->qhk', qx, kg).astype(jnp.float32) * scale
        logits = jnp.where(ok[:, None, :], logits, NEG)
        p = jax.nn.softmax(logits, axis=-1).astype(vg.dtype)
        return jnp.einsum('qhk,qkhd->qhd', p, vg)

    out = lax.map(block, (qb, iqb, iwb, pb))
    return out.reshape(nb * blk, H, d)[:T]


def _band_one(q, k, v, qpos, kpos, rel_bias):
    T, H, d = q.shape
    P = k.shape[0] - T
    nqc = -(-T // CHUNK)
    tp = nqc * CHUNK
    front = BAND_PAST - P
    back = tp - T
    kp = jnp.pad(k, ((front, back), (0, 0), (0, 0)))
    vp = jnp.pad(v, ((front, back), (0, 0), (0, 0)))
    valid = jnp.pad(jnp.ones((P + T,), dtype=bool), (front, back))
    kposp = jnp.pad(kpos, (front, back))
    qc = jnp.pad(q, ((0, back), (0, 0), (0, 0))).reshape(nqc, CHUNK, H, d)
    qposc = (qpos[0] + jnp.arange(tp)).reshape(nqc, CHUNK)
    idx = (jnp.arange(nqc) * CHUNK)[:, None] + jnp.arange(BAND_W)[None, :]
    kb = kp[idx]
    vb = vp[idx]
    kposb = kposp[idx]
    validb = valid[idx]
    s = jnp.einsum('cqhd,ckhd->chqk', qc, kb).astype(jnp.float32) * (HEAD_DIM ** -0.5)
    rel = jnp.clip(qposc[:, :, None] - kposb[:, None, :], -REL_CLIP, REL_CLIP) + REL_CLIP
    bias = jnp.moveaxis(rel_bias.astype(jnp.float32)[:, rel], 0, 1)
    cq = (qposc // CHUNK)[:, :, None]
    ck = (kposb // CHUNK)[:, None, :]
    ok = (ck <= cq) & (ck >= cq - BAND_PREV) & validb[:, None, :]
    s = jnp.where(ok[:, None, :, :], s + bias, NEG)
    p = jax.nn.softmax(s, axis=-1).astype(vb.dtype)
    o = jnp.einsum('chqk,ckhd->cqhd', p, vb)
    return o.reshape(tp, H, d)[:T]


def _peer(h, w_pq, c1, c2, u, v):
    N = h.shape[0]
    nb = -(-N // PEER_BLK)
    pad = nb * PEER_BLK - N
    hb = jnp.pad(h, ((0, pad), (0, 0))).reshape(nb, PEER_BLK, D_MODEL)
    half = PEER_DQ // 2

    def block(hx):
        q = (hx @ w_pq).astype(jnp.float32).reshape(PEER_BLK, PEER_HEADS, PEER_DQ)
        s1 = jnp.einsum('nhd,kd->nhk', q[..., :half], c1.astype(jnp.float32))
        s2 = jnp.einsum('nhd,kd->nhk', q[..., half:], c2.astype(jnp.float32))
        v1, i1 = lax.top_k(s1, PEER_TOPK)
        v2, i2 = lax.top_k(s2, PEER_TOPK)
        cand = (v1[..., :, None] + v2[..., None, :]).reshape(PEER_BLK, PEER_HEADS, PEER_TOPK * PEER_TOPK)
        cidx = (i1[..., :, None] * N_KEYS + i2[..., None, :]).reshape(PEER_BLK, PEER_HEADS, PEER_TOPK * PEER_TOPK)
        sv, si = lax.top_k(cand, PEER_TOPK)
        e = jnp.take_along_axis(cidx, si, axis=-1)
        g = jax.nn.softmax(sv, axis=-1)
        ue = u[e]
        ve = v[e]
        a = jax.nn.gelu(jnp.einsum('nhkd,nd->nhk', ue, hx).astype(jnp.float32), approximate=False)
        return jnp.einsum('nhk,nhkd->nd', (g * a).astype(ve.dtype), ve)

    return lax.map(block, hb).reshape(nb * PEER_BLK, D_MODEL)[:N]


def _layer(x, start, ca_k, ca_v, ca_ik, cb_k, cb_v, g_attn, w_in, g_qa, g_ka, g_ik, g_qb, g_kb,
           rel_bias, w_br_a, w_br_b, w_out, g_ffn, w_pq, c1, c2, u, v):
    B, T, _ = x.shape
    pos = start + jnp.arange(T)
    h = _rmsnorm(x, g_attn)
    proj = h @ w_in
    splits = [int(s) for s in np.cumsum(COL_SIZES)[:-1]]
    aq, ak, av, iq, ik, iw, bq, bk, bv, gates = jnp.split(proj, splits, axis=-1)
    aq = _rope(_rmsnorm(aq.reshape(B, T, N_HEADS_A, HEAD_DIM), g_qa), pos)
    ak = _rope(_rmsnorm(ak.reshape(B, T, N_HEADS_A, HEAD_DIM), g_ka), pos)
    av = av.reshape(B, T, N_HEADS_A, HEAD_DIM)
    iq = _rope(iq.reshape(B, T, IDX_HEADS, IDX_DIM), pos)
    ik = _rope(_rmsnorm(ik, g_ik)[:, :, None, :], pos)[:, :, 0, :]
    iw = iw * (IDX_HEADS ** -0.5)
    bq = _rmsnorm(bq.reshape(B, T, N_HEADS_B, HEAD_DIM), g_qb)
    bk = _rmsnorm(bk.reshape(B, T, N_HEADS_B, HEAD_DIM), g_kb)
    bv = bv.reshape(B, T, N_HEADS_B, HEAD_DIM)

    pa = ca_k.shape[1]
    pb = cb_k.shape[1]
    ka_all = jnp.concatenate([ca_k.astype(ak.dtype), ak], axis=1)
    va_all = jnp.concatenate([ca_v.astype(av.dtype), av], axis=1)
    ika_all = jnp.concatenate([ca_ik.astype(ik.dtype), ik], axis=1)
    kb_all = jnp.concatenate([cb_k.astype(bk.dtype), bk], axis=1)
    vb_all = jnp.concatenate([cb_v.astype(bv.dtype), bv], axis=1)
    kpos_a = start - pa + jnp.arange(pa + T)
    kpos_b = start - pb + jnp.arange(pb + T)

    oa = lax.map(lambda a: _dsa_one(a[0], a[1], a[2], a[3], a[4], a[5], pos, kpos_a),
                 (aq, ka_all, va_all, iq, ika_all, iw))
    ob = lax.map(lambda a: _band_one(a[0], a[1], a[2], pos, kpos_b, rel_bias),
                 (bq, kb_all, vb_all))

    ga, gb = jnp.split(jax.nn.sigmoid(gates.astype(jnp.float32)).astype(x.dtype), 2, axis=-1)
    merged = ga * (oa.reshape(B, T, -1) @ w_br_a) + gb * (ob.reshape(B, T, -1) @ w_br_b)
    x = x + merged @ w_out
    y_ffn = _peer(_rmsnorm(x, g_ffn).reshape(B * T, D_MODEL), w_pq, c1, c2, u, v)
    x = x + y_ffn.reshape(B, T, D_MODEL)
    keep = min(BAND_PAST, T)
    return x, (ak, av, ik, bk[:, T - keep:], bv[:, T - keep:])


def setup_inputs(seed: int = 0) -> dict:
    key = jax.random.key(seed)
    ks = jax.random.split(key, 24)
    f32 = jnp.float32
    band_rows = min(BAND_PAST, PAST_LEN)
    wa = N_HEADS_A * HEAD_DIM
    wb = N_HEADS_B * HEAD_DIM

    def nrm(k, shape, scale):
        return jax.random.normal(k, shape, f32) * scale

    def gain(k, n):
        return 1.0 + 0.05 * jax.random.normal(k, (DEPTH, n), f32)

    return {
        'x_prompt': nrm(ks[0], (BATCH, SEQ, D_MODEL), 1.0),
        'x_sample': nrm(ks[1], (DEC_BATCH, DEC_SEQ, D_MODEL), 1.0),
        'cache_a_k': nrm(ks[2], (DEPTH, DEC_BATCH, PAST_LEN, N_HEADS_A, HEAD_DIM), 1.0),
        'cache_a_v': nrm(ks[3], (DEPTH, DEC_BATCH, PAST_LEN, N_HEADS_A, HEAD_DIM), 1.0),
        'cache_a_ik': nrm(ks[4], (DEPTH, DEC_BATCH, PAST_LEN, IDX_DIM), 1.0),
        'cache_b_k': nrm(ks[5], (DEPTH, DEC_BATCH, band_rows, N_HEADS_B, HEAD_DIM), 1.0),
        'cache_b_v': nrm(ks[6], (DEPTH, DEC_BATCH, band_rows, N_HEADS_B, HEAD_DIM), 1.0),
        'g_attn': gain(ks[7], D_MODEL),
        'w_in': nrm(ks[8], (DEPTH, D_MODEL, D_IN), D_MODEL ** -0.5),
        'g_qa': gain(ks[9], HEAD_DIM),
        'g_ka': gain(ks[10], HEAD_DIM),
        'g_ik': gain(ks[11], IDX_DIM),
        'g_qb': gain(ks[12], HEAD_DIM),
        'g_kb': gain(ks[13], HEAD_DIM),
        'rel_bias': nrm(ks[14], (DEPTH, N_HEADS_B, 2 * REL_CLIP + 1), 0.1),
        'w_br_a': nrm(ks[15], (DEPTH, wa, D_MODEL), wa ** -0.5),
        'w_br_b': nrm(ks[16], (DEPTH, wb, D_MODEL), wb ** -0.5),
        'w_out': nrm(ks[17], (DEPTH, D_MODEL, D_MODEL), D_MODEL ** -0.5),
        'g_ffn': gain(ks[18], D_MODEL),
        'w_pq': nrm(ks[19], (DEPTH, D_MODEL, PEER_HEADS * PEER_DQ), D_MODEL ** -0.5),
        'peer_c1': nrm(ks[20], (DEPTH, N_KEYS, PEER_DQ // 2), (PEER_DQ // 2) ** -0.5),
        'peer_c2': nrm(ks[21], (DEPTH, N_KEYS, PEER_DQ // 2), (PEER_DQ // 2) ** -0.5),
        'peer_u': nrm(ks[22], (DEPTH, N_EXPERTS, D_MODEL), D_MODEL ** -0.5),
        'peer_v': nrm(ks[23], (DEPTH, N_EXPERTS, D_MODEL), PEER_HEADS ** -0.5),
    }


def reference(x_prompt, x_sample, cache_a_k, cache_a_v, cache_a_ik, cache_b_k, cache_b_v,
              g_attn, w_in, g_qa, g_ka, g_ik, g_qb, g_kb, rel_bias, w_br_a, w_br_b, w_out,
              g_ffn, w_pq, peer_c1, peer_c2, peer_u, peer_v):
    xp = x_prompt
    xs = x_sample
    bp = x_prompt.shape[0]
    past_len = cache_a_k.shape[2]
    sp = []
    ss = []
    for l in range(DEPTH):
        w = (g_attn[l], w_in[l], g_qa[l], g_ka[l], g_ik[l], g_qb[l], g_kb[l], rel_bias[l],
             w_br_a[l], w_br_b[l], w_out[l], g_ffn[l], w_pq[l], peer_c1[l], peer_c2[l],
             peer_u[l], peer_v[l])
        xp, st_p = _layer(xp, 0,
                          jnp.zeros((bp, 0, N_HEADS_A, HEAD_DIM), xp.dtype),
                          jnp.zeros((bp, 0, N_HEADS_A, HEAD_DIM), xp.dtype),
                          jnp.zeros((bp, 0, IDX_DIM), xp.dtype),
                          jnp.zeros((bp, 0, N_HEADS_B, HEAD_DIM), xp.dtype),
                          jnp.zeros((bp, 0, N_HEADS_B, HEAD_DIM), xp.dtype),
                          *w)
        xs, st_s = _layer(xs, past_len, cache_a_k[l], cache_a_v[l], cache_a_ik[l],
                          cache_b_k[l], cache_b_v[l], *w)
        sp.append(st_p)
        ss.append(st_s)
    a_k_prompt = jnp.stack([s[0] for s in sp])
    a_v_prompt = jnp.stack([s[1] for s in sp])
    a_ik_prompt = jnp.stack([s[2] for s in sp])
    b_k_prompt = jnp.stack([s[3] for s in sp])
    b_v_prompt = jnp.stack([s[4] for s in sp])
    a_k_sample = jnp.stack([s[0] for s in ss])
    a_v_sample = jnp.stack([s[1] for s in ss])
    a_ik_sample = jnp.stack([s[2] for s in ss])
    b_k_sample = jnp.stack([s[3] for s in ss])
    b_v_sample = jnp.stack([s[4] for s in ss])
    return (xp, xs, a_k_prompt, a_v_prompt, a_ik_prompt, b_k_prompt, b_v_prompt,
            a_k_sample, a_v_sample, a_ik_sample, b_k_sample, b_v_sample)
```

```python
import functools

import jax
import jax.numpy as jnp
import numpy as np
from jax import lax
from jax.experimental import pallas as pl
from jax.experimental.pallas import tpu as pltpu

F32 = jnp.float32
BF16 = jnp.bfloat16
I32 = jnp.int32

D_MODEL = 1024
HEAD_DIM = 64
N_HEADS = 8
ATT_W = N_HEADS * HEAD_DIM
IDX_HEADS = 8
IDX_DIM = 32
IDX_W = 384
CHUNK = 64
TOPK_MAX = 256
BAND_PAST = 512
BAND_KEYS = BAND_PAST + 128
REL_CLIP = 128
PEER_HEADS = 8
PEER_DQ = 256
N_KEYS = 128
N_EXPERTS = N_KEYS * N_KEYS
PEER_TOPK = 16
ROPE_THETA = 10000.0
EPS = 1e-6
NEG = -1e30
INT_MIN = -(2 ** 31)

LANES = 128
KEY_TILE = 256
ROW_TILE = 512
SEL_TILE = 256
PEER_TOK = 512
PEER_EB = 1024
VMEM_LIMIT = 56 * 1024 * 1024


def _cparams(sem):
    return pltpu.CompilerParams(dimension_semantics=sem, vmem_limit_bytes=VMEM_LIMIT)


def _rope_chunks(y, cos, sin, half):
    lane = lax.broadcasted_iota(I32, (1, LANES), 1)
    lo = (lane % (2 * half)) < half
    out = []
    for c in range(y.shape[1] // LANES):
        sl = slice(c * LANES, (c + 1) * LANES)
        yc = y[:, sl]
        partner = jnp.where(lo, pltpu.roll(yc, LANES - half, 1), pltpu.roll(yc, half, 1))
        out.append(yc * cos[:, sl] + partner * sin[:, sl])
    return jnp.concatenate(out, axis=1)


def _proj_kernel(x_ref, gattn_ref, w1_ref, hd_ref, gh_ref, gidx_ref,
                 cosq_ref, sinq_ref, cosi_ref, sini_ref,
                 aqT_ref, ak_ref, akb_ref, av_ref, avT_ref, iqT_ref, ik_ref, ikb_ref,
                 iwT_ref, bq_ref, bk_ref, bkb_ref, bv_ref, bvb_ref):
    x = x_ref[...]
    ms = jnp.mean(x * x, axis=-1, keepdims=True)
    h = (x * lax.rsqrt(ms + EPS) * gattn_ref[...]).astype(BF16)

    def seg(i):
        return jnp.dot(h, w1_ref[:, i * ATT_W:(i + 1) * ATT_W], preferred_element_type=F32)

    def headnorm(y, gi):
        hms = jnp.dot((y * y).astype(BF16), hd_ref[...], preferred_element_type=F32)
        return y * lax.rsqrt(hms + EPS) * gh_ref[gi:gi + 1, :]

    cosq = cosq_ref[...]
    sinq = sinq_ref[...]

    aq = _rope_chunks(headnorm(seg(0), 0), cosq, sinq, HEAD_DIM // 2) * (HEAD_DIM ** -0.5)
    aqT_ref[...] = aq.T.astype(BF16)

    ak = _rope_chunks(headnorm(seg(1), 1), cosq, sinq, HEAD_DIM // 2)
    ak_ref[...] = ak
    akb_ref[...] = ak.astype(BF16)

    av = seg(2)
    av_ref[...] = av
    for c in range(av.shape[0] // KEY_TILE):
        avT_ref[c] = av[c * KEY_TILE:(c + 1) * KEY_TILE, :].T.astype(BF16)

    bq_ref[...] = (headnorm(seg(3), 2) * (HEAD_DIM ** -0.5)).astype(BF16)
    bk = headnorm(seg(4), 3)
    bk_ref[...] = bk
    bkb_ref[...] = bk.astype(BF16)
    bv = seg(5)
    bv_ref[...] = bv
    bvb_ref[...] = bv.astype(BF16)

    s = jnp.dot(h, w1_ref[:, 6 * ATT_W:6 * ATT_W + IDX_W], preferred_element_type=F32)
    lane = lax.broadcasted_iota(I32, (1, LANES), 1)
    is_ik = lane < IDX_DIM
    c2 = s[:, 2 * LANES:3 * LANES]
    ikms = jnp.sum(jnp.where(is_ik, c2 * c2, 0.0), axis=-1, keepdims=True) * (1.0 / IDX_DIM)
    c2n = c2 * jnp.where(is_ik, lax.rsqrt(ikms + EPS), 1.0) * gidx_ref[...]
    sn = jnp.concatenate([s[:, :2 * LANES], c2n], axis=1)
    r = _rope_chunks(sn, cosi_ref[...], sini_ref[...], IDX_DIM // 2)
    iqT_ref[...] = r[:, :2 * LANES].T.astype(BF16)
    c2r = r[:, 2 * LANES:]
    ik_ref[...] = c2r[:, :IDX_DIM]
    ikb_ref[...] = jnp.where(is_ik, c2r, 0.0).astype(BF16)
    iwT_ref[...] = c2r.T[IDX_DIM:IDX_DIM + IDX_HEADS, :]


def _proj(x, nb, tb, w1, gattn, hd, gh, gidx, cosq, sinq, cosi, sini):
    n = x.shape[0]
    tm = ROW_TILE
    tpb = tb // tm
    row = lambda i: (i, 0)
    tab = lambda i: (i % tpb, 0)
    const = lambda i: (0, 0)
    grp = lambda i: (i // tpb, 0, i % tpb)
    out_shape = (
        jax.ShapeDtypeStruct((nb, ATT_W, tb), BF16),
        jax.ShapeDtypeStruct((n, ATT_W), F32),
        jax.ShapeDtypeStruct((n, ATT_W), BF16),
        jax.ShapeDtypeStruct((n, ATT_W), F32),
        jax.ShapeDtypeStruct((nb, tb // KEY_TILE, ATT_W, KEY_TILE), BF16),
        jax.ShapeDtypeStruct((nb, 2 * LANES, tb), BF16),
        jax.ShapeDtypeStruct((n, IDX_DIM), F32),
        jax.ShapeDtypeStruct((n, LANES), BF16),
        jax.ShapeDtypeStruct((nb, IDX_HEADS, tb), F32),
        jax.ShapeDtypeStruct((n, ATT_W), BF16),
        jax.ShapeDtypeStruct((n, ATT_W), F32),
        jax.ShapeDtypeStruct((n, ATT_W), BF16),
        jax.ShapeDtypeStruct((n, ATT_W), F32),
        jax.ShapeDtypeStruct((n, ATT_W), BF16),
    )
    out_specs = (
        pl.BlockSpec((None, ATT_W, tm), grp),
        pl.BlockSpec((tm, ATT_W), row),
        pl.BlockSpec((tm, ATT_W), row),
        pl.BlockSpec((tm, ATT_W), row),
        pl.BlockSpec((None, tm // KEY_TILE, ATT_W, KEY_TILE),
                     lambda i: (i // tpb, i % tpb, 0, 0)),
        pl.BlockSpec((None, 2 * LANES, tm), grp),
        pl.BlockSpec((tm, IDX_DIM), row),
        pl.BlockSpec((tm, LANES), row),
        pl.BlockSpec((None, IDX_HEADS, tm), grp),
        pl.BlockSpec((tm, ATT_W), row),
        pl.BlockSpec((tm, ATT_W), row),
        pl.BlockSpec((tm, ATT_W), row),
        pl.BlockSpec((tm, ATT_W), row),
        pl.BlockSpec((tm, ATT_W), row),
    )
    in_specs = [
        pl.BlockSpec((tm, D_MODEL), row),
        pl.BlockSpec((1, D_MODEL), const),
        pl.BlockSpec(w1.shape, const),
        pl.BlockSpec(hd.shape, const),
        pl.BlockSpec(gh.shape, const),
        pl.BlockSpec(gidx.shape, const),
        pl.BlockSpec((tm, ATT_W), tab),
        pl.BlockSpec((tm, ATT_W), tab),
        pl.BlockSpec((tm, IDX_W), tab),
        pl.BlockSpec((tm, IDX_W), tab),
    ]
    return pl.pallas_call(
        _proj_kernel, out_shape=out_shape, grid=(n // tm,),
        in_specs=in_specs, out_specs=out_specs,
        compiler_params=_cparams(("parallel",)), name="proj",
    )(x, gattn, w1, hd, gh, gidx, cosq, sinq, cosi, sini)


def _dsa_kernel(qT_ref, iqT_ref, iwT_ref, k_ref, vT_ref, ikb_ref, tri_ref,
                o_ref, sc_ref, am_ref, s_ref, oT_ref, *, q0, n_valid, n_key_tiles, n_sel):
    j = pl.program_id(1)
    qbase = q0 + j * LANES
    n_t = jnp.minimum(lax.shift_right_logical(qbase + LANES + KEY_TILE - 1, 8), n_key_tiles)
    qchunk = lax.shift_right_logical(
        qbase + lax.broadcasted_iota(I32, (1, LANES), 1), 6)

    def rows(t):
        return pl.ds(pl.multiple_of(t * KEY_TILE, KEY_TILE), KEY_TILE)

    def admissible(t):
        kidx = t * KEY_TILE + lax.broadcasted_iota(I32, (KEY_TILE, LANES), 0)
        return jnp.where(kidx < n_valid, lax.shift_right_logical(kidx, 6), 1 << 30) <= qchunk

    def fold8(v):
        return v.reshape(KEY_TILE // 8, 8, v.shape[-1])

    zpad = jnp.zeros((LANES - IDX_DIM, LANES), BF16)
    rhs_idx = jnp.concatenate(
        [jnp.concatenate([iqT_ref[h * IDX_DIM:(h + 1) * IDX_DIM, :], zpad], axis=0)
         for h in range(IDX_HEADS)], axis=1)

    def score_body(t, carry):
        res = jnp.dot(ikb_ref[rows(t), :], rhs_idx, preferred_element_type=F32)
        acc = jnp.zeros((KEY_TILE, LANES), F32)
        for h in range(IDX_HEADS):
            acc = acc + jnp.maximum(res[:, h * LANES:(h + 1) * LANES], 0.0) * iwT_ref[h:h + 1, :]
        sc_ref[rows(t), :] = jnp.where(admissible(t), acc, NEG)
        return carry

    lax.fori_loop(0, n_t, score_body, 0)

    def count(pred):
        def body(t, c8):
            return c8 + jnp.sum(fold8(jnp.where(pred(sc_ref[rows(t), :]), 1, 0)), axis=0)
        c8 = lax.fori_loop(0, n_t, body, jnp.zeros((8, LANES), I32))
        return jnp.sum(c8, axis=0, keepdims=True)

    def bisect_body(it, p):
        c = p + lax.shift_left(jnp.int32(1), 31 - it)
        cf = pltpu.bitcast(jnp.where(c >= 0, c, c ^ 0x7FFFFFFF), F32)
        return jnp.where(count(lambda v: v >= cf) >= n_sel, c, p)

    p = lax.fori_loop(0, 32, bisect_body, jnp.full((1, LANES), INT_MIN, I32))
    thr = pltpu.bitcast(jnp.where(p >= 0, p, p ^ 0x7FFFFFFF), F32)
    need = (n_sel - count(lambda v: v > thr)).astype(F32)

    def mask_body(t, carry):
        v = sc_ref[rows(t), :]
        eq = v == thr
        eqf = jnp.where(eq, 1.0, 0.0)
        pre = jnp.dot(tri_ref[...], eqf.astype(BF16), preferred_element_type=F32) + carry
        tie = jnp.where(eq, jnp.where(pre < need, 0.0, NEG), NEG)
        keep = jnp.where(v > thr, 0.0, tie)
        am_ref[rows(t), :] = jnp.where(admissible(t), keep, NEG)
        return carry + jnp.sum(jnp.sum(fold8(eqf), axis=0), axis=0, keepdims=True)

    lax.fori_loop(0, n_t, mask_body, jnp.zeros((1, LANES), F32))

    zq = jnp.zeros((HEAD_DIM, LANES), BF16)
    for pr in range(N_HEADS // 2):
        lo = pr * LANES
        rhs = jnp.concatenate(
            [jnp.concatenate([qT_ref[lo:lo + HEAD_DIM, :], zq], axis=1),
             jnp.concatenate([zq, qT_ref[lo + HEAD_DIM:lo + LANES, :]], axis=1)], axis=0)

        def logit_body(t, m8, lo=lo, rhs=rhs):
            s = jnp.dot(k_ref[rows(t), lo:lo + LANES], rhs, preferred_element_type=F32)
            am = am_ref[rows(t), :]
            s = s + jnp.concatenate([am, am], axis=1)
            s_ref[rows(t), :] = s
            return jnp.maximum(m8, jnp.max(fold8(s), axis=0))

        m8 = lax.fori_loop(0, n_t, logit_body, jnp.full((8, 2 * LANES), NEG, F32))
        m = jnp.max(m8, axis=0, keepdims=True)

        def pv_body(t, carry, lo=lo, m=m):
            l8, acc = carry
            pexp = jnp.exp(s_ref[rows(t), :] - m)
            l8 = l8 + jnp.sum(fold8(pexp), axis=0)
            acc = acc + jnp.dot(vT_ref[t, lo:lo + LANES, :], pexp.astype(BF16),
                                preferred_element_type=F32)
            return l8, acc

        l8, acc = lax.fori_loop(
            0, n_t, pv_body,
            (jnp.zeros((8, 2 * LANES), F32), jnp.zeros((LANES, 2 * LANES), F32)))
        inv = 1.0 / jnp.sum(l8, axis=0, keepdims=True)
        oT_ref[lo:lo + HEAD_DIM, :] = acc[:HEAD_DIM, :LANES] * inv[:, :LANES]
        oT_ref[lo + HEAD_DIM:lo + LANES, :] = acc[HEAD_DIM:, LANES:] * inv[:, LANES:]

    o_ref[...] = oT_ref[...].T.astype(BF16)


def _dsa(qT, iqT, iwT, k, vT, ikb, tri, *, q0, n_valid):
    nb, _, tq = qT.shape
    lp = k.shape[1]
    qblk = lambda b, j: (b, 0, j)
    seq3 = lambda b, j: (b, 0, 0)
    seq4 = lambda b, j: (b, 0, 0, 0)
    kern = functools.partial(_dsa_kernel, q0=q0, n_valid=n_valid, n_key_tiles=lp // KEY_TILE,
                             n_sel=min(TOPK_MAX, n_valid // 4))
    return pl.pallas_call(
        kern, out_shape=jax.ShapeDtypeStruct((nb, tq, ATT_W), BF16),
        grid=(nb, tq // LANES),
        in_specs=[
            pl.BlockSpec((None, ATT_W, LANES), qblk),
            pl.BlockSpec((None, 2 * LANES, LANES), qblk),
            pl.BlockSpec((None, IDX_HEADS, LANES), qblk),
            pl.BlockSpec((None, lp, ATT_W), seq3),
            pl.BlockSpec((None, lp // KEY_TILE, ATT_W, KEY_TILE), seq4),
            pl.BlockSpec((None, lp, LANES), seq3),
            pl.BlockSpec((KEY_TILE, KEY_TILE), lambda b, j: (0, 0)),
        ],
        out_specs=pl.BlockSpec((None, LANES, ATT_W), lambda b, j: (b, j, 0)),
        scratch_shapes=[
            pltpu.VMEM((lp, LANES), F32),
            pltpu.VMEM((lp, LANES), F32),
            pltpu.VMEM((lp, 2 * LANES), F32),
            pltpu.VMEM((ATT_W, LANES), F32),
        ],
        compiler_params=_cparams(("parallel", "arbitrary")), name="dsa",
    )(qT, iqT, iwT, k, vT, ikb, tri)


def _band_kernel(q_ref, k0, k1, k2, k3, k4, v0, v1, v2, v3, v4, bias_ref, o_ref,
                 *, row_lo, row_hi):
    j = pl.program_id(1)
    kb = jnp.concatenate([r[...] for r in (k0, k1, k2, k3, k4)], axis=0)
    vb = jnp.concatenate([r[...] for r in (v0, v1, v2, v3, v4)], axis=0)
    row = j * LANES + lax.broadcasted_iota(I32, (1, BAND_KEYS), 1)
    valid = jnp.where(row >= row_lo, row, row_hi) < row_hi
    lane = lax.broadcasted_iota(I32, (1, LANES), 1)
    lo_half = lane < HEAD_DIM
    zero = jnp.zeros((), BF16)
    for pr in range(N_HEADS // 2):
        sl = slice(pr * LANES, (pr + 1) * LANES)
        qp = q_ref[:, sl]
        lhs = jnp.concatenate([jnp.where(lo_half, qp, zero), jnp.where(lo_half, zero, qp)], axis=0)
        s = lax.dot_general(lhs, kb[:, sl], (((1,), (1,)), ((), ())),
                            preferred_element_type=F32)
        bias2 = jnp.concatenate([bias_ref[2 * pr], bias_ref[2 * pr + 1]], axis=0)
        s = jnp.where(valid, s + bias2, NEG)
        m = jnp.max(s, axis=-1, keepdims=True)
        pexp = jnp.exp(s - m)
        inv = 1.0 / jnp.sum(pexp, axis=-1, keepdims=True)
        o = jnp.dot(pexp.astype(BF16), vb[:, sl], preferred_element_type=F32) * inv
        o_ref[:, sl] = jnp.where(lo_half, o[:LANES], o[LANES:]).astype(BF16)


def _band(q, kp, vp, bias, *, row_lo, row_hi):
    nb, tq, _ = q.shape
    kspecs = [pl.BlockSpec((None, LANES, ATT_W), (lambda b, j, c=c: (b, j + c, 0)))
              for c in range(BAND_KEYS // LANES)]
    kern = functools.partial(_band_kernel, row_lo=row_lo, row_hi=row_hi)
    return pl.pallas_call(
        kern, out_shape=jax.ShapeDtypeStruct((nb, tq, ATT_W), BF16),
        grid=(nb, tq // LANES),
        in_specs=[pl.BlockSpec((None, LANES, ATT_W), lambda b, j: (b, j, 0))]
        + kspecs + kspecs
        + [pl.BlockSpec(bias.shape, lambda b, j: (0, 0, 0))],
        out_specs=pl.BlockSpec((None, LANES, ATT_W), lambda b, j: (b, j, 0)),
        compiler_params=_cparams(("parallel", "parallel")), name="band",
    )(q, *([kp] * 5), *([vp] * 5), bias)


def _merge_kernel(x_ref, oa_ref, ob_ref, gattn_ref, wg_ref, wa_ref, wb_ref, wo_ref,
                  gffn_ref, wpqT_ref, c1_ref, c2_ref,
                  x2_ref, h2T_ref, s1_ref, s2_ref):
    x = x_ref[...]
    ms = jnp.mean(x * x, axis=-1, keepdims=True)
    h = (x * lax.rsqrt(ms + EPS) * gattn_ref[...]).astype(BF16)
    ga = jax.nn.sigmoid(jnp.dot(h, wg_ref[:, :D_MODEL], preferred_element_type=F32))
    gb = jax.nn.sigmoid(jnp.dot(h, wg_ref[:, D_MODEL:], preferred_element_type=F32))
    ma = jnp.dot(oa_ref[...], wa_ref[...], preferred_element_type=F32)
    mb = jnp.dot(ob_ref[...], wb_ref[...], preferred_element_type=F32)
    merged = (ga * ma + gb * mb).astype(BF16)
    x2 = x + jnp.dot(merged, wo_ref[...], preferred_element_type=F32)
    x2_ref[...] = x2
    ms2 = jnp.mean(x2 * x2, axis=-1, keepdims=True)
    h2T = (x2 * lax.rsqrt(ms2 + EPS) * gffn_ref[...]).T.astype(BF16)
    h2T_ref[...] = h2T
    half = PEER_DQ // 2
    for hh in range(PEER_HEADS):
        qT = jnp.dot(wpqT_ref[hh * PEER_DQ:(hh + 1) * PEER_DQ, :], h2T,
                     preferred_element_type=F32).astype(BF16)
        s1_ref[hh] = jnp.dot(c1_ref[...], qT[:half], preferred_element_type=F32)
        s2_ref[hh] = jnp.dot(c2_ref[...], qT[half:], preferred_element_type=F32)


def _merge(x, oa, ob, gattn, wg, wa, wb, wo, gffn, wpqT, c1, c2):
    n = x.shape[0]
    tm = ROW_TILE
    row = lambda i: (i, 0)
    const = lambda i: (0, 0)
    return pl.pallas_call(
        _merge_kernel,
        out_shape=(
            jax.ShapeDtypeStruct((n, D_MODEL), F32),
            jax.ShapeDtypeStruct((D_MODEL, n), BF16),
            jax.ShapeDtypeStruct((PEER_HEADS, N_KEYS, n), F32),
            jax.ShapeDtypeStruct((PEER_HEADS, N_KEYS, n), F32),
        ),
        grid=(n // tm,),
        in_specs=[
            pl.BlockSpec((tm, D_MODEL), row),
            pl.BlockSpec((tm, ATT_W), row),
            pl.BlockSpec((tm, ATT_W), row),
            pl.BlockSpec((1, D_MODEL), const),
            pl.BlockSpec(wg.shape, const),
            pl.BlockSpec(wa.shape, const),
            pl.BlockSpec(wb.shape, const),
            pl.BlockSpec(wo.shape, const),
            pl.BlockSpec((1, D_MODEL), const),
            pl.BlockSpec(wpqT.shape, const),
            pl.BlockSpec(c1.shape, const),
            pl.BlockSpec(c2.shape, const),
        ],
        out_specs=(
            pl.BlockSpec((tm, D_MODEL), row),
            pl.BlockSpec((D_MODEL, tm), lambda i: (0, i)),
            pl.BlockSpec((PEER_HEADS, N_KEYS, tm), lambda i: (0, 0, i)),
            pl.BlockSpec((PEER_HEADS, N_KEYS, tm), lambda i: (0, 0, i)),
        ),
        compiler_params=_cparams(("parallel",)), name="merge",
    )(x, oa, ob, gattn, wg, wa, wb, wo, gffn, wpqT, c1, c2)


def _top16_rows(x):
    idx = lax.broadcasted_iota(I32, x.shape, 0)
    rows = []
    for _ in range(PEER_TOPK):
        m = jnp.max(x, axis=0, keepdims=True)
        rows.append(m)
        first = jnp.min(jnp.where(x == m, idx, N_KEYS), axis=0, keepdims=True)
        x = jnp.where(idx == first, -jnp.inf, x)
    return jnp.concatenate(rows, axis=0)


def _select_kernel(s1_ref, s2_ref, e1_ref, e2_ref, tau_ref):
    sub8 = lax.broadcasted_iota(I32, (8, 1), 0)

    def head_body(hh, carry):
        s1 = s1_ref[hh]
        s2 = s2_ref[hh]
        v1 = _top16_rows(s1)
        v2 = _top16_rows(s2)
        pieces = [v1 + v2[0:1], ]
        for b, amax in ((1, 8), (2, 5), (3, 4), (4, 3), (5, 2), (6, 2), (7, 2)):
            piece = v1[0:8] + v2[b:b + 1]
            pieces.append(piece if amax == 8 else jnp.where(sub8 < amax, piece, -jnp.inf))
        pieces.append(v1[0:1] + v2[8:16])
        cand = jnp.concatenate(pieces, axis=0)
        rem = jnp.full((1, cand.shape[1]), PEER_TOPK, I32)
        tau = jnp.zeros((1, cand.shape[1]), F32)
        x = cand
        for _ in range(PEER_TOPK):
            m = jnp.max(x, axis=0, keepdims=True)
            hit = x == m
            tau = jnp.where(rem > 0, m, tau)
            rem = rem - jnp.sum(jnp.where(hit, 1, 0), axis=0, keepdims=True)
            x = jnp.where(hit, -jnp.inf, x)
        m1 = v1[0:1]
        m2 = v2[0:1]
        z = jnp.sum(jnp.where(cand >= tau, jnp.exp(cand - (m1 + m2)), 0.0), axis=0, keepdims=True)
        e1_ref[hh] = jnp.exp(s1 - m1) * (1.0 / z)
        e2_ref[hh] = jnp.exp(s2 - m2)
        tau_ref[pl.ds(hh, 1), :] = tau
        return carry

    lax.fori_loop(0, PEER_HEADS, head_body, 0)


def _select(s1, s2):
    n = s1.shape[-1]
    tn = SEL_TILE
    blk = pl.BlockSpec((PEER_HEADS, N_KEYS, tn), lambda i: (0, 0, i))
    return pl.pallas_call(
        _select_kernel,
        out_shape=(
            jax.ShapeDtypeStruct(s1.shape, F32),
            jax.ShapeDtypeStruct(s1.shape, F32),
            jax.ShapeDtypeStruct((PEER_HEADS, n), F32),
        ),
        grid=(n // tn,),
        in_specs=[blk, blk],
        out_specs=(blk, blk, pl.BlockSpec((PEER_HEADS, tn), lambda i: (0, i))),
        compiler_params=_cparams(("parallel",)), name="select",
    )(s1, s2)


def _peer_kernel(h2T_ref, u_ref, vT_ref, s1_ref, s2_ref, e1_ref, e2_ref, tau_ref, x2_ref,
                 y_ref, acc_ref, wa_ref):
    e = pl.program_id(1)

    @pl.when(e == 0)
    def _():
        acc_ref[...] = jnp.zeros_like(acc_ref)

    tn = h2T_ref.shape[1]
    for sub in range(PEER_EB // N_KEYS):
        z = jnp.dot(u_ref[sub * N_KEYS:(sub + 1) * N_KEYS, :], h2T_ref[...],
                    preferred_element_type=F32)
        for c in range(tn // LANES):
            cs = slice(c * LANES, (c + 1) * LANES)
            zc = z[:, cs]
            a = 0.5 * zc * (1.0 + lax.erf(zc * (2.0 ** -0.5)))
            w = jnp.zeros((N_KEYS, LANES), F32)
            for hh in range(PEER_HEADS):
                s1r = s1_ref[hh, sub:sub + 1, cs]
                e1r = e1_ref[hh, sub:sub + 1, cs]
                hit = (s1r + s2_ref[hh, :, cs]) >= tau_ref[hh:hh + 1, cs]
                w = w + jnp.where(hit, e1r * e2_ref[hh, :, cs], 0.0)
            wa_ref[sub * N_KEYS:(sub + 1) * N_KEYS, cs] = (w * a).astype(BF16)
    acc_ref[...] += jnp.dot(vT_ref[...], wa_ref[...], preferred_element_type=F32)

    @pl.when(e == pl.num_programs(1) - 1)
    def _():
        y_ref[...] = x2_ref[...] + acc_ref[...].T


def _peer(h2T, u, vT, s1, s2, e1, e2, tau, x2):
    n = x2.shape[0]
    tn = PEER_TOK
    tok3 = pl.BlockSpec((PEER_HEADS, N_KEYS, tn), lambda i, e: (0, 0, i))
    key1 = pl.BlockSpec((PEER_HEADS, PEER_EB // N_KEYS, tn), lambda i, e: (0, e, i))
    return pl.pallas_call(
        _peer_kernel,
        out_shape=jax.ShapeDtypeStruct((n, D_MODEL), F32),
        grid=(n // tn, N_EXPERTS // PEER_EB),
        in_specs=[
            pl.BlockSpec((D_MODEL, tn), lambda i, e: (0, i)),
            pl.BlockSpec((PEER_EB, D_MODEL), lambda i, e: (e, 0)),
            pl.BlockSpec((D_MODEL, PEER_EB), lambda i, e: (0, e)),
            key1, tok3, key1, tok3,
            pl.BlockSpec((PEER_HEADS, tn), lambda i, e: (0, i)),
            pl.BlockSpec((tn, D_MODEL), lambda i, e: (i, 0)),
        ],
        out_specs=pl.BlockSpec((tn, D_MODEL), lambda i, e: (i, 0)),
        scratch_shapes=[
            pltpu.VMEM((D_MODEL, tn), F32),
            pltpu.VMEM((PEER_EB, tn), BF16),
        ],
        compiler_params=_cparams(("parallel", "arbitrary")), name="peer",
    )(h2T, u, vT, s1, s2, e1, e2, tau, x2)


def _rope_tables(pos):
    posf = pos.astype(F32)[:, None]
    t = pos.shape[0]

    def tab(half, reps):
        inv = jnp.power(ROPE_THETA, -jnp.arange(half, dtype=F32) / half)
        ang = posf * inv[None, :]
        cos = jnp.cos(ang)
        sin = jnp.sin(ang)
        return (jnp.tile(jnp.concatenate([cos, cos], axis=1), (1, reps)),
                jnp.tile(jnp.concatenate([-sin, sin], axis=1), (1, reps)))

    cosq, sinq = tab(HEAD_DIM // 2, N_HEADS)
    cosi, sini = tab(IDX_DIM // 2, IDX_HEADS + 1)
    pad = IDX_W - cosi.shape[1]
    cosi = jnp.concatenate([cosi, jnp.ones((t, pad), F32)], axis=1)
    sini = jnp.concatenate([sini, jnp.zeros((t, pad), F32)], axis=1)
    return cosq, sinq, cosi, sini


def _band_bias(rel_bias):
    qi = np.arange(LANES)[:, None]
    kj = np.arange(BAND_KEYS)[None, :]
    rel = np.clip(BAND_PAST + qi - kj, -REL_CLIP, REL_CLIP) + REL_CLIP
    ok = (kj // CHUNK >= qi // CHUNK) & (kj // CHUNK <= qi // CHUNK + BAND_PAST // CHUNK)
    return jnp.where(jnp.asarray(ok)[None], rel_bias.astype(F32)[:, rel], NEG)


def _layer_weights(g_attn, w_in, g_qa, g_ka, g_ik, g_qb, g_kb, rel_bias, w_br_a, w_br_b,
                   w_out, g_ffn, w_pq, c1, c2, u, v):
    a0 = 3 * ATT_W
    i0 = a0 + IDX_HEADS * IDX_DIM + IDX_DIM + IDX_HEADS
    b0 = i0 + 3 * ATT_W
    idx_cols = i0 - a0
    w1 = jnp.concatenate(
        [w_in[:, :a0], w_in[:, i0:b0], w_in[:, a0:i0],
         jnp.zeros((D_MODEL, IDX_W - idx_cols), w_in.dtype)], axis=1).astype(BF16)
    wg = w_in[:, b0:].astype(BF16)
    hd = jnp.asarray(np.kron(np.eye(N_HEADS), np.ones((HEAD_DIM, HEAD_DIM))) / HEAD_DIM, BF16)
    gh = jnp.stack([jnp.tile(g, N_HEADS) for g in (g_qa, g_ka, g_qb, g_kb)]).astype(F32)
    gidx = jnp.concatenate(
        [g_ik.astype(F32), jnp.full((IDX_HEADS,), IDX_HEADS ** -0.5, F32),
         jnp.ones((LANES - IDX_DIM - IDX_HEADS,), F32)])[None, :]
    tri = jnp.asarray(np.tril(np.ones((KEY_TILE, KEY_TILE)), -1), BF16)
    return dict(
        w1=w1, wg=wg, hd=hd, gh=gh, gidx=gidx, tri=tri,
        gattn=g_attn.astype(F32)[None, :], gffn=g_ffn.astype(F32)[None, :],
        bias=_band_bias(rel_bias),
        wa=w_br_a.astype(BF16), wb=w_br_b.astype(BF16), wo=w_out.astype(BF16),
        wpqT=w_pq.T.astype(BF16), c1=c1.astype(BF16), c2=c2.astype(BF16),
        u=u.astype(BF16), vT=v.T.astype(BF16))


def _ffn(x, oa, ob, w):
    x2, h2T, s1, s2 = _merge(x, oa, ob, w["gattn"], w["wg"], w["wa"], w["wb"], w["wo"],
                             w["gffn"], w["wpqT"], w["c1"], w["c2"])
    e1, e2, tau = _select(s1, s2)
    return _peer(h2T, w["u"], w["vT"], s1, s2, e1, e2, tau, x2)


def _prompt_layer(x, w):
    b, t, _ = x.shape
    xf = x.reshape(b * t, D_MODEL)
    tabs = _rope_tables(jnp.arange(t))
    (aqT, ak, akb, av, avT, iqT, ik, ikb, iwT, bq, bk, bkb, bv, bvb) = _proj(
        xf, b, t, w["w1"], w["gattn"], w["hd"], w["gh"], w["gidx"], *tabs)
    oa = _dsa(aqT, iqT, iwT, akb.reshape(b, t, ATT_W), avT, ikb.reshape(b, t, LANES),
              w["tri"], q0=0, n_valid=t)
    front = ((0, 0), (BAND_PAST, 0), (0, 0))
    ob = _band(bq.reshape(b, t, ATT_W), jnp.pad(bkb.reshape(b, t, ATT_W), front),
               jnp.pad(bvb.reshape(b, t, ATT_W), front), w["bias"],
               row_lo=BAND_PAST, row_hi=BAND_PAST + t)
    y = _ffn(xf, oa.reshape(b * t, ATT_W), ob.reshape(b * t, ATT_W), w)
    keep = min(BAND_PAST, t)
    heads = lambda a: a.reshape(b, t, N_HEADS, HEAD_DIM)
    return y.reshape(b, t, D_MODEL), (
        heads(ak), heads(av), ik.reshape(b, t, IDX_DIM),
        heads(bk)[:, t - keep:], heads(bv)[:, t - keep:])


def _sample_layer(x, ca_k, ca_v, ca_ik, cb_k, cb_v, w):
    b, t, _ = x.shape
    n = b * t
    past = ca_k.shape[1]
    xf = x.reshape(n, D_MODEL)
    tabs = _rope_tables(jnp.tile(past + jnp.arange(t), b))
    (aqT, ak, akb, av, avT, iqT, ik, ikb, iwT, bq, bk, bkb, bv, bvb) = _proj(
        xf, 1, n, w["w1"], w["gattn"], w["hd"], w["gh"], w["gidx"], *tabs)

    def per_seq_T(aT, rows):
        a = aT[0].T.reshape(b, t, rows)
        return jnp.pad(a, ((0, 0), (0, LANES - t), (0, 0))).transpose(0, 2, 1)

    lk = past + t
    lp = -(-lk // KEY_TILE) * KEY_TILE
    padk = ((0, 0), (0, lp - lk), (0, 0))
    k_all = jnp.pad(jnp.concatenate(
        [ca_k.reshape(b, past, ATT_W).astype(BF16), akb.reshape(b, t, ATT_W)], axis=1), padk)
    v_all = jnp.pad(jnp.concatenate(
        [ca_v.reshape(b, past, ATT_W).astype(BF16), av.astype(BF16).reshape(b, t, ATT_W)],
        axis=1), padk)
    vT_all = v_all.reshape(b, lp // KEY_TILE, KEY_TILE, ATT_W).transpose(0, 1, 3, 2)
    ik_all = jnp.pad(jnp.concatenate(
        [jnp.pad(ca_ik.astype(BF16), ((0, 0), (0, 0), (0, LANES - IDX_DIM))),
         ikb.reshape(b, t, LANES)], axis=1), padk)
    oa = _dsa(per_seq_T(aqT, ATT_W), per_seq_T(iqT, 2 * LANES), per_seq_T(iwT, IDX_HEADS),
              k_all, vT_all, ik_all, w["tri"], q0=past, n_valid=lk)[:, :t]

    pb = cb_k.shape[1]
    padb = ((0, 0), (BAND_PAST - pb, LANES - t), (0, 0))
    qpad = jnp.pad(bq.reshape(b, t, ATT_W), ((0, 0), (0, LANES - t), (0, 0)))
    kb_all = jnp.pad(jnp.concatenate(
        [cb_k.reshape(b, pb, ATT_W).astype(BF16), bkb.reshape(b, t, ATT_W)], axis=1), padb)
    vb_all = jnp.pad(jnp.concatenate(
        [cb_v.reshape(b, pb, ATT_W).astype(BF16), bvb.reshape(b, t, ATT_W)], axis=1), padb)
    ob = _band(qpad, kb_all, vb_all, w["bias"],
               row_lo=BAND_PAST - pb, row_hi=BAND_PAST + t)[:, :t]

    y = _ffn(xf, oa.reshape(n, ATT_W), ob.reshape(n, ATT_W), w)
    keep = min(BAND_PAST, t)
    heads = lambda a: a.reshape(b, t, N_HEADS, HEAD_DIM)
    return y.reshape(b, t, D_MODEL), (
        heads(ak), heads(av), ik.reshape(b, t, IDX_DIM),
        heads(bk)[:, t - keep:], heads(bv)[:, t - keep:])


def kernel(x_prompt, x_sample, cache_a_k, cache_a_v, cache_a_ik, cache_b_k, cache_b_v,
           g_attn, w_in, g_qa, g_ka, g_ik, g_qb, g_kb, rel_bias, w_br_a, w_br_b, w_out,
           g_ffn, w_pq, peer_c1, peer_c2, peer_u, peer_v):
    depth = w_in.shape[0]
    xp, xs = x_prompt, x_sample
    sp, ss = [], []
    for l in range(depth):
        w = _layer_weights(g_attn[l], w_in[l], g_qa[l], g_ka[l], g_ik[l], g_qb[l], g_kb[l],
                           rel_bias[l], w_br_a[l], w_br_b[l], w_out[l], g_ffn[l], w_pq[l],
                           peer_c1[l], peer_c2[l], peer_u[l], peer_v[l])
        xp, st_p = _prompt_layer(xp, w)
        xs, st_s = _sample_layer(xs, cache_a_k[l], cache_a_v[l], cache_a_ik[l],
                                 cache_b_k[l], cache_b_v[l], w)
        sp.append(st_p)
        ss.append(st_s)
    stack = lambda sts, i: jnp.stack([s[i] for s in sts])
    return (xp, xs) + tuple(stack(sp, i) for i in range(5)) + tuple(stack(ss, i) for i in range(5))
```

```python
import functools

import jax
import jax.numpy as jnp
import numpy as np
from jax import lax
from jax.experimental import pallas as pl
from jax.experimental.pallas import tpu as pltpu

F32 = jnp.float32
BF16 = jnp.bfloat16
I32 = jnp.int32

D_MODEL = 1024
HEAD_DIM = 64
N_HEADS = 8
ATT_W = N_HEADS * HEAD_DIM
IDX_HEADS = 8
IDX_DIM = 32
IDX_W = 384
CHUNK = 64
TOPK_MAX = 256
BAND_PAST = 512
BAND_KEYS = BAND_PAST + 128
REL_CLIP = 128
PEER_HEADS = 8
PEER_DQ = 256
N_KEYS = 128
N_EXPERTS = N_KEYS * N_KEYS
PEER_TOPK = 16
ROPE_THETA = 10000.0
EPS = 1e-6
NEG = -1e30
INT_MIN = -(2 ** 31)

LANES = 128
BF16_ROWS = 16
KEY_TILE = 256
ROW_TILE = 512
SEL_TILE = 256
PEER_TOK = 512
PEER_EB = 1024
VMEM_LIMIT = 56 * 1024 * 1024


def _cparams(sem):
    return pltpu.CompilerParams(dimension_semantics=sem, vmem_limit_bytes=VMEM_LIMIT)


def _rope_chunks(y, cos, sin, half):
    lane = lax.broadcasted_iota(I32, (1, LANES), 1)
    lo = (lane % (2 * half)) < half
    out = []
    for c in range(y.shape[1] // LANES):
        sl = slice(c * LANES, (c + 1) * LANES)
        yc = y[:, sl]
        partner = jnp.where(lo, pltpu.roll(yc, LANES - half, 1), pltpu.roll(yc, half, 1))
        out.append(yc * cos[:, sl] + partner * sin[:, sl])
    return jnp.concatenate(out, axis=1)


def _proj_kernel(x_ref, gattn_ref, w1_ref, hd_ref, gh_ref, gidx_ref,
                 cosq_ref, sinq_ref, cosi_ref, sini_ref,
                 aqT_ref, ak_ref, akb_ref, av_ref, avT_ref, iqT_ref, ik_ref, ikb_ref,
                 iwT_ref, bq_ref, bk_ref, bkb_ref, bv_ref, bvb_ref):
    x = x_ref[...]
    ms = jnp.mean(x * x, axis=-1, keepdims=True)
    h = (x * lax.rsqrt(ms + EPS) * gattn_ref[...]).astype(BF16)

    def seg(i):
        return jnp.dot(h, w1_ref[:, i * ATT_W:(i + 1) * ATT_W], preferred_element_type=F32)

    def headnorm(y, gi):
        hms = jnp.dot((y * y).astype(BF16), hd_ref[...], preferred_element_type=F32)
        return y * lax.rsqrt(hms + EPS) * gh_ref[gi:gi + 1, :]

    cosq = cosq_ref[...]
    sinq = sinq_ref[...]

    aq = _rope_chunks(headnorm(seg(0), 0), cosq, sinq, HEAD_DIM // 2) * (HEAD_DIM ** -0.5)
    aqT_ref[...] = aq.T.astype(BF16)

    ak = _rope_chunks(headnorm(seg(1), 1), cosq, sinq, HEAD_DIM // 2)
    ak_ref[...] = ak
    akb_ref[...] = ak.astype(BF16)

    av = seg(2)
    av_ref[...] = av
    for c in range(av.shape[0] // KEY_TILE):
        avT_ref[c] = av[c * KEY_TILE:(c + 1) * KEY_TILE, :].T.astype(BF16)

    bq_ref[...] = (headnorm(seg(3), 2) * (HEAD_DIM ** -0.5)).astype(BF16)
    bk = headnorm(seg(4), 3)
    bk_ref[...] = bk
    bkb_ref[...] = bk.astype(BF16)
    bv = seg(5)
    bv_ref[...] = bv
    bvb_ref[...] = bv.astype(BF16)

    s = jnp.dot(h, w1_ref[:, 6 * ATT_W:6 * ATT_W + IDX_W], preferred_element_type=F32)
    lane = lax.broadcasted_iota(I32, (1, LANES), 1)
    is_ik = lane < IDX_DIM
    c2 = s[:, 2 * LANES:3 * LANES]
    ikms = jnp.sum(jnp.where(is_ik, c2 * c2, 0.0), axis=-1, keepdims=True) * (1.0 / IDX_DIM)
    c2n = c2 * jnp.where(is_ik, lax.rsqrt(ikms + EPS), 1.0) * gidx_ref[...]
    sn = jnp.concatenate([s[:, :2 * LANES], c2n], axis=1)
    r = _rope_chunks(sn, cosi_ref[...], sini_ref[...], IDX_DIM // 2)
    iqT_ref[...] = r[:, :2 * LANES].T.astype(BF16)
    c2r = r[:, 2 * LANES:]
    ik_ref[...] = c2r[:, :IDX_DIM]
    ikb_ref[...] = jnp.where(is_ik, c2r, 0.0).astype(BF16)
    iwT_ref[...] = c2r.T[IDX_DIM:IDX_DIM + IDX_HEADS, :]


def _proj(x, nb, tb, w1, gattn, hd, gh, gidx, cosq, sinq, cosi, sini):
    n = x.shape[0]
    tm = ROW_TILE
    tpb = tb // tm
    row = lambda i: (i, 0)
    tab = lambda i: (i % tpb, 0)
    const = lambda i: (0, 0)
    grp = lambda i: (i // tpb, 0, i % tpb)
    out_shape = (
        jax.ShapeDtypeStruct((nb, ATT_W, tb), BF16),
        jax.ShapeDtypeStruct((n, ATT_W), F32),
        jax.ShapeDtypeStruct((n, ATT_W), BF16),
        jax.ShapeDtypeStruct((n, ATT_W), F32),
        jax.ShapeDtypeStruct((nb, tb // KEY_TILE, ATT_W, KEY_TILE), BF16),
        jax.ShapeDtypeStruct((nb, 2 * LANES, tb), BF16),
        jax.ShapeDtypeStruct((n, IDX_DIM), F32),
        jax.ShapeDtypeStruct((n, LANES), BF16),
        jax.ShapeDtypeStruct((nb, IDX_HEADS, tb), F32),
        jax.ShapeDtypeStruct((n, ATT_W), BF16),
        jax.ShapeDtypeStruct((n, ATT_W), F32),
        jax.ShapeDtypeStruct((n, ATT_W), BF16),
        jax.ShapeDtypeStruct((n, ATT_W), F32),
        jax.ShapeDtypeStruct((n, ATT_W), BF16),
    )
    out_specs = (
        pl.BlockSpec((None, ATT_W, tm), grp),
        pl.BlockSpec((tm, ATT_W), row),
        pl.BlockSpec((tm, ATT_W), row),
        pl.BlockSpec((tm, ATT_W), row),
        pl.BlockSpec((None, tm // KEY_TILE, ATT_W, KEY_TILE),
                     lambda i: (i // tpb, i % tpb, 0, 0)),
        pl.BlockSpec((None, 2 * LANES, tm), grp),
        pl.BlockSpec((tm, IDX_DIM), row),
        pl.BlockSpec((tm, LANES), row),
        pl.BlockSpec((None, IDX_HEADS, tm), grp),
        pl.BlockSpec((tm, ATT_W), row),
        pl.BlockSpec((tm, ATT_W), row),
        pl.BlockSpec((tm, ATT_W), row),
        pl.BlockSpec((tm, ATT_W), row),
        pl.BlockSpec((tm, ATT_W), row),
    )
    in_specs = [
        pl.BlockSpec((tm, D_MODEL), row),
        pl.BlockSpec((1, D_MODEL), const),
        pl.BlockSpec(w1.shape, const),
        pl.BlockSpec(hd.shape, const),
        pl.BlockSpec(gh.shape, const),
        pl.BlockSpec(gidx.shape, const),
        pl.BlockSpec((tm, ATT_W), tab),
        pl.BlockSpec((tm, ATT_W), tab),
        pl.BlockSpec((tm, IDX_W), tab),
        pl.BlockSpec((tm, IDX_W), tab),
    ]
    return pl.pallas_call(
        _proj_kernel, out_shape=out_shape, grid=(n // tm,),
        in_specs=in_specs, out_specs=out_specs,
        compiler_params=_cparams(("parallel",)), name="proj",
    )(x, gattn, w1, hd, gh, gidx, cosq, sinq, cosi, sini)


def _dsa_kernel(qT_ref, iqT_ref, iwT_ref, k_ref, vT_ref, ikb_ref, tri_ref,
                o_ref, sc_ref, am_ref, s_ref, oT_ref, *, q0, n_valid, n_key_tiles, n_sel):
    j = pl.program_id(1)
    qbase = q0 + j * LANES
    n_t = jnp.minimum(lax.shift_right_logical(qbase + LANES + KEY_TILE - 1, 8), n_key_tiles)
    qchunk = lax.shift_right_logical(
        qbase + lax.broadcasted_iota(I32, (1, LANES), 1), 6)

    def rows(t):
        return pl.ds(pl.multiple_of(t * KEY_TILE, KEY_TILE), KEY_TILE)

    def admissible(t):
        kidx = t * KEY_TILE + lax.broadcasted_iota(I32, (KEY_TILE, LANES), 0)
        return jnp.where(kidx < n_valid, lax.shift_right_logical(kidx, 6), 1 << 30) <= qchunk

    def fold8(v):
        return v.reshape(KEY_TILE // 8, 8, v.shape[-1])

    zpad = jnp.zeros((LANES - IDX_DIM, LANES), BF16)
    rhs_idx = jnp.concatenate(
        [jnp.concatenate([iqT_ref[h * IDX_DIM:(h + 1) * IDX_DIM, :], zpad], axis=0)
         for h in range(IDX_HEADS)], axis=1)

    def score_body(t, carry):
        res = jnp.dot(ikb_ref[rows(t), :], rhs_idx, preferred_element_type=F32)
        acc = jnp.zeros((KEY_TILE, LANES), F32)
        for h in range(IDX_HEADS):
            acc = acc + jnp.maximum(res[:, h * LANES:(h + 1) * LANES], 0.0) * iwT_ref[h:h + 1, :]
        sc_ref[rows(t), :] = jnp.where(admissible(t), acc, NEG)
        return carry

    lax.fori_loop(0, n_t, score_body, 0)

    def count(pred):
        def body(t, c8):
            return c8 + jnp.sum(fold8(jnp.where(pred(sc_ref[rows(t), :]), 1, 0)), axis=0)
        c8 = lax.fori_loop(0, n_t, body, jnp.zeros((8, LANES), I32))
        return jnp.sum(c8, axis=0, keepdims=True)

    def bisect_body(it, p):
        c = p + lax.shift_left(jnp.int32(1), 31 - it)
        cf = pltpu.bitcast(jnp.where(c >= 0, c, c ^ 0x7FFFFFFF), F32)
        return jnp.where(count(lambda v: v >= cf) >= n_sel, c, p)

    p = lax.fori_loop(0, 32, bisect_body, jnp.full((1, LANES), INT_MIN, I32))
    thr = pltpu.bitcast(jnp.where(p >= 0, p, p ^ 0x7FFFFFFF), F32)
    need = (n_sel - count(lambda v: v > thr)).astype(F32)

    def mask_body(t, carry):
        v = sc_ref[rows(t), :]
        eq = v == thr
        eqf = jnp.where(eq, 1.0, 0.0)
        pre = jnp.dot(tri_ref[...], eqf.astype(BF16), preferred_element_type=F32) + carry
        tie = jnp.where(eq, jnp.where(pre < need, 0.0, NEG), NEG)
        keep = jnp.where(v > thr, 0.0, tie)
        am_ref[rows(t), :] = jnp.where(admissible(t), keep, NEG)
        return carry + jnp.sum(jnp.sum(fold8(eqf), axis=0), axis=0, keepdims=True)

    lax.fori_loop(0, n_t, mask_body, jnp.zeros((1, LANES), F32))

    zq = jnp.zeros((HEAD_DIM, LANES), BF16)
    for pr in range(N_HEADS // 2):
        lo = pr * LANES
        rhs = jnp.concatenate(
            [jnp.concatenate([qT_ref[lo:lo + HEAD_DIM, :], zq], axis=1),
             jnp.concatenate([zq, qT_ref[lo + HEAD_DIM:lo + LANES, :]], axis=1)], axis=0)

        def logit_body(t, m8, lo=lo, rhs=rhs):
            s = jnp.dot(k_ref[rows(t), lo:lo + LANES], rhs, preferred_element_type=F32)
            am = am_ref[rows(t), :]
            s = s + jnp.concatenate([am, am], axis=1)
            s_ref[rows(t), :] = s
            return jnp.maximum(m8, jnp.max(fold8(s), axis=0))

        m8 = lax.fori_loop(0, n_t, logit_body, jnp.full((8, 2 * LANES), NEG, F32))
        m = jnp.max(m8, axis=0, keepdims=True)

        def pv_body(t, carry, lo=lo, m=m):
            l8, acc = carry
            pexp = jnp.exp(s_ref[rows(t), :] - m)
            l8 = l8 + jnp.sum(fold8(pexp), axis=0)
            acc = acc + jnp.dot(vT_ref[t, lo:lo + LANES, :], pexp.astype(BF16),
                                preferred_element_type=F32)
            return l8, acc

        l8, acc = lax.fori_loop(
            0, n_t, pv_body,
            (jnp.zeros((8, 2 * LANES), F32), jnp.zeros((LANES, 2 * LANES), F32)))
        inv = 1.0 / jnp.sum(l8, axis=0, keepdims=True)
        oT_ref[lo:lo + HEAD_DIM, :] = acc[:HEAD_DIM, :LANES] * inv[:, :LANES]
        oT_ref[lo + HEAD_DIM:lo + LANES, :] = acc[HEAD_DIM:, LANES:] * inv[:, LANES:]

    o_ref[...] = oT_ref[...].T.astype(BF16)


def _dsa(qT, iqT, iwT, k, vT, ikb, tri, *, q0, n_valid):
    nb, _, tq = qT.shape
    lp = k.shape[1]
    qblk = lambda b, j: (b, 0, j)
    seq3 = lambda b, j: (b, 0, 0)
    seq4 = lambda b, j: (b, 0, 0, 0)
    kern = functools.partial(_dsa_kernel, q0=q0, n_valid=n_valid, n_key_tiles=lp // KEY_TILE,
                             n_sel=min(TOPK_MAX, n_valid // 4))
    return pl.pallas_call(
        kern, out_shape=jax.ShapeDtypeStruct((nb, tq, ATT_W), BF16),
        grid=(nb, tq // LANES),
        in_specs=[
            pl.BlockSpec((None, ATT_W, LANES), qblk),
            pl.BlockSpec((None, 2 * LANES, LANES), qblk),
            pl.BlockSpec((None, IDX_HEADS, LANES), qblk),
            pl.BlockSpec((None, lp, ATT_W), seq3),
            pl.BlockSpec((None, lp // KEY_TILE, ATT_W, KEY_TILE), seq4),
            pl.BlockSpec((None, lp, LANES), seq3),
            pl.BlockSpec((KEY_TILE, KEY_TILE), lambda b, j: (0, 0)),
        ],
        out_specs=pl.BlockSpec((None, LANES, ATT_W), lambda b, j: (b, j, 0)),
        scratch_shapes=[
            pltpu.VMEM((lp, LANES), F32),
            pltpu.VMEM((lp, LANES), F32),
            pltpu.VMEM((lp, 2 * LANES), F32),
            pltpu.VMEM((ATT_W, LANES), F32),
        ],
        compiler_params=_cparams(("parallel", "arbitrary")), name="dsa",
    )(qT, iqT, iwT, k, vT, ikb, tri)


def _band_kernel(q_ref, k0, k1, k2, k3, k4, v0, v1, v2, v3, v4, bias_ref, o_ref,
                 *, row_lo, row_hi):
    j = pl.program_id(1)
    kb = jnp.concatenate([r[...] for r in (k0, k1, k2, k3, k4)], axis=0)
    vb = jnp.concatenate([r[...] for r in (v0, v1, v2, v3, v4)], axis=0)
    row = j * LANES + lax.broadcasted_iota(I32, (1, BAND_KEYS), 1)
    valid = jnp.where(row >= row_lo, row, row_hi) < row_hi
    lane = lax.broadcasted_iota(I32, (1, LANES), 1)
    lo_half = lane < HEAD_DIM
    zero = jnp.zeros((), BF16)
    for pr in range(N_HEADS // 2):
        sl = slice(pr * LANES, (pr + 1) * LANES)
        qp = q_ref[:, sl]
        lhs = jnp.concatenate([jnp.where(lo_half, qp, zero), jnp.where(lo_half, zero, qp)], axis=0)
        s = lax.dot_general(lhs, kb[:, sl], (((1,), (1,)), ((), ())),
                            preferred_element_type=F32)
        bias2 = jnp.concatenate([bias_ref[2 * pr], bias_ref[2 * pr + 1]], axis=0)
        s = jnp.where(valid, s + bias2, NEG)
        m = jnp.max(s, axis=-1, keepdims=True)
        pexp = jnp.exp(s - m)
        inv = 1.0 / jnp.sum(pexp, axis=-1, keepdims=True)
        o = jnp.dot(pexp.astype(BF16), vb[:, sl], preferred_element_type=F32) * inv
        o_ref[:, sl] = jnp.where(lo_half, o[:LANES], o[LANES:]).astype(BF16)


def _band(q, kp, vp, bias, *, row_lo, row_hi):
    nb, tq, _ = q.shape
    kspecs = [pl.BlockSpec((None, LANES, ATT_W), (lambda b, j, c=c: (b, j + c, 0)))
              for c in range(BAND_KEYS // LANES)]
    kern = functools.partial(_band_kernel, row_lo=row_lo, row_hi=row_hi)
    return pl.pallas_call(
        kern, out_shape=jax.ShapeDtypeStruct((nb, tq, ATT_W), BF16),
        grid=(nb, tq // LANES),
        in_specs=[pl.BlockSpec((None, LANES, ATT_W), lambda b, j: (b, j, 0))]
        + kspecs + kspecs
        + [pl.BlockSpec(bias.shape, lambda b, j: (0, 0, 0))],
        out_specs=pl.BlockSpec((None, LANES, ATT_W), lambda b, j: (b, j, 0)),
        compiler_params=_cparams(("parallel", "parallel")), name="band",
    )(q, *([kp] * 5), *([vp] * 5), bias)


def _merge_kernel(x_ref, oa_ref, ob_ref, gattn_ref, wg_ref, wa_ref, wb_ref, wo_ref,
                  gffn_ref, wpqT_ref, c1_ref, c2_ref,
                  x2_ref, h2T_ref, s1_ref, s2_ref):
    x = x_ref[...]
    ms = jnp.mean(x * x, axis=-1, keepdims=True)
    h = (x * lax.rsqrt(ms + EPS) * gattn_ref[...]).astype(BF16)
    ga = jax.nn.sigmoid(jnp.dot(h, wg_ref[:, :D_MODEL], preferred_element_type=F32))
    gb = jax.nn.sigmoid(jnp.dot(h, wg_ref[:, D_MODEL:], preferred_element_type=F32))
    ma = jnp.dot(oa_ref[...], wa_ref[...], preferred_element_type=F32)
    mb = jnp.dot(ob_ref[...], wb_ref[...], preferred_element_type=F32)
    merged = (ga * ma + gb * mb).astype(BF16)
    x2 = x + jnp.dot(merged, wo_ref[...], preferred_element_type=F32)
    x2_ref[...] = x2
    ms2 = jnp.mean(x2 * x2, axis=-1, keepdims=True)
    h2T = (x2 * lax.rsqrt(ms2 + EPS) * gffn_ref[...]).T.astype(BF16)
    h2T_ref[...] = h2T
    half = PEER_DQ // 2
    for hh in range(PEER_HEADS):
        qT = jnp.dot(wpqT_ref[hh * PEER_DQ:(hh + 1) * PEER_DQ, :], h2T,
                     preferred_element_type=F32).astype(BF16)
        s1_ref[hh] = jnp.dot(c1_ref[...], qT[:half], preferred_element_type=F32)
        s2_ref[hh] = jnp.dot(c2_ref[...], qT[half:], preferred_element_type=F32)


def _merge(x, oa, ob, gattn, wg, wa, wb, wo, gffn, wpqT, c1, c2):
    n = x.shape[0]
    tm = ROW_TILE
    row = lambda i: (i, 0)
    const = lambda i: (0, 0)
    return pl.pallas_call(
        _merge_kernel,
        out_shape=(
            jax.ShapeDtypeStruct((n, D_MODEL), F32),
            jax.ShapeDtypeStruct((D_MODEL, n), BF16),
            jax.ShapeDtypeStruct((PEER_HEADS, N_KEYS, n), F32),
            jax.ShapeDtypeStruct((PEER_HEADS, N_KEYS, n), F32),
        ),
        grid=(n // tm,),
        in_specs=[
            pl.BlockSpec((tm, D_MODEL), row),
            pl.BlockSpec((tm, ATT_W), row),
            pl.BlockSpec((tm, ATT_W), row),
            pl.BlockSpec((1, D_MODEL), const),
            pl.BlockSpec(wg.shape, const),
            pl.BlockSpec(wa.shape, const),
            pl.BlockSpec(wb.shape, const),
            pl.BlockSpec(wo.shape, const),
            pl.BlockSpec((1, D_MODEL), const),
            pl.BlockSpec(wpqT.shape, const),
            pl.BlockSpec(c1.shape, const),
            pl.BlockSpec(c2.shape, const),
        ],
        out_specs=(
            pl.BlockSpec((tm, D_MODEL), row),
            pl.BlockSpec((D_MODEL, tm), lambda i: (0, i)),
            pl.BlockSpec((PEER_HEADS, N_KEYS, tm), lambda i: (0, 0, i)),
            pl.BlockSpec((PEER_HEADS, N_KEYS, tm), lambda i: (0, 0, i)),
        ),
        compiler_params=_cparams(("parallel",)), name="merge",
    )(x, oa, ob, gattn, wg, wa, wb, wo, gffn, wpqT, c1, c2)


def _pack_rows(x):
    return pltpu.bitcast(x, jnp.uint32)


def _unpack_rows(x):
    return pltpu.bitcast(x, BF16)


def _top16(x):
    idx = lax.broadcasted_iota(I32, x.shape, 0)
    rank = jnp.full(x.shape, PEER_TOPK, I32)
    rows = []
    for r in range(PEER_TOPK):
        m = jnp.max(x, axis=0, keepdims=True)
        rows.append(m)
        first = jnp.min(jnp.where(x == m, idx, N_KEYS), axis=0, keepdims=True)
        hit = idx == first
        rank = jnp.where(hit, r, rank)
        x = jnp.where(hit, -jnp.inf, x)
    return jnp.concatenate(rows, axis=0), rank


def _select_kernel(s1_ref, s2_ref, cnt_ref, e1_ref, r2_ref, e2_ref):
    sub8 = lax.broadcasted_iota(I32, (8, 1), 0).astype(F32)
    sub16 = lax.broadcasted_iota(I32, (PEER_TOPK, 1), 0).astype(F32)

    def count(mask):
        return jnp.sum(jnp.where(mask, 1.0, 0.0), axis=0, keepdims=True)

    def head_body(hh, carry):
        s1 = s1_ref[hh]
        s2 = s2_ref[hh]
        v1, r1 = _top16(s1)
        v2, r2 = _top16(s2)
        pieces = [v1[0:1] + v2]
        for a in range(1, 8):
            piece = v1[a:a + 1] + v2[0:8]
            pieces.append(jnp.where(sub8 < PEER_TOPK // (a + 1), piece, -jnp.inf))
        pieces.append(v1[8:16] + v2[0:1])
        cand = jnp.concatenate(pieces, axis=0)
        rem = jnp.full((1, cand.shape[1]), float(PEER_TOPK), F32)
        tau = jnp.zeros((1, cand.shape[1]), F32)
        x = cand
        for _ in range(PEER_TOPK):
            m = jnp.max(x, axis=0, keepdims=True)
            hit = x == m
            tau = jnp.where(rem > 0, m, tau)
            rem = rem - count(hit)
            x = jnp.where(hit, -jnp.inf, x)
        need = PEER_TOPK - count(cand > tau)
        kept = []
        for a in range(PEER_TOPK):
            row = pieces[a] if a < 8 else pieces[8][a - 8:a - 7]
            ties = count(row == tau)
            kept.append(count(row > tau) + jnp.minimum(ties, jnp.maximum(need, 0.0)))
            need = need - ties
        smax = v1[0:1] + v2[0:1]
        z = jnp.sum(jnp.where(sub16 < kept[0], jnp.exp(pieces[0] - smax), 0.0), axis=0, keepdims=True)
        for a in range(1, 8):
            z = z + jnp.sum(jnp.where(sub8 < kept[a], jnp.exp(pieces[a] - smax), 0.0),
                            axis=0, keepdims=True)
        tail = jnp.concatenate(kept[8:], axis=0)
        z = z + jnp.sum(jnp.where(tail > 0, jnp.exp(pieces[8] - smax), 0.0), axis=0, keepdims=True)
        cnt = jnp.zeros(s1.shape, F32)
        for a in range(PEER_TOPK):
            cnt = jnp.where(r1 == a, kept[a], cnt)
        cnt_ref[hh] = cnt
        e1_ref[hh] = jnp.exp(s1 - v1[0:1]) * (1.0 / z)
        r2b = r2.astype(F32).astype(BF16)
        e2b = jnp.exp(s2 - v2[0:1]).astype(BF16)
        for c in range(s2.shape[1] // LANES):
            cs = slice(c * LANES, (c + 1) * LANES)
            r2_ref[hh, c] = _pack_rows(r2b[:, cs])
            e2_ref[hh, c] = _pack_rows(e2b[:, cs])
        return carry

    lax.fori_loop(0, PEER_HEADS, head_body, 0)


def _select(s1, s2):
    n = s1.shape[-1]
    tn = SEL_TILE
    blk = pl.BlockSpec((PEER_HEADS, N_KEYS, tn), lambda i: (0, 0, i))
    packed = jax.ShapeDtypeStruct((PEER_HEADS, n // LANES, N_KEYS // 2, LANES), jnp.uint32)
    pblk = pl.BlockSpec((PEER_HEADS, tn // LANES, N_KEYS // 2, LANES), lambda i: (0, i, 0, 0))
    return pl.pallas_call(
        _select_kernel,
        out_shape=(
            jax.ShapeDtypeStruct(s1.shape, F32),
            jax.ShapeDtypeStruct(s1.shape, F32),
            packed, packed,
        ),
        grid=(n // tn,),
        in_specs=[blk, blk],
        out_specs=(blk, blk, pblk, pblk),
        compiler_params=_cparams(("parallel",)), name="select",
    )(s1, s2)


def _peer_kernel(h2T_ref, u_ref, vT_ref, cnt_ref, e1_ref, r2_ref, e2_ref, x2_ref,
                 y_ref, acc_ref, wa_ref, rowc_ref, rowe_ref):
    e = pl.program_id(1)

    @pl.when(e == 0)
    def _():
        acc_ref[...] = jnp.zeros_like(acc_ref)

    tn = h2T_ref.shape[1]
    nsub = PEER_EB // N_KEYS
    reps = N_KEYS // BF16_ROWS
    for sub in range(nsub):
        for hh in range(PEER_HEADS):
            rowc_ref[sub, hh] = _pack_rows(
                jnp.broadcast_to(cnt_ref[hh, sub:sub + 1, :], (BF16_ROWS, tn)).astype(BF16))
            rowe_ref[sub, hh] = _pack_rows(
                jnp.broadcast_to(e1_ref[hh, sub:sub + 1, :], (BF16_ROWS, tn)).astype(BF16))

    for sub in range(nsub):
        z = jnp.dot(u_ref[sub * N_KEYS:(sub + 1) * N_KEYS, :], h2T_ref[...],
                    preferred_element_type=F32)
        for c in range(tn // LANES):
            cs = slice(c * LANES, (c + 1) * LANES)
            zc = z[:, cs]
            a = (0.5 * zc * (1.0 + lax.erf(zc * (2.0 ** -0.5)))).astype(BF16)
            w = jnp.zeros((N_KEYS, LANES), BF16)
            for hh in range(PEER_HEADS):
                cnt = jnp.concatenate([_unpack_rows(rowc_ref[sub, hh, :, cs])] * reps, axis=0)
                e1 = jnp.concatenate([_unpack_rows(rowe_ref[sub, hh, :, cs])] * reps, axis=0)
                w = w + jnp.where(_unpack_rows(r2_ref[hh, c]) < cnt,
                                  _unpack_rows(e2_ref[hh, c]) * e1, jnp.zeros((), BF16))
            wa_ref[sub * (N_KEYS // 2):(sub + 1) * (N_KEYS // 2), cs] = _pack_rows(w * a)
    acc_ref[...] += jnp.dot(vT_ref[...], _unpack_rows(wa_ref[...]), preferred_element_type=F32)

    @pl.when(e == pl.num_programs(1) - 1)
    def _():
        y_ref[...] = x2_ref[...] + acc_ref[...].T


def _peer(h2T, u, vT, cnt, e1, r2, e2, x2):
    n = x2.shape[0]
    tn = PEER_TOK
    nsub = PEER_EB // N_KEYS
    tok3 = pl.BlockSpec((PEER_HEADS, tn // LANES, N_KEYS // 2, LANES), lambda i, e: (0, i, 0, 0))
    key1 = pl.BlockSpec((PEER_HEADS, nsub, tn), lambda i, e: (0, e, i))
    return pl.pallas_call(
        _peer_kernel,
        out_shape=jax.ShapeDtypeStruct((n, D_MODEL), F32),
        grid=(n // tn, N_EXPERTS // PEER_EB),
        in_specs=[
            pl.BlockSpec((D_MODEL, tn), lambda i, e: (0, i)),
            pl.BlockSpec((PEER_EB, D_MODEL), lambda i, e: (e, 0)),
            pl.BlockSpec((D_MODEL, PEER_EB), lambda i, e: (0, e)),
            key1, key1, tok3, tok3,
            pl.BlockSpec((tn, D_MODEL), lambda i, e: (i, 0)),
        ],
        out_specs=pl.BlockSpec((tn, D_MODEL), lambda i, e: (i, 0)),
        scratch_shapes=[
            pltpu.VMEM((D_MODEL, tn), F32),
            pltpu.VMEM((PEER_EB // 2, tn), jnp.uint32),
            pltpu.VMEM((nsub, PEER_HEADS, BF16_ROWS // 2, tn), jnp.uint32),
            pltpu.VMEM((nsub, PEER_HEADS, BF16_ROWS // 2, tn), jnp.uint32),
        ],
        compiler_params=_cparams(("parallel", "arbitrary")), name="peer",
    )(h2T, u, vT, cnt, e1, r2, e2, x2)


def _rope_tables(pos):
    posf = pos.astype(F32)[:, None]
    t = pos.shape[0]

    def tab(half, reps):
        inv = jnp.power(ROPE_THETA, -jnp.arange(half, dtype=F32) / half)
        ang = posf * inv[None, :]
        cos = jnp.cos(ang)
        sin = jnp.sin(ang)
        return (jnp.tile(jnp.concatenate([cos, cos], axis=1), (1, reps)),
                jnp.tile(jnp.concatenate([-sin, sin], axis=1), (1, reps)))

    cosq, sinq = tab(HEAD_DIM // 2, N_HEADS)
    cosi, sini = tab(IDX_DIM // 2, IDX_HEADS + 1)
    pad = IDX_W - cosi.shape[1]
    cosi = jnp.concatenate([cosi, jnp.ones((t, pad), F32)], axis=1)
    sini = jnp.concatenate([sini, jnp.zeros((t, pad), F32)], axis=1)
    return cosq, sinq, cosi, sini


def _band_bias(rel_bias):
    qi = np.arange(LANES)[:, None]
    kj = np.arange(BAND_KEYS)[None, :]
    ok = (kj // CHUNK >= qi // CHUNK) & (kj // CHUNK <= qi // CHUNK + BAND_PAST // CHUNK)
    m = np.arange(BAND_KEYS + LANES - 1)
    diag = rel_bias.astype(F32)[:, np.clip(BAND_PAST + LANES - 1 - m, -REL_CLIP, REL_CLIP) + REL_CLIP]
    rows = [diag[:, LANES - 1 - q:LANES - 1 - q + BAND_KEYS] for q in range(LANES)]
    return jnp.where(jnp.asarray(ok)[None], jnp.stack(rows, axis=1), NEG)


def _layer_weights(g_attn, w_in, g_qa, g_ka, g_ik, g_qb, g_kb, rel_bias, w_br_a, w_br_b,
                   w_out, g_ffn, w_pq, c1, c2, u, v):
    a0 = 3 * ATT_W
    i0 = a0 + IDX_HEADS * IDX_DIM + IDX_DIM + IDX_HEADS
    b0 = i0 + 3 * ATT_W
    idx_cols = i0 - a0
    w1 = jnp.concatenate(
        [w_in[:, :a0], w_in[:, i0:b0], w_in[:, a0:i0],
         jnp.zeros((D_MODEL, IDX_W - idx_cols), w_in.dtype)], axis=1).astype(BF16)
    wg = w_in[:, b0:].astype(BF16)
    hd = jnp.asarray(np.kron(np.eye(N_HEADS), np.ones((HEAD_DIM, HEAD_DIM))) / HEAD_DIM, BF16)
    gh = jnp.stack([jnp.tile(g, N_HEADS) for g in (g_qa, g_ka, g_qb, g_kb)]).astype(F32)
    gidx = jnp.concatenate(
        [g_ik.astype(F32), jnp.full((IDX_HEADS,), IDX_HEADS ** -0.5, F32),
         jnp.ones((LANES - IDX_DIM - IDX_HEADS,), F32)])[None, :]
    tri = jnp.asarray(np.tril(np.ones((KEY_TILE, KEY_TILE)), -1), BF16)
    return dict(
        w1=w1, wg=wg, hd=hd, gh=gh, gidx=gidx, tri=tri,
        gattn=g_attn.astype(F32)[None, :], gffn=g_ffn.astype(F32)[None, :],
        bias=_band_bias(rel_bias),
        wa=w_br_a.astype(BF16), wb=w_br_b.astype(BF16), wo=w_out.astype(BF16),
        wpqT=w_pq.T.astype(BF16), c1=c1.astype(BF16), c2=c2.astype(BF16),
        u=u.astype(BF16), vT=v.T.astype(BF16))


def _ffn(x, oa, ob, w):
    x2, h2T, s1, s2 = _merge(x, oa, ob, w["gattn"], w["wg"], w["wa"], w["wb"], w["wo"],
                             w["gffn"], w["wpqT"], w["c1"], w["c2"])
    cnt, e1, r2, e2 = _select(s1, s2)
    return _peer(h2T, w["u"], w["vT"], cnt, e1, r2, e2, x2)


def _prompt_layer(x, w):
    b, t, _ = x.shape
    xf = x.reshape(b * t, D_MODEL)
    tabs = _rope_tables(jnp.arange(t))
    (aqT, ak, akb, av, avT, iqT, ik, ikb, iwT, bq, bk, bkb, bv, bvb) = _proj(
        xf, b, t, w["w1"], w["gattn"], w["hd"], w["gh"], w["gidx"], *tabs)
    oa = _dsa(aqT, iqT, iwT, akb.reshape(b, t, ATT_W), avT, ikb.reshape(b, t, LANES),
              w["tri"], q0=0, n_valid=t)
    front = ((0, 0), (BAND_PAST, 0), (0, 0))
    ob = _band(bq.reshape(b, t, ATT_W), jnp.pad(bkb.reshape(b, t, ATT_W), front),
               jnp.pad(bvb.reshape(b, t, ATT_W), front), w["bias"],
               row_lo=BAND_PAST, row_hi=BAND_PAST + t)
    y = _ffn(xf, oa.reshape(b * t, ATT_W), ob.reshape(b * t, ATT_W), w)
    keep = min(BAND_PAST, t)
    heads = lambda a: a.reshape(b, t, N_HEADS, HEAD_DIM)
    return y.reshape(b, t, D_MODEL), (
        heads(ak), heads(av), ik.reshape(b, t, IDX_DIM),
        heads(bk)[:, t - keep:], heads(bv)[:, t - keep:])


def _sample_layer(x, ca_k, ca_v, ca_ik, cb_k, cb_v, w):
    b, t, _ = x.shape
    n = b * t
    past = ca_k.shape[1]
    xf = x.reshape(n, D_MODEL)
    tabs = _rope_tables(jnp.tile(past + jnp.arange(t), b))
    (aqT, ak, akb, av, avT, iqT, ik, ikb, iwT, bq, bk, bkb, bv, bvb) = _proj(
        xf, 1, n, w["w1"], w["gattn"], w["hd"], w["gh"], w["gidx"], *tabs)

    def per_seq_T(aT, rows):
        a = aT[0].T.reshape(b, t, rows)
        return jnp.pad(a, ((0, 0), (0, LANES - t), (0, 0))).transpose(0, 2, 1)

    lk = past + t
    lp = -(-lk // KEY_TILE) * KEY_TILE
    padk = ((0, 0), (0, lp - lk), (0, 0))
    k_all = jnp.pad(jnp.concatenate(
        [ca_k.reshape(b, past, ATT_W).astype(BF16), akb.reshape(b, t, ATT_W)], axis=1), padk)
    v_all = jnp.pad(jnp.concatenate(
        [ca_v.reshape(b, past, ATT_W).astype(BF16), av.astype(BF16).reshape(b, t, ATT_W)],
        axis=1), padk)
    vT_all = v_all.reshape(b, lp // KEY_TILE, KEY_TILE, ATT_W).transpose(0, 1, 3, 2)
    ik_all = jnp.pad(jnp.concatenate(
        [jnp.pad(ca_ik.astype(BF16), ((0, 0), (0, 0), (0, LANES - IDX_DIM))),
         ikb.reshape(b, t, LANES)], axis=1), padk)
    oa = _dsa(per_seq_T(aqT, ATT_W), per_seq_T(iqT, 2 * LANES), per_seq_T(iwT, IDX_HEADS),
              k_all, vT_all, ik_all, w["tri"], q0=past, n_valid=lk)[:, :t]

    pb = cb_k.shape[1]
    padb = ((0, 0), (BAND_PAST - pb, LANES - t), (0, 0))
    qpad = jnp.pad(bq.reshape(b, t, ATT_W), ((0, 0), (0, LANES - t), (0, 0)))
    kb_all = jnp.pad(jnp.concatenate(
        [cb_k.reshape(b, pb, ATT_W).astype(BF16), bkb.reshape(b, t, ATT_W)], axis=1), padb)
    vb_all = jnp.pad(jnp.concatenate(
        [cb_v.reshape(b, pb, ATT_W).astype(BF16), bvb.reshape(b, t, ATT_W)], axis=1), padb)
    ob = _band(qpad, kb_all, vb_all, w["bias"],
               row_lo=BAND_PAST - pb, row_hi=BAND_PAST + t)[:, :t]

    y = _ffn(xf, oa.reshape(n, ATT_W), ob.reshape(n, ATT_W), w)
    keep = min(BAND_PAST, t)
    heads = lambda a: a.reshape(b, t, N_HEADS, HEAD_DIM)
    return y.reshape(b, t, D_MODEL), (
        heads(ak), heads(av), ik.reshape(b, t, IDX_DIM),
        heads(bk)[:, t - keep:], heads(bv)[:, t - keep:])


def kernel(x_prompt, x_sample, cache_a_k, cache_a_v, cache_a_ik, cache_b_k, cache_b_v,
           g_attn, w_in, g_qa, g_ka, g_ik, g_qb, g_kb, rel_bias, w_br_a, w_br_b, w_out,
           g_ffn, w_pq, peer_c1, peer_c2, peer_u, peer_v):
    depth = w_in.shape[0]
    xp, xs = x_prompt, x_sample
    sp, ss = [], []
    for l in range(depth):
        w = _layer_weights(g_attn[l], w_in[l], g_qa[l], g_ka[l], g_ik[l], g_qb[l], g_kb[l],
                           rel_bias[l], w_br_a[l], w_br_b[l], w_out[l], g_ffn[l], w_pq[l],
                           peer_c1[l], peer_c2[l], peer_u[l], peer_v[l])
        xp, st_p = _prompt_layer(xp, w)
        xs, st_s = _sample_layer(xs, cache_a_k[l], cache_a_v[l], cache_a_ik[l],
                                 cache_b_k[l], cache_b_v[l], w)
        sp.append(st_p)
        ss.append(st_s)
    stack = lambda sts, i: jnp.stack([s[i] for s in sts])
    return (xp, xs) + tuple(stack(sp, i) for i in range(5)) + tuple(stack(ss, i) for i in range(5))
```

```python
import functools

import jax
import jax.numpy as jnp
import numpy as np
from jax import lax
from jax.experimental import pallas as pl
from jax.experimental.pallas import tpu as pltpu

F32 = jnp.float32
BF16 = jnp.bfloat16
I32 = jnp.int32

D_MODEL = 1024
HEAD_DIM = 64
N_HEADS = 8
ATT_W = N_HEADS * HEAD_DIM
IDX_HEADS = 8
IDX_DIM = 32
IDX_W = 384
CHUNK = 64
TOPK_MAX = 256
BAND_PAST = 512
BAND_KEYS = BAND_PAST + 128
REL_CLIP = 128
PEER_HEADS = 8
PEER_DQ = 256
N_KEYS = 128
N_EXPERTS = N_KEYS * N_KEYS
PEER_TOPK = 16
ROPE_THETA = 10000.0
EPS = 1e-6
NEG = -1e30
INT_MIN = -(2 ** 31)

LANES = 128
BF16_ROWS = 16
KEY_TILE = 256
ROW_TILE = 512
SEL_TILE = 256
PEER_TOK = 512
PEER_EB = 1024
PEER_GRP = 256
VMEM_LIMIT = 56 * 1024 * 1024


def _cparams(sem):
    return pltpu.CompilerParams(dimension_semantics=sem, vmem_limit_bytes=VMEM_LIMIT)


def _rope_chunks(y, cos, sin, half):
    lane = lax.broadcasted_iota(I32, (1, LANES), 1)
    lo = (lane % (2 * half)) < half
    out = []
    for c in range(y.shape[1] // LANES):
        sl = slice(c * LANES, (c + 1) * LANES)
        yc = y[:, sl]
        partner = jnp.where(lo, pltpu.roll(yc, LANES - half, 1), pltpu.roll(yc, half, 1))
        out.append(yc * cos[:, sl] + partner * sin[:, sl])
    return jnp.concatenate(out, axis=1)


def _proj_kernel(x_ref, gattn_ref, w1_ref, hd_ref, gh_ref, gidx_ref,
                 cosq_ref, sinq_ref, cosi_ref, sini_ref,
                 aqT_ref, ak_ref, akb_ref, av_ref, avT_ref, iqT_ref, ik_ref, ikb_ref,
                 iwT_ref, bq_ref, bk_ref, bkb_ref, bv_ref, bvb_ref):
    x = x_ref[...]
    ms = jnp.mean(x * x, axis=-1, keepdims=True)
    h = (x * lax.rsqrt(ms + EPS) * gattn_ref[...]).astype(BF16)

    def seg(i):
        return jnp.dot(h, w1_ref[:, i * ATT_W:(i + 1) * ATT_W], preferred_element_type=F32)

    def headnorm(y, gi):
        hms = jnp.dot((y * y).astype(BF16), hd_ref[...], preferred_element_type=F32)
        return y * lax.rsqrt(hms + EPS) * gh_ref[gi:gi + 1, :]

    cosq = cosq_ref[...]
    sinq = sinq_ref[...]

    aq = _rope_chunks(headnorm(seg(0), 0), cosq, sinq, HEAD_DIM // 2) * (HEAD_DIM ** -0.5)
    aqT_ref[...] = aq.T.astype(BF16)

    ak = _rope_chunks(headnorm(seg(1), 1), cosq, sinq, HEAD_DIM // 2)
    ak_ref[...] = ak
    akb_ref[...] = ak.astype(BF16)

    av = seg(2)
    av_ref[...] = av
    for c in range(av.shape[0] // KEY_TILE):
        avT_ref[c] = av[c * KEY_TILE:(c + 1) * KEY_TILE, :].T.astype(BF16)

    bq_ref[...] = (headnorm(seg(3), 2) * (HEAD_DIM ** -0.5)).astype(BF16)
    bk = headnorm(seg(4), 3)
    bk_ref[...] = bk
    bkb_ref[...] = bk.astype(BF16)
    bv = seg(5)
    bv_ref[...] = bv
    bvb_ref[...] = bv.astype(BF16)

    s = jnp.dot(h, w1_ref[:, 6 * ATT_W:6 * ATT_W + IDX_W], preferred_element_type=F32)
    lane = lax.broadcasted_iota(I32, (1, LANES), 1)
    is_ik = lane < IDX_DIM
    c2 = s[:, 2 * LANES:3 * LANES]
    ikms = jnp.sum(jnp.where(is_ik, c2 * c2, 0.0), axis=-1, keepdims=True) * (1.0 / IDX_DIM)
    c2n = c2 * jnp.where(is_ik, lax.rsqrt(ikms + EPS), 1.0) * gidx_ref[...]
    sn = jnp.concatenate([s[:, :2 * LANES], c2n], axis=1)
    r = _rope_chunks(sn, cosi_ref[...], sini_ref[...], IDX_DIM // 2)
    iqT_ref[...] = r[:, :2 * LANES].T.astype(BF16)
    c2r = r[:, 2 * LANES:]
    ik_ref[...] = c2r[:, :IDX_DIM]
    ikb_ref[...] = jnp.where(is_ik, c2r, 0.0).astype(BF16)
    iwT_ref[...] = c2r.T[IDX_DIM:IDX_DIM + IDX_HEADS, :]


def _proj(x, nb, tb, w1, gattn, hd, gh, gidx, cosq, sinq, cosi, sini):
    n = x.shape[0]
    tm = ROW_TILE
    tpb = tb // tm
    row = lambda i: (i, 0)
    tab = lambda i: (i % tpb, 0)
    const = lambda i: (0, 0)
    grp = lambda i: (i // tpb, 0, i % tpb)
    out_shape = (
        jax.ShapeDtypeStruct((nb, ATT_W, tb), BF16),
        jax.ShapeDtypeStruct((n, ATT_W), F32),
        jax.ShapeDtypeStruct((n, ATT_W), BF16),
        jax.ShapeDtypeStruct((n, ATT_W), F32),
        jax.ShapeDtypeStruct((nb, tb // KEY_TILE, ATT_W, KEY_TILE), BF16),
        jax.ShapeDtypeStruct((nb, 2 * LANES, tb), BF16),
        jax.ShapeDtypeStruct((n, IDX_DIM), F32),
        jax.ShapeDtypeStruct((n, LANES), BF16),
        jax.ShapeDtypeStruct((nb, IDX_HEADS, tb), F32),
        jax.ShapeDtypeStruct((n, ATT_W), BF16),
        jax.ShapeDtypeStruct((n, ATT_W), F32),
        jax.ShapeDtypeStruct((n, ATT_W), BF16),
        jax.ShapeDtypeStruct((n, ATT_W), F32),
        jax.ShapeDtypeStruct((n, ATT_W), BF16),
    )
    out_specs = (
        pl.BlockSpec((None, ATT_W, tm), grp),
        pl.BlockSpec((tm, ATT_W), row),
        pl.BlockSpec((tm, ATT_W), row),
        pl.BlockSpec((tm, ATT_W), row),
        pl.BlockSpec((None, tm // KEY_TILE, ATT_W, KEY_TILE),
                     lambda i: (i // tpb, i % tpb, 0, 0)),
        pl.BlockSpec((None, 2 * LANES, tm), grp),
        pl.BlockSpec((tm, IDX_DIM), row),
        pl.BlockSpec((tm, LANES), row),
        pl.BlockSpec((None, IDX_HEADS, tm), grp),
        pl.BlockSpec((tm, ATT_W), row),
        pl.BlockSpec((tm, ATT_W), row),
        pl.BlockSpec((tm, ATT_W), row),
        pl.BlockSpec((tm, ATT_W), row),
        pl.BlockSpec((tm, ATT_W), row),
    )
    in_specs = [
        pl.BlockSpec((tm, D_MODEL), row),
        pl.BlockSpec((1, D_MODEL), const),
        pl.BlockSpec(w1.shape, const),
        pl.BlockSpec(hd.shape, const),
        pl.BlockSpec(gh.shape, const),
        pl.BlockSpec(gidx.shape, const),
        pl.BlockSpec((tm, ATT_W), tab),
        pl.BlockSpec((tm, ATT_W), tab),
        pl.BlockSpec((tm, IDX_W), tab),
        pl.BlockSpec((tm, IDX_W), tab),
    ]
    return pl.pallas_call(
        _proj_kernel, out_shape=out_shape, grid=(n // tm,),
        in_specs=in_specs, out_specs=out_specs,
        compiler_params=_cparams(("parallel",)), name="proj",
    )(x, gattn, w1, hd, gh, gidx, cosq, sinq, cosi, sini)


def _dsa_kernel(qT_ref, iqT_ref, iwT_ref, k_ref, vT_ref, ikb_ref, tri_ref,
                o_ref, sc_ref, am_ref, s_ref, oT_ref, acc_ref, *, q0, n_valid, n_key_tiles, n_sel):
    j = pl.program_id(1)
    qbase = q0 + j * LANES
    n_t = jnp.minimum(lax.shift_right_logical(qbase + LANES + KEY_TILE - 1, 8), n_key_tiles)
    qchunk = lax.shift_right_logical(
        qbase + lax.broadcasted_iota(I32, (1, LANES), 1), 6)

    def rows(t):
        return pl.ds(pl.multiple_of(t * KEY_TILE, KEY_TILE), KEY_TILE)

    def admissible(t):
        kidx = t * KEY_TILE + lax.broadcasted_iota(I32, (KEY_TILE, LANES), 0)
        return jnp.where(kidx < n_valid, lax.shift_right_logical(kidx, 6), 1 << 30) <= qchunk

    def fold8(v):
        return v.reshape(KEY_TILE // 8, 8, v.shape[-1])

    zpad = jnp.zeros((LANES - IDX_DIM, LANES), BF16)
    rhs_idx = jnp.concatenate(
        [jnp.concatenate([iqT_ref[h * IDX_DIM:(h + 1) * IDX_DIM, :], zpad], axis=0)
         for h in range(IDX_HEADS)], axis=1)

    def score_body(t, carry):
        res = jnp.dot(ikb_ref[rows(t), :], rhs_idx, preferred_element_type=F32)
        acc = jnp.zeros((KEY_TILE, LANES), F32)
        for h in range(IDX_HEADS):
            acc = acc + jnp.maximum(res[:, h * LANES:(h + 1) * LANES], 0.0) * iwT_ref[h:h + 1, :]
        sc_ref[rows(t), :] = jnp.where(admissible(t), acc, NEG)
        return carry

    lax.fori_loop(0, n_t, score_body, 0)

    def count(pred):
        def body(t, c8):
            return c8 + jnp.sum(fold8(jnp.where(pred(sc_ref[rows(t), :]), 1, 0)), axis=0)
        c8 = lax.fori_loop(0, n_t, body, jnp.zeros((8, LANES), I32))
        return jnp.sum(c8, axis=0, keepdims=True)

    def bisect_body(it, p):
        c = p + lax.shift_left(jnp.int32(1), 31 - it)
        cf = pltpu.bitcast(jnp.where(c >= 0, c, c ^ 0x7FFFFFFF), F32)
        return jnp.where(count(lambda v: v >= cf) >= n_sel, c, p)

    p = lax.fori_loop(0, 32, bisect_body, jnp.full((1, LANES), INT_MIN, I32))
    thr = pltpu.bitcast(jnp.where(p >= 0, p, p ^ 0x7FFFFFFF), F32)
    need = (n_sel - count(lambda v: v > thr)).astype(F32)

    def mask_body(t, carry):
        v = sc_ref[rows(t), :]
        eq = v == thr
        eqf = jnp.where(eq, 1.0, 0.0)
        pre = jnp.dot(tri_ref[...], eqf.astype(BF16), preferred_element_type=F32) + carry
        tie = jnp.where(eq, jnp.where(pre < need, 0.0, NEG), NEG)
        keep = jnp.where(v > thr, 0.0, tie)
        am_ref[rows(t), :] = jnp.where(admissible(t), keep, NEG)
        return carry + jnp.sum(jnp.sum(fold8(eqf), axis=0), axis=0, keepdims=True)

    lax.fori_loop(0, n_t, mask_body, jnp.zeros((1, LANES), F32))

    npair = N_HEADS // 2
    pw = 2 * LANES
    zq = jnp.zeros((HEAD_DIM, LANES), BF16)
    rhs = [jnp.concatenate(
        [jnp.concatenate([qT_ref[pr * LANES:pr * LANES + HEAD_DIM, :], zq], axis=1),
         jnp.concatenate([zq, qT_ref[pr * LANES + HEAD_DIM:(pr + 1) * LANES, :]], axis=1)], axis=0)
        for pr in range(npair)]

    def logit_body(t, m8):
        am = am_ref[rows(t), :]
        am2 = jnp.concatenate([am, am], axis=1)
        tops = []
        for pr in range(npair):
            s = jnp.dot(k_ref[rows(t), pr * LANES:(pr + 1) * LANES], rhs[pr],
                        preferred_element_type=F32) + am2
            s_ref[rows(t), pr * pw:(pr + 1) * pw] = s
            tops.append(jnp.max(fold8(s), axis=0))
        return jnp.maximum(m8, jnp.concatenate(tops, axis=1))

    m8 = lax.fori_loop(0, n_t, logit_body, jnp.full((8, npair * pw), NEG, F32))
    m = jnp.max(m8, axis=0, keepdims=True)
    acc_ref[...] = jnp.zeros_like(acc_ref)

    def pv_body(t, l8):
        sums = []
        for pr in range(npair):
            cols = slice(pr * pw, (pr + 1) * pw)
            pexp = jnp.exp(s_ref[rows(t), cols] - m[:, cols])
            sums.append(jnp.sum(fold8(pexp), axis=0))
            acc_ref[pr] += jnp.dot(vT_ref[t, pr * LANES:(pr + 1) * LANES, :], pexp.astype(BF16),
                                   preferred_element_type=F32)
        return l8 + jnp.concatenate(sums, axis=1)

    l8 = lax.fori_loop(0, n_t, pv_body, jnp.zeros((8, npair * pw), F32))
    inv = 1.0 / jnp.sum(l8, axis=0, keepdims=True)
    for pr in range(npair):
        lo = pr * LANES
        acc = acc_ref[pr]
        oT_ref[lo:lo + HEAD_DIM, :] = acc[:HEAD_DIM, :LANES] * inv[:, pr * pw:pr * pw + LANES]
        oT_ref[lo + HEAD_DIM:lo + LANES, :] = acc[HEAD_DIM:, LANES:] * inv[:, pr * pw + LANES:(pr + 1) * pw]

    o_ref[...] = oT_ref[...].T.astype(BF16)


def _dsa(qT, iqT, iwT, k, vT, ikb, tri, *, q0, n_valid):
    nb, _, tq = qT.shape
    lp = k.shape[1]
    qblk = lambda b, j: (b, 0, j)
    seq3 = lambda b, j: (b, 0, 0)
    seq4 = lambda b, j: (b, 0, 0, 0)
    kern = functools.partial(_dsa_kernel, q0=q0, n_valid=n_valid, n_key_tiles=lp // KEY_TILE,
                             n_sel=min(TOPK_MAX, n_valid // 4))
    return pl.pallas_call(
        kern, out_shape=jax.ShapeDtypeStruct((nb, tq, ATT_W), BF16),
        grid=(nb, tq // LANES),
        in_specs=[
            pl.BlockSpec((None, ATT_W, LANES), qblk),
            pl.BlockSpec((None, 2 * LANES, LANES), qblk),
            pl.BlockSpec((None, IDX_HEADS, LANES), qblk),
            pl.BlockSpec((None, lp, ATT_W), seq3),
            pl.BlockSpec((None, lp // KEY_TILE, ATT_W, KEY_TILE), seq4),
            pl.BlockSpec((None, lp, LANES), seq3),
            pl.BlockSpec((KEY_TILE, KEY_TILE), lambda b, j: (0, 0)),
        ],
        out_specs=pl.BlockSpec((None, LANES, ATT_W), lambda b, j: (b, j, 0)),
        scratch_shapes=[
            pltpu.VMEM((lp, LANES), F32),
            pltpu.VMEM((lp, LANES), F32),
            pltpu.VMEM((lp, N_HEADS * LANES), F32),
            pltpu.VMEM((ATT_W, LANES), F32),
            pltpu.VMEM((N_HEADS // 2, LANES, 2 * LANES), F32),
        ],
        compiler_params=_cparams(("parallel", "arbitrary")), name="dsa",
    )(qT, iqT, iwT, k, vT, ikb, tri)


def _band_kernel(q_ref, k0, k1, k2, k3, k4, v0, v1, v2, v3, v4, bias_ref, o_ref,
                 *, row_lo, row_hi):
    j = pl.program_id(1)
    kb = jnp.concatenate([r[...] for r in (k0, k1, k2, k3, k4)], axis=0)
    vb = jnp.concatenate([r[...] for r in (v0, v1, v2, v3, v4)], axis=0)
    row = j * LANES + lax.broadcasted_iota(I32, (1, BAND_KEYS), 1)
    valid = jnp.where(row >= row_lo, row, row_hi) < row_hi
    lane = lax.broadcasted_iota(I32, (1, LANES), 1)
    lo_half = lane < HEAD_DIM
    zero = jnp.zeros((), BF16)
    for pr in range(N_HEADS // 2):
        sl = slice(pr * LANES, (pr + 1) * LANES)
        qp = q_ref[:, sl]
        lhs = jnp.concatenate([jnp.where(lo_half, qp, zero), jnp.where(lo_half, zero, qp)], axis=0)
        s = lax.dot_general(lhs, kb[:, sl], (((1,), (1,)), ((), ())),
                            preferred_element_type=F32)
        bias2 = jnp.concatenate([bias_ref[2 * pr], bias_ref[2 * pr + 1]], axis=0)
        s = jnp.where(valid, s + bias2, NEG)
        m = jnp.max(s, axis=-1, keepdims=True)
        pexp = jnp.exp(s - m)
        inv = 1.0 / jnp.sum(pexp, axis=-1, keepdims=True)
        o = jnp.dot(pexp.astype(BF16), vb[:, sl], preferred_element_type=F32) * inv
        o_ref[:, sl] = jnp.where(lo_half, o[:LANES], o[LANES:]).astype(BF16)


def _band(q, kp, vp, bias, *, row_lo, row_hi):
    nb, tq, _ = q.shape
    kspecs = [pl.BlockSpec((None, LANES, ATT_W), (lambda b, j, c=c: (b, j + c, 0)))
              for c in range(BAND_KEYS // LANES)]
    kern = functools.partial(_band_kernel, row_lo=row_lo, row_hi=row_hi)
    return pl.pallas_call(
        kern, out_shape=jax.ShapeDtypeStruct((nb, tq, ATT_W), BF16),
        grid=(nb, tq // LANES),
        in_specs=[pl.BlockSpec((None, LANES, ATT_W), lambda b, j: (b, j, 0))]
        + kspecs + kspecs
        + [pl.BlockSpec(bias.shape, lambda b, j: (0, 0, 0))],
        out_specs=pl.BlockSpec((None, LANES, ATT_W), lambda b, j: (b, j, 0)),
        compiler_params=_cparams(("parallel", "parallel")), name="band",
    )(q, *([kp] * 5), *([vp] * 5), bias)


def _merge_kernel(x_ref, oa_ref, ob_ref, gattn_ref, wg_ref, wa_ref, wb_ref, wo_ref,
                  gffn_ref, wpqT_ref, c1_ref, c2_ref,
                  x2_ref, h2T_ref, s1_ref, s2_ref):
    x = x_ref[...]
    ms = jnp.mean(x * x, axis=-1, keepdims=True)
    h = (x * lax.rsqrt(ms + EPS) * gattn_ref[...]).astype(BF16)
    ga = jax.nn.sigmoid(jnp.dot(h, wg_ref[:, :D_MODEL], preferred_element_type=F32))
    gb = jax.nn.sigmoid(jnp.dot(h, wg_ref[:, D_MODEL:], preferred_element_type=F32))
    ma = jnp.dot(oa_ref[...], wa_ref[...], preferred_element_type=F32)
    mb = jnp.dot(ob_ref[...], wb_ref[...], preferred_element_type=F32)
    merged = (ga * ma + gb * mb).astype(BF16)
    x2 = x + jnp.dot(merged, wo_ref[...], preferred_element_type=F32)
    x2_ref[...] = x2
    ms2 = jnp.mean(x2 * x2, axis=-1, keepdims=True)
    h2T = (x2 * lax.rsqrt(ms2 + EPS) * gffn_ref[...]).T.astype(BF16)
    h2T_ref[...] = h2T
    half = PEER_DQ // 2
    for hh in range(PEER_HEADS):
        qT = jnp.dot(wpqT_ref[hh * PEER_DQ:(hh + 1) * PEER_DQ, :], h2T,
                     preferred_element_type=F32).astype(BF16)
        s1_ref[hh] = jnp.dot(c1_ref[...], qT[:half], preferred_element_type=F32)
        s2_ref[hh] = jnp.dot(c2_ref[...], qT[half:], preferred_element_type=F32)


def _merge(x, oa, ob, gattn, wg, wa, wb, wo, gffn, wpqT, c1, c2):
    n = x.shape[0]
    tm = ROW_TILE
    row = lambda i: (i, 0)
    const = lambda i: (0, 0)
    return pl.pallas_call(
        _merge_kernel,
        out_shape=(
            jax.ShapeDtypeStruct((n, D_MODEL), F32),
            jax.ShapeDtypeStruct((D_MODEL, n), BF16),
            jax.ShapeDtypeStruct((PEER_HEADS, N_KEYS, n), F32),
            jax.ShapeDtypeStruct((PEER_HEADS, N_KEYS, n), F32),
        ),
        grid=(n // tm,),
        in_specs=[
            pl.BlockSpec((tm, D_MODEL), row),
            pl.BlockSpec((tm, ATT_W), row),
            pl.BlockSpec((tm, ATT_W), row),
            pl.BlockSpec((1, D_MODEL), const),
            pl.BlockSpec(wg.shape, const),
            pl.BlockSpec(wa.shape, const),
            pl.BlockSpec(wb.shape, const),
            pl.BlockSpec(wo.shape, const),
            pl.BlockSpec((1, D_MODEL), const),
            pl.BlockSpec(wpqT.shape, const),
            pl.BlockSpec(c1.shape, const),
            pl.BlockSpec(c2.shape, const),
        ],
        out_specs=(
            pl.BlockSpec((tm, D_MODEL), row),
            pl.BlockSpec((D_MODEL, tm), lambda i: (0, i)),
            pl.BlockSpec((PEER_HEADS, N_KEYS, tm), lambda i: (0, 0, i)),
            pl.BlockSpec((PEER_HEADS, N_KEYS, tm), lambda i: (0, 0, i)),
        ),
        compiler_params=_cparams(("parallel",)), name="merge",
    )(x, oa, ob, gattn, wg, wa, wb, wo, gffn, wpqT, c1, c2)


def _pack_rows(x):
    return pltpu.bitcast(x, jnp.uint32)


def _unpack_rows(x):
    return pltpu.bitcast(x, BF16)


def _top16(x, distinct):
    idx = lax.broadcasted_iota(I32, x.shape, 0)
    rank = jnp.full(x.shape, PEER_TOPK, I32)
    rows = []
    for r in range(PEER_TOPK):
        m = jnp.max(x, axis=0, keepdims=True)
        rows.append(m)
        hit = x == m
        if not distinct:
            hit = idx == jnp.min(jnp.where(hit, idx, N_KEYS), axis=0, keepdims=True)
        rank = jnp.where(hit, r, rank)
        x = jnp.where(hit, -jnp.inf, x)
    return jnp.concatenate(rows, axis=0), rank


def _select_kernel(s1_ref, s2_ref, cnt_ref, e1_ref, r2_ref, e2_ref):
    sub8 = lax.broadcasted_iota(I32, (8, 1), 0).astype(F32)
    sub16 = lax.broadcasted_iota(I32, (PEER_TOPK, 1), 0).astype(F32)

    def count(mask):
        return jnp.sum(jnp.where(mask, 1.0, 0.0), axis=0, keepdims=True)

    def head_body(hh, carry):
        s1 = s1_ref[hh]
        s2 = s2_ref[hh]
        v1, r1 = _top16(s1, True)
        v2, r2 = _top16(s2, True)
        ranked = jnp.maximum(count(r1 < PEER_TOPK), count(r2 < PEER_TOPK))
        v1, r1, v2, r2 = lax.cond(
            jnp.max(ranked) > PEER_TOPK,
            lambda: _top16(s1, False) + _top16(s2, False),
            lambda: (v1, r1, v2, r2))
        pieces = [v1[0:1] + v2]
        for a in range(1, 8):
            piece = v1[a:a + 1] + v2[0:8]
            pieces.append(jnp.where(sub8 < PEER_TOPK // (a + 1), piece, -jnp.inf))
        pieces.append(v1[8:16] + v2[0:1])
        cand = jnp.concatenate(pieces, axis=0)
        rem = jnp.full((1, cand.shape[1]), float(PEER_TOPK), F32)
        tau = jnp.zeros((1, cand.shape[1]), F32)
        x = cand
        for _ in range(PEER_TOPK):
            m = jnp.max(x, axis=0, keepdims=True)
            hit = x == m
            tau = jnp.where(rem > 0, m, tau)
            rem = rem - count(hit)
            x = jnp.where(hit, -jnp.inf, x)
        need = PEER_TOPK - count(cand > tau)
        kept = []
        for a in range(PEER_TOPK):
            row = pieces[a] if a < 8 else pieces[8][a - 8:a - 7]
            ties = count(row == tau)
            kept.append(count(row > tau) + jnp.minimum(ties, jnp.maximum(need, 0.0)))
            need = need - ties
        smax = v1[0:1] + v2[0:1]
        z = jnp.sum(jnp.where(sub16 < kept[0], jnp.exp(pieces[0] - smax), 0.0), axis=0, keepdims=True)
        for a in range(1, 8):
            z = z + jnp.sum(jnp.where(sub8 < kept[a], jnp.exp(pieces[a] - smax), 0.0),
                            axis=0, keepdims=True)
        tail = jnp.concatenate(kept[8:], axis=0)
        z = z + jnp.sum(jnp.where(tail > 0, jnp.exp(pieces[8] - smax), 0.0), axis=0, keepdims=True)
        cnt = jnp.zeros(s1.shape, F32)
        for a in range(PEER_TOPK):
            cnt = jnp.where(r1 == a, kept[a], cnt)
        cnt_ref[hh] = cnt
        e1_ref[hh] = jnp.exp(s1 - v1[0:1]) * (1.0 / z)
        r2b = r2.astype(F32).astype(BF16)
        e2b = jnp.exp(s2 - v2[0:1]).astype(BF16)
        for c in range(s2.shape[1] // LANES):
            cs = slice(c * LANES, (c + 1) * LANES)
            r2_ref[hh, c] = _pack_rows(r2b[:, cs])
            e2_ref[hh, c] = _pack_rows(e2b[:, cs])
        return carry

    lax.fori_loop(0, PEER_HEADS, head_body, 0)


def _select(s1, s2):
    n = s1.shape[-1]
    tn = SEL_TILE
    blk = pl.BlockSpec((PEER_HEADS, N_KEYS, tn), lambda i: (0, 0, i))
    packed = jax.ShapeDtypeStruct((PEER_HEADS, n // LANES, N_KEYS // 2, LANES), jnp.uint32)
    pblk = pl.BlockSpec((PEER_HEADS, tn // LANES, N_KEYS // 2, LANES), lambda i: (0, i, 0, 0))
    return pl.pallas_call(
        _select_kernel,
        out_shape=(
            jax.ShapeDtypeStruct(s1.shape, F32),
            jax.ShapeDtypeStruct(s1.shape, F32),
            packed, packed,
        ),
        grid=(n // tn,),
        in_specs=[blk, blk],
        out_specs=(blk, blk, pblk, pblk),
        compiler_params=_cparams(("parallel",)), name="select",
    )(s1, s2)


def _peer_kernel(h2T_ref, u_ref, vT_ref, cnt_ref, e1_ref, r2_ref, e2_ref, x2_ref,
                 y_ref, acc_ref, wa0_ref, wa1_ref, z0_ref, z1_ref, rowc_ref, rowe_ref):
    e = pl.program_id(1)
    zero = jnp.minimum(e, 0)
    z_refs = (z0_ref, z1_ref)
    wa_refs = (wa0_ref, wa1_ref)

    @pl.when(e == 0)
    def _():
        acc_ref[...] = jnp.zeros_like(acc_ref)

    tn = h2T_ref.shape[1]
    nsub = PEER_EB // N_KEYS
    reps = N_KEYS // BF16_ROWS
    for sub in range(nsub):
        for hh in range(PEER_HEADS):
            rowc_ref[sub, hh] = _pack_rows(
                jnp.broadcast_to(cnt_ref[hh, sub:sub + 1, :], (BF16_ROWS, tn)).astype(BF16))
            rowe_ref[sub, hh] = _pack_rows(
                jnp.broadcast_to(e1_ref[hh, sub:sub + 1, :], (BF16_ROWS, tn)).astype(BF16))

    ngrp = PEER_EB // PEER_GRP

    def scores(g, slot):
        start = g * PEER_GRP if isinstance(g, int) else pl.multiple_of(g * PEER_GRP, PEER_GRP)
        z_refs[slot][0] = jnp.dot(u_ref[pl.ds(start, PEER_GRP), :], h2T_ref[...],
                                  preferred_element_type=F32)

    def weights(g, slot):
        for s in range(PEER_GRP // N_KEYS):
            sub = g * (PEER_GRP // N_KEYS) + s
            for c in range(tn // LANES):
                cs = slice(c * LANES, (c + 1) * LANES)
                zc = z_refs[slot][zero, s * N_KEYS:(s + 1) * N_KEYS, cs]
                a = (0.5 * zc * (1.0 + lax.erf(zc * (2.0 ** -0.5)))).astype(BF16)
                w = jnp.zeros((N_KEYS, LANES), BF16)
                for hh in range(PEER_HEADS):
                    cnt = jnp.concatenate([_unpack_rows(rowc_ref[sub + zero, hh, :, cs])] * reps, axis=0)
                    e1 = jnp.concatenate([_unpack_rows(rowe_ref[sub + zero, hh, :, cs])] * reps, axis=0)
                    w = w + jnp.where(_unpack_rows(r2_ref[hh, c]) < cnt,
                                      _unpack_rows(e2_ref[hh, c]) * e1, jnp.zeros((), BF16))
                wa_refs[slot][0, s * (N_KEYS // 2):(s + 1) * (N_KEYS // 2), cs] = _pack_rows(w * a)

    def down(g, slot):
        acc_ref[...] += jnp.dot(vT_ref[g], _unpack_rows(wa_refs[slot][zero]),
                                preferred_element_type=F32)

    scores(0, 0)
    for g in range(ngrp):
        if g + 1 < ngrp:
            scores(g + 1, (g + 1) % 2)
        weights(g, g % 2)
        if g > 0:
            down(g - 1, (g - 1) % 2)
    down(ngrp - 1, (ngrp - 1) % 2)

    @pl.when(e == pl.num_programs(1) - 1)
    def _():
        y_ref[...] = x2_ref[...] + acc_ref[...].T


def _peer(h2T, u, vT, cnt, e1, r2, e2, x2):
    n = x2.shape[0]
    tn = PEER_TOK
    nsub = PEER_EB // N_KEYS
    tok3 = pl.BlockSpec((PEER_HEADS, tn // LANES, N_KEYS // 2, LANES), lambda i, e: (0, i, 0, 0))
    key1 = pl.BlockSpec((PEER_HEADS, nsub, tn), lambda i, e: (0, e, i))
    return pl.pallas_call(
        _peer_kernel,
        out_shape=jax.ShapeDtypeStruct((n, D_MODEL), F32),
        grid=(n // tn, N_EXPERTS // PEER_EB),
        in_specs=[
            pl.BlockSpec((D_MODEL, tn), lambda i, e: (0, i)),
            pl.BlockSpec((PEER_EB, D_MODEL), lambda i, e: (e, 0)),
            pl.BlockSpec((PEER_EB // PEER_GRP, D_MODEL, PEER_GRP), lambda i, e: (e, 0, 0)),
            key1, key1, tok3, tok3,
            pl.BlockSpec((tn, D_MODEL), lambda i, e: (i, 0)),
        ],
        out_specs=pl.BlockSpec((tn, D_MODEL), lambda i, e: (i, 0)),
        scratch_shapes=[
            pltpu.VMEM((D_MODEL, tn), F32),
            pltpu.VMEM((1, PEER_GRP // 2, tn), jnp.uint32),
            pltpu.VMEM((1, PEER_GRP // 2, tn), jnp.uint32),
            pltpu.VMEM((1, PEER_GRP, tn), F32),
            pltpu.VMEM((1, PEER_GRP, tn), F32),
            pltpu.VMEM((nsub, PEER_HEADS, BF16_ROWS // 2, tn), jnp.uint32),
            pltpu.VMEM((nsub, PEER_HEADS, BF16_ROWS // 2, tn), jnp.uint32),
        ],
        compiler_params=_cparams(("parallel", "arbitrary")), name="peer",
    )(h2T, u, vT, cnt, e1, r2, e2, x2)


def _rope_tables(pos):
    posf = pos.astype(F32)[:, None]
    t = pos.shape[0]

    def tab(half, reps):
        inv = jnp.power(ROPE_THETA, -jnp.arange(half, dtype=F32) / half)
        ang = posf * inv[None, :]
        cos = jnp.cos(ang)
        sin = jnp.sin(ang)
        return (jnp.tile(jnp.concatenate([cos, cos], axis=1), (1, reps)),
                jnp.tile(jnp.concatenate([-sin, sin], axis=1), (1, reps)))

    cosq, sinq = tab(HEAD_DIM // 2, N_HEADS)
    cosi, sini = tab(IDX_DIM // 2, IDX_HEADS + 1)
    pad = IDX_W - cosi.shape[1]
    cosi = jnp.concatenate([cosi, jnp.ones((t, pad), F32)], axis=1)
    sini = jnp.concatenate([sini, jnp.zeros((t, pad), F32)], axis=1)
    return cosq, sinq, cosi, sini


def _band_bias(rel_bias):
    qi = np.arange(LANES)[:, None]
    kj = np.arange(BAND_KEYS)[None, :]
    ok = (kj // CHUNK >= qi // CHUNK) & (kj // CHUNK <= qi // CHUNK + BAND_PAST // CHUNK)
    m = np.arange(BAND_KEYS + LANES - 1)
    diag = rel_bias.astype(F32)[:, np.clip(BAND_PAST + LANES - 1 - m, -REL_CLIP, REL_CLIP) + REL_CLIP]
    rows = [diag[:, LANES - 1 - q:LANES - 1 - q + BAND_KEYS] for q in range(LANES)]
    return jnp.where(jnp.asarray(ok)[None], jnp.stack(rows, axis=1), NEG)


def _layer_weights(g_attn, w_in, g_qa, g_ka, g_ik, g_qb, g_kb, rel_bias, w_br_a, w_br_b,
                   w_out, g_ffn, w_pq, c1, c2, u, v):
    a0 = 3 * ATT_W
    i0 = a0 + IDX_HEADS * IDX_DIM + IDX_DIM + IDX_HEADS
    b0 = i0 + 3 * ATT_W
    idx_cols = i0 - a0
    w1 = jnp.concatenate(
        [w_in[:, :a0], w_in[:, i0:b0], w_in[:, a0:i0],
         jnp.zeros((D_MODEL, IDX_W - idx_cols), w_in.dtype)], axis=1).astype(BF16)
    wg = w_in[:, b0:].astype(BF16)
    hd = jnp.asarray(np.kron(np.eye(N_HEADS), np.ones((HEAD_DIM, HEAD_DIM))) / HEAD_DIM, BF16)
    gh = jnp.stack([jnp.tile(g, N_HEADS) for g in (g_qa, g_ka, g_qb, g_kb)]).astype(F32)
    gidx = jnp.concatenate(
        [g_ik.astype(F32), jnp.full((IDX_HEADS,), IDX_HEADS ** -0.5, F32),
         jnp.ones((LANES - IDX_DIM - IDX_HEADS,), F32)])[None, :]
    tri = jnp.asarray(np.tril(np.ones((KEY_TILE, KEY_TILE)), -1), BF16)
    return dict(
        w1=w1, wg=wg, hd=hd, gh=gh, gidx=gidx, tri=tri,
        gattn=g_attn.astype(F32)[None, :], gffn=g_ffn.astype(F32)[None, :],
        bias=_band_bias(rel_bias),
        wa=w_br_a.astype(BF16), wb=w_br_b.astype(BF16), wo=w_out.astype(BF16),
        wpqT=w_pq.T.astype(BF16), c1=c1.astype(BF16), c2=c2.astype(BF16),
        u=u.astype(BF16),
        vT=v.astype(BF16).reshape(N_EXPERTS // PEER_GRP, PEER_GRP, D_MODEL).transpose(0, 2, 1))


def _ffn(x, oa, ob, w):
    x2, h2T, s1, s2 = _merge(x, oa, ob, w["gattn"], w["wg"], w["wa"], w["wb"], w["wo"],
                             w["gffn"], w["wpqT"], w["c1"], w["c2"])
    cnt, e1, r2, e2 = _select(s1, s2)
    return _peer(h2T, w["u"], w["vT"], cnt, e1, r2, e2, x2)


def _prompt_layer(x, w):
    b, t, _ = x.shape
    xf = x.reshape(b * t, D_MODEL)
    tabs = _rope_tables(jnp.arange(t))
    (aqT, ak, akb, av, avT, iqT, ik, ikb, iwT, bq, bk, bkb, bv, bvb) = _proj(
        xf, b, t, w["w1"], w["gattn"], w["hd"], w["gh"], w["gidx"], *tabs)
    oa = _dsa(aqT, iqT, iwT, akb.reshape(b, t, ATT_W), avT, ikb.reshape(b, t, LANES),
              w["tri"], q0=0, n_valid=t)
    front = ((0, 0), (BAND_PAST, 0), (0, 0))
    ob = _band(bq.reshape(b, t, ATT_W), jnp.pad(bkb.reshape(b, t, ATT_W), front),
               jnp.pad(bvb.reshape(b, t, ATT_W), front), w["bias"],
               row_lo=BAND_PAST, row_hi=BAND_PAST + t)
    y = _ffn(xf, oa.reshape(b * t, ATT_W), ob.reshape(b * t, ATT_W), w)
    keep = min(BAND_PAST, t)
    heads = lambda a: a.reshape(b, t, N_HEADS, HEAD_DIM)
    return y.reshape(b, t, D_MODEL), (
        heads(ak), heads(av), ik.reshape(b, t, IDX_DIM),
        heads(bk)[:, t - keep:], heads(bv)[:, t - keep:])


def _sample_layer(x, ca_k, ca_v, ca_ik, cb_k, cb_v, w):
    b, t, _ = x.shape
    n = b * t
    past = ca_k.shape[1]
    xf = x.reshape(n, D_MODEL)
    tabs = _rope_tables(jnp.tile(past + jnp.arange(t), b))
    (aqT, ak, akb, av, avT, iqT, ik, ikb, iwT, bq, bk, bkb, bv, bvb) = _proj(
        xf, 1, n, w["w1"], w["gattn"], w["hd"], w["gh"], w["gidx"], *tabs)

    def per_seq_T(aT, rows):
        a = aT[0].T.reshape(b, t, rows)
        return jnp.pad(a, ((0, 0), (0, LANES - t), (0, 0))).transpose(0, 2, 1)

    lk = past + t
    lp = -(-lk // KEY_TILE) * KEY_TILE
    padk = ((0, 0), (0, lp - lk), (0, 0))
    k_all = jnp.pad(jnp.concatenate(
        [ca_k.reshape(b, past, ATT_W).astype(BF16), akb.reshape(b, t, ATT_W)], axis=1), padk)
    v_all = jnp.pad(jnp.concatenate(
        [ca_v.reshape(b, past, ATT_W).astype(BF16), av.astype(BF16).reshape(b, t, ATT_W)],
        axis=1), padk)
    vT_all = v_all.reshape(b, lp // KEY_TILE, KEY_TILE, ATT_W).transpose(0, 1, 3, 2)
    ik_all = jnp.pad(jnp.concatenate(
        [jnp.pad(ca_ik.astype(BF16), ((0, 0), (0, 0), (0, LANES - IDX_DIM))),
         ikb.reshape(b, t, LANES)], axis=1), padk)
    oa = _dsa(per_seq_T(aqT, ATT_W), per_seq_T(iqT, 2 * LANES), per_seq_T(iwT, IDX_HEADS),
              k_all, vT_all, ik_all, w["tri"], q0=past, n_valid=lk)[:, :t]

    pb = cb_k.shape[1]
    padb = ((0, 0), (BAND_PAST - pb, LANES - t), (0, 0))
    qpad = jnp.pad(bq.reshape(b, t, ATT_W), ((0, 0), (0, LANES - t), (0, 0)))
    kb_all = jnp.pad(jnp.concatenate(
        [cb_k.reshape(b, pb, ATT_W).astype(BF16), bkb.reshape(b, t, ATT_W)], axis=1), padb)
    vb_all = jnp.pad(jnp.concatenate(
        [cb_v.reshape(b, pb, ATT_W).astype(BF16), bvb.reshape(b, t, ATT_W)], axis=1), padb)
    ob = _band(qpad, kb_all, vb_all, w["bias"],
               row_lo=BAND_PAST - pb, row_hi=BAND_PAST + t)[:, :t]

    y = _ffn(xf, oa.reshape(n, ATT_W), ob.reshape(n, ATT_W), w)
    keep = min(BAND_PAST, t)
    heads = lambda a: a.reshape(b, t, N_HEADS, HEAD_DIM)
    return y.reshape(b, t, D_MODEL), (
        heads(ak), heads(av), ik.reshape(b, t, IDX_DIM),
        heads(bk)[:, t - keep:], heads(bv)[:, t - keep:])


def kernel(x_prompt, x_sample, cache_a_k, cache_a_v, cache_a_ik, cache_b_k, cache_b_v,
           g_attn, w_in, g_qa, g_ka, g_ik, g_qb, g_kb, rel_bias, w_br_a, w_br_b, w_out,
           g_ffn, w_pq, peer_c1, peer_c2, peer_u, peer_v):
    depth = w_in.shape[0]
    xp, xs = x_prompt, x_sample
    sp, ss = [], []
    for l in range(depth):
        w = _layer_weights(g_attn[l], w_in[l], g_qa[l], g_ka[l], g_ik[l], g_qb[l], g_kb[l],
                           rel_bias[l], w_br_a[l], w_br_b[l], w_out[l], g_ffn[l], w_pq[l],
                           peer_c1[l], peer_c2[l], peer_u[l], peer_v[l])
        xp, st_p = _prompt_layer(xp, w)
        xs, st_s = _sample_layer(xs, cache_a_k[l], cache_a_v[l], cache_a_ik[l],
                                 cache_b_k[l], cache_b_v[l], w)
        sp.append(st_p)
        ss.append(st_s)
    stack = lambda sts, i: jnp.stack([s[i] for s in sts])
    return (xp, xs) + tuple(stack(sp, i) for i in range(5)) + tuple(stack(ss, i) for i in range(5))
```

```python
import functools

import jax
import jax.numpy as jnp
import numpy as np
from jax import lax
from jax.experimental import pallas as pl
from jax.experimental.pallas import tpu as pltpu

F32 = jnp.float32
BF16 = jnp.bfloat16
I32 = jnp.int32

D_MODEL = 1024
HEAD_DIM = 64
N_HEADS = 8
ATT_W = N_HEADS * HEAD_DIM
IDX_HEADS = 8
IDX_DIM = 32
IDX_W = 384
CHUNK = 64
TOPK_MAX = 256
BAND_PAST = 512
BAND_KEYS = BAND_PAST + 128
REL_CLIP = 128
PEER_HEADS = 8
PEER_DQ = 256
N_KEYS = 128
N_EXPERTS = N_KEYS * N_KEYS
PEER_TOPK = 16
ROPE_THETA = 10000.0
EPS = 1e-6
NEG = -1e30
INT_MIN = -(2 ** 31)

LANES = 128
BF16_ROWS = 16
KEY_TILE = 256
ROW_TILE = 512
SEL_TILE = 256
PEER_TOK = 512
PEER_EB = 1024
PEER_GRP = 256
VMEM_LIMIT = 56 * 1024 * 1024


def _cparams(sem):
    return pltpu.CompilerParams(dimension_semantics=sem, vmem_limit_bytes=VMEM_LIMIT)


def _rope_chunks(y, cos, sin, half):
    lane = lax.broadcasted_iota(I32, (1, LANES), 1)
    lo = (lane % (2 * half)) < half
    out = []
    for c in range(y.shape[1] // LANES):
        sl = slice(c * LANES, (c + 1) * LANES)
        yc = y[:, sl]
        partner = jnp.where(lo, pltpu.roll(yc, LANES - half, 1), pltpu.roll(yc, half, 1))
        out.append(yc * cos[:, sl] + partner * sin[:, sl])
    return jnp.concatenate(out, axis=1)


def _proj_kernel(x_ref, gattn_ref, w1_ref, hd_ref, gh_ref, gidx_ref,
                 cosq_ref, sinq_ref, cosi_ref, sini_ref,
                 aqT_ref, ak_ref, akb_ref, av_ref, avT_ref, iqT_ref, ik_ref, ikb_ref,
                 iwT_ref, bq_ref, bk_ref, bkb_ref, bv_ref, bvb_ref):
    x = x_ref[...]
    ms = jnp.mean(x * x, axis=-1, keepdims=True)
    h = (x * lax.rsqrt(ms + EPS) * gattn_ref[...]).astype(BF16)

    def seg(i):
        return jnp.dot(h, w1_ref[:, i * ATT_W:(i + 1) * ATT_W], preferred_element_type=F32)

    def headnorm(y, gi):
        hms = jnp.dot((y * y).astype(BF16), hd_ref[...], preferred_element_type=F32)
        return y * lax.rsqrt(hms + EPS) * gh_ref[gi:gi + 1, :]

    cosq = cosq_ref[...]
    sinq = sinq_ref[...]

    aq = _rope_chunks(headnorm(seg(0), 0), cosq, sinq, HEAD_DIM // 2) * (HEAD_DIM ** -0.5)
    aqT_ref[...] = aq.T.astype(BF16)

    ak = _rope_chunks(headnorm(seg(1), 1), cosq, sinq, HEAD_DIM // 2)
    ak_ref[...] = ak
    akb_ref[...] = ak.astype(BF16)

    av = seg(2)
    av_ref[...] = av
    for c in range(av.shape[0] // KEY_TILE):
        avT_ref[c] = av[c * KEY_TILE:(c + 1) * KEY_TILE, :].T.astype(BF16)

    bq_ref[...] = (headnorm(seg(3), 2) * (HEAD_DIM ** -0.5)).astype(BF16)
    bk = headnorm(seg(4), 3)
    bk_ref[...] = bk
    bkb_ref[...] = bk.astype(BF16)
    bv = seg(5)
    bv_ref[...] = bv
    bvb_ref[...] = bv.astype(BF16)

    s = jnp.dot(h, w1_ref[:, 6 * ATT_W:6 * ATT_W + IDX_W], preferred_element_type=F32)
    lane = lax.broadcasted_iota(I32, (1, LANES), 1)
    is_ik = lane < IDX_DIM
    c2 = s[:, 2 * LANES:3 * LANES]
    ikms = jnp.sum(jnp.where(is_ik, c2 * c2, 0.0), axis=-1, keepdims=True) * (1.0 / IDX_DIM)
    c2n = c2 * jnp.where(is_ik, lax.rsqrt(ikms + EPS), 1.0) * gidx_ref[...]
    sn = jnp.concatenate([s[:, :2 * LANES], c2n], axis=1)
    r = _rope_chunks(sn, cosi_ref[...], sini_ref[...], IDX_DIM // 2)
    iqT_ref[...] = r[:, :2 * LANES].T.astype(BF16)
    c2r = r[:, 2 * LANES:]
    ik_ref[...] = c2r[:, :IDX_DIM]
    ikb_ref[...] = jnp.where(is_ik, c2r, 0.0).astype(BF16)
    iwT_ref[...] = c2r.T[IDX_DIM:IDX_DIM + IDX_HEADS, :]


def _proj(x, nb, tb, w1, gattn, hd, gh, gidx, cosq, sinq, cosi, sini):
    n = x.shape[0]
    tm = ROW_TILE
    tpb = tb // tm
    row = lambda i: (i, 0)
    tab = lambda i: (i % tpb, 0)
    const = lambda i: (0, 0)
    grp = lambda i: (i // tpb, 0, i % tpb)
    out_shape = (
        jax.ShapeDtypeStruct((nb, ATT_W, tb), BF16),
        jax.ShapeDtypeStruct((n, ATT_W), F32),
        jax.ShapeDtypeStruct((n, ATT_W), BF16),
        jax.ShapeDtypeStruct((n, ATT_W), F32),
        jax.ShapeDtypeStruct((nb, tb // KEY_TILE, ATT_W, KEY_TILE), BF16),
        jax.ShapeDtypeStruct((nb, 2 * LANES, tb), BF16),
        jax.ShapeDtypeStruct((n, IDX_DIM), F32),
        jax.ShapeDtypeStruct((n, LANES), BF16),
        jax.ShapeDtypeStruct((nb, IDX_HEADS, tb), F32),
        jax.ShapeDtypeStruct((n, ATT_W), BF16),
        jax.ShapeDtypeStruct((n, ATT_W), F32),
        jax.ShapeDtypeStruct((n, ATT_W), BF16),
        jax.ShapeDtypeStruct((n, ATT_W), F32),
        jax.ShapeDtypeStruct((n, ATT_W), BF16),
    )
    out_specs = (
        pl.BlockSpec((None, ATT_W, tm), grp),
        pl.BlockSpec((tm, ATT_W), row),
        pl.BlockSpec((tm, ATT_W), row),
        pl.BlockSpec((tm, ATT_W), row),
        pl.BlockSpec((None, tm // KEY_TILE, ATT_W, KEY_TILE),
                     lambda i: (i // tpb, i % tpb, 0, 0)),
        pl.BlockSpec((None, 2 * LANES, tm), grp),
        pl.BlockSpec((tm, IDX_DIM), row),
        pl.BlockSpec((tm, LANES), row),
        pl.BlockSpec((None, IDX_HEADS, tm), grp),
        pl.BlockSpec((tm, ATT_W), row),
        pl.BlockSpec((tm, ATT_W), row),
        pl.BlockSpec((tm, ATT_W), row),
        pl.BlockSpec((tm, ATT_W), row),
        pl.BlockSpec((tm, ATT_W), row),
    )
    in_specs = [
        pl.BlockSpec((tm, D_MODEL), row),
        pl.BlockSpec((1, D_MODEL), const),
        pl.BlockSpec(w1.shape, const),
        pl.BlockSpec(hd.shape, const),
        pl.BlockSpec(gh.shape, const),
        pl.BlockSpec(gidx.shape, const),
        pl.BlockSpec((tm, ATT_W), tab),
        pl.BlockSpec((tm, ATT_W), tab),
        pl.BlockSpec((tm, IDX_W), tab),
        pl.BlockSpec((tm, IDX_W), tab),
    ]
    return pl.pallas_call(
        _proj_kernel, out_shape=out_shape, grid=(n // tm,),
        in_specs=in_specs, out_specs=out_specs,
        compiler_params=_cparams(("parallel",)), name="proj",
    )(x, gattn, w1, hd, gh, gidx, cosq, sinq, cosi, sini)


def _dsa_kernel(qT_ref, iqT_ref, iwT_ref, k_ref, vT_ref, ikb_ref, tri_ref,
                o_ref, sc_ref, am_ref, s_ref, oT_ref, acc_ref, *, q0, n_valid, n_key_tiles, n_sel):
    j = pl.program_id(1)
    qbase = q0 + j * LANES
    n_t = jnp.minimum(lax.shift_right_logical(qbase + LANES + KEY_TILE - 1, 8), n_key_tiles)
    qchunk = lax.shift_right_logical(
        qbase + lax.broadcasted_iota(I32, (1, LANES), 1), 6)

    def rows(t):
        return pl.ds(pl.multiple_of(t * KEY_TILE, KEY_TILE), KEY_TILE)

    def admissible(t):
        kidx = t * KEY_TILE + lax.broadcasted_iota(I32, (KEY_TILE, LANES), 0)
        return jnp.where(kidx < n_valid, lax.shift_right_logical(kidx, 6), 1 << 30) <= qchunk

    def fold8(v):
        return v.reshape(KEY_TILE // 8, 8, v.shape[-1])

    zpad = jnp.zeros((LANES - IDX_DIM, LANES), BF16)
    rhs_idx = jnp.concatenate(
        [jnp.concatenate([iqT_ref[h * IDX_DIM:(h + 1) * IDX_DIM, :], zpad], axis=0)
         for h in range(IDX_HEADS)], axis=1)

    def score_body(t, carry):
        res = jnp.dot(ikb_ref[rows(t), :], rhs_idx, preferred_element_type=F32)
        acc = jnp.zeros((KEY_TILE, LANES), F32)
        for h in range(IDX_HEADS):
            acc = acc + jnp.maximum(res[:, h * LANES:(h + 1) * LANES], 0.0) * iwT_ref[h:h + 1, :]
        sc_ref[rows(t), :] = jnp.where(admissible(t), acc, NEG)
        return carry

    lax.fori_loop(0, n_t, score_body, 0)

    def count(pred):
        def body(t, c8):
            return c8 + jnp.sum(fold8(jnp.where(pred(sc_ref[rows(t), :]), 1, 0)), axis=0)
        c8 = lax.fori_loop(0, n_t, body, jnp.zeros((8, LANES), I32))
        return jnp.sum(c8, axis=0, keepdims=True)

    def bisect_body(it, p):
        c = p + lax.shift_left(jnp.int32(1), 31 - it)
        cf = pltpu.bitcast(jnp.where(c >= 0, c, c ^ 0x7FFFFFFF), F32)
        return jnp.where(count(lambda v: v >= cf) >= n_sel, c, p)

    p = lax.fori_loop(0, 32, bisect_body, jnp.full((1, LANES), INT_MIN, I32))
    thr = pltpu.bitcast(jnp.where(p >= 0, p, p ^ 0x7FFFFFFF), F32)

    def mask_simple():
        def body(t, carry):
            keep = jnp.where(sc_ref[rows(t), :] >= thr, 0.0, NEG)
            am_ref[rows(t), :] = jnp.where(admissible(t), keep, NEG)
            return carry
        lax.fori_loop(0, n_t, body, 0)

    def mask_ties():
        need = (n_sel - count(lambda v: v > thr)).astype(F32)

        def body(t, carry):
            v = sc_ref[rows(t), :]
            eq = v == thr
            eqf = jnp.where(eq, 1.0, 0.0)
            pre = jnp.dot(tri_ref[...], eqf.astype(BF16), preferred_element_type=F32) + carry
            tie = jnp.where(eq, jnp.where(pre < need, 0.0, NEG), NEG)
            keep = jnp.where(v > thr, 0.0, tie)
            am_ref[rows(t), :] = jnp.where(admissible(t), keep, NEG)
            return carry + jnp.sum(jnp.sum(fold8(eqf), axis=0), axis=0, keepdims=True)
        lax.fori_loop(0, n_t, body, jnp.zeros((1, LANES), F32))

    lax.cond(jnp.max(count(lambda v: v >= thr)) == n_sel, mask_simple, mask_ties)

    npair = N_HEADS // 2
    pw = 2 * LANES
    zq = jnp.zeros((HEAD_DIM, LANES), BF16)
    rhs = [jnp.concatenate(
        [jnp.concatenate([qT_ref[pr * LANES:pr * LANES + HEAD_DIM, :], zq], axis=1),
         jnp.concatenate([zq, qT_ref[pr * LANES + HEAD_DIM:(pr + 1) * LANES, :]], axis=1)], axis=0)
        for pr in range(npair)]

    def logit_body(t, m8):
        am = am_ref[rows(t), :]
        am2 = jnp.concatenate([am, am], axis=1)
        tops = []
        for pr in range(npair):
            s = jnp.dot(k_ref[rows(t), pr * LANES:(pr + 1) * LANES], rhs[pr],
                        preferred_element_type=F32) + am2
            s_ref[rows(t), pr * pw:(pr + 1) * pw] = s
            tops.append(jnp.max(fold8(s), axis=0))
        return jnp.maximum(m8, jnp.concatenate(tops, axis=1))

    m8 = lax.fori_loop(0, n_t, logit_body, jnp.full((8, npair * pw), NEG, F32))
    m = jnp.max(m8, axis=0, keepdims=True)
    acc_ref[...] = jnp.zeros_like(acc_ref)

    def pv_body(t, l8):
        sums = []
        for pr in range(npair):
            cols = slice(pr * pw, (pr + 1) * pw)
            pexp = jnp.exp(s_ref[rows(t), cols] - m[:, cols])
            sums.append(jnp.sum(fold8(pexp), axis=0))
            acc_ref[pr] += jnp.dot(vT_ref[t, pr * LANES:(pr + 1) * LANES, :], pexp.astype(BF16),
                                   preferred_element_type=F32)
        return l8 + jnp.concatenate(sums, axis=1)

    l8 = lax.fori_loop(0, n_t, pv_body, jnp.zeros((8, npair * pw), F32))
    inv = 1.0 / jnp.sum(l8, axis=0, keepdims=True)
    for pr in range(npair):
        lo = pr * LANES
        acc = acc_ref[pr]
        oT_ref[lo:lo + HEAD_DIM, :] = acc[:HEAD_DIM, :LANES] * inv[:, pr * pw:pr * pw + LANES]
        oT_ref[lo + HEAD_DIM:lo + LANES, :] = acc[HEAD_DIM:, LANES:] * inv[:, pr * pw + LANES:(pr + 1) * pw]

    o_ref[...] = oT_ref[...].T.astype(BF16)


def _dsa(qT, iqT, iwT, k, vT, ikb, tri, *, q0, n_valid):
    nb, _, tq = qT.shape
    lp = k.shape[1]
    qblk = lambda b, j: (b, 0, j)
    seq3 = lambda b, j: (b, 0, 0)
    seq4 = lambda b, j: (b, 0, 0, 0)
    kern = functools.partial(_dsa_kernel, q0=q0, n_valid=n_valid, n_key_tiles=lp // KEY_TILE,
                             n_sel=min(TOPK_MAX, n_valid // 4))
    return pl.pallas_call(
        kern, out_shape=jax.ShapeDtypeStruct((nb, tq, ATT_W), BF16),
        grid=(nb, tq // LANES),
        in_specs=[
            pl.BlockSpec((None, ATT_W, LANES), qblk),
            pl.BlockSpec((None, 2 * LANES, LANES), qblk),
            pl.BlockSpec((None, IDX_HEADS, LANES), qblk),
            pl.BlockSpec((None, lp, ATT_W), seq3),
            pl.BlockSpec((None, lp // KEY_TILE, ATT_W, KEY_TILE), seq4),
            pl.BlockSpec((None, lp, LANES), seq3),
            pl.BlockSpec((KEY_TILE, KEY_TILE), lambda b, j: (0, 0)),
        ],
        out_specs=pl.BlockSpec((None, LANES, ATT_W), lambda b, j: (b, j, 0)),
        scratch_shapes=[
            pltpu.VMEM((lp, LANES), F32),
            pltpu.VMEM((lp, LANES), F32),
            pltpu.VMEM((lp, N_HEADS * LANES), F32),
            pltpu.VMEM((ATT_W, LANES), F32),
            pltpu.VMEM((N_HEADS // 2, LANES, 2 * LANES), F32),
        ],
        compiler_params=_cparams(("parallel", "arbitrary")), name="dsa",
    )(qT, iqT, iwT, k, vT, ikb, tri)


def _band_kernel(q_ref, k0, k1, k2, k3, k4, v0, v1, v2, v3, v4, bias_ref, o_ref,
                 *, row_lo, row_hi):
    j = pl.program_id(1)
    kb = jnp.concatenate([r[...] for r in (k0, k1, k2, k3, k4)], axis=0)
    vb = jnp.concatenate([r[...] for r in (v0, v1, v2, v3, v4)], axis=0)
    row = j * LANES + lax.broadcasted_iota(I32, (1, BAND_KEYS), 1)
    valid = jnp.where(row >= row_lo, row, row_hi) < row_hi
    lane = lax.broadcasted_iota(I32, (1, LANES), 1)
    lo_half = lane < HEAD_DIM
    zero = jnp.zeros((), BF16)
    for pr in range(N_HEADS // 2):
        sl = slice(pr * LANES, (pr + 1) * LANES)
        qp = q_ref[:, sl]
        lhs = jnp.concatenate([jnp.where(lo_half, qp, zero), jnp.where(lo_half, zero, qp)], axis=0)
        s = lax.dot_general(lhs, kb[:, sl], (((1,), (1,)), ((), ())),
                            preferred_element_type=F32)
        bias2 = jnp.concatenate([bias_ref[2 * pr], bias_ref[2 * pr + 1]], axis=0)
        s = jnp.where(valid, s + bias2, NEG)
        m = jnp.max(s, axis=-1, keepdims=True)
        pexp = jnp.exp(s - m)
        inv = 1.0 / jnp.sum(pexp, axis=-1, keepdims=True)
        o = jnp.dot(pexp.astype(BF16), vb[:, sl], preferred_element_type=F32) * inv
        o_ref[:, sl] = jnp.where(lo_half, o[:LANES], o[LANES:]).astype(BF16)


def _band(q, kp, vp, bias, *, row_lo, row_hi):
    nb, tq, _ = q.shape
    kspecs = [pl.BlockSpec((None, LANES, ATT_W), (lambda b, j, c=c: (b, j + c, 0)))
              for c in range(BAND_KEYS // LANES)]
    kern = functools.partial(_band_kernel, row_lo=row_lo, row_hi=row_hi)
    return pl.pallas_call(
        kern, out_shape=jax.ShapeDtypeStruct((nb, tq, ATT_W), BF16),
        grid=(nb, tq // LANES),
        in_specs=[pl.BlockSpec((None, LANES, ATT_W), lambda b, j: (b, j, 0))]
        + kspecs + kspecs
        + [pl.BlockSpec(bias.shape, lambda b, j: (0, 0, 0))],
        out_specs=pl.BlockSpec((None, LANES, ATT_W), lambda b, j: (b, j, 0)),
        compiler_params=_cparams(("parallel", "parallel")), name="band",
    )(q, *([kp] * 5), *([vp] * 5), bias)


def _merge_kernel(x_ref, oa_ref, ob_ref, gattn_ref, wg_ref, wa_ref, wb_ref, wo_ref,
                  gffn_ref, wpqT_ref, c1_ref, c2_ref,
                  x2_ref, h2T_ref, s1_ref, s2_ref):
    x = x_ref[...]
    ms = jnp.mean(x * x, axis=-1, keepdims=True)
    h = (x * lax.rsqrt(ms + EPS) * gattn_ref[...]).astype(BF16)
    ga = jax.nn.sigmoid(jnp.dot(h, wg_ref[:, :D_MODEL], preferred_element_type=F32))
    gb = jax.nn.sigmoid(jnp.dot(h, wg_ref[:, D_MODEL:], preferred_element_type=F32))
    ma = jnp.dot(oa_ref[...], wa_ref[...], preferred_element_type=F32)
    mb = jnp.dot(ob_ref[...], wb_ref[...], preferred_element_type=F32)
    merged = (ga * ma + gb * mb).astype(BF16)
    x2 = x + jnp.dot(merged, wo_ref[...], preferred_element_type=F32)
    x2_ref[...] = x2
    ms2 = jnp.mean(x2 * x2, axis=-1, keepdims=True)
    h2T = (x2 * lax.rsqrt(ms2 + EPS) * gffn_ref[...]).T.astype(BF16)
    h2T_ref[...] = h2T
    half = PEER_DQ // 2
    for hh in range(PEER_HEADS):
        qT = jnp.dot(wpqT_ref[hh * PEER_DQ:(hh + 1) * PEER_DQ, :], h2T,
                     preferred_element_type=F32).astype(BF16)
        s1_ref[hh] = jnp.dot(c1_ref[...], qT[:half], preferred_element_type=F32)
        s2_ref[hh] = jnp.dot(c2_ref[...], qT[half:], preferred_element_type=F32)


def _merge(x, oa, ob, gattn, wg, wa, wb, wo, gffn, wpqT, c1, c2):
    n = x.shape[0]
    tm = ROW_TILE
    row = lambda i: (i, 0)
    const = lambda i: (0, 0)
    return pl.pallas_call(
        _merge_kernel,
        out_shape=(
            jax.ShapeDtypeStruct((n, D_MODEL), F32),
            jax.ShapeDtypeStruct((D_MODEL, n), BF16),
            jax.ShapeDtypeStruct((PEER_HEADS, N_KEYS, n), F32),
            jax.ShapeDtypeStruct((PEER_HEADS, N_KEYS, n), F32),
        ),
        grid=(n // tm,),
        in_specs=[
            pl.BlockSpec((tm, D_MODEL), row),
            pl.BlockSpec((tm, ATT_W), row),
            pl.BlockSpec((tm, ATT_W), row),
            pl.BlockSpec((1, D_MODEL), const),
            pl.BlockSpec(wg.shape, const),
            pl.BlockSpec(wa.shape, const),
            pl.BlockSpec(wb.shape, const),
            pl.BlockSpec(wo.shape, const),
            pl.BlockSpec((1, D_MODEL), const),
            pl.BlockSpec(wpqT.shape, const),
            pl.BlockSpec(c1.shape, const),
            pl.BlockSpec(c2.shape, const),
        ],
        out_specs=(
            pl.BlockSpec((tm, D_MODEL), row),
            pl.BlockSpec((D_MODEL, tm), lambda i: (0, i)),
            pl.BlockSpec((PEER_HEADS, N_KEYS, tm), lambda i: (0, 0, i)),
            pl.BlockSpec((PEER_HEADS, N_KEYS, tm), lambda i: (0, 0, i)),
        ),
        compiler_params=_cparams(("parallel",)), name="merge",
    )(x, oa, ob, gattn, wg, wa, wb, wo, gffn, wpqT, c1, c2)


def _pack_rows(x):
    return pltpu.bitcast(x, jnp.uint32)


def _unpack_rows(x):
    return pltpu.bitcast(x, BF16)


def _top16(x, distinct):
    idx = lax.broadcasted_iota(I32, x.shape, 0)
    rank = jnp.full(x.shape, PEER_TOPK, I32)
    rows = []
    for r in range(PEER_TOPK):
        m = jnp.max(x, axis=0, keepdims=True)
        rows.append(m)
        hit = x == m
        if not distinct:
            hit = idx == jnp.min(jnp.where(hit, idx, N_KEYS), axis=0, keepdims=True)
        rank = jnp.where(hit, r, rank)
        x = jnp.where(hit, -jnp.inf, x)
    return jnp.concatenate(rows, axis=0), rank


def _select_kernel(s1_ref, s2_ref, cnt_ref, e1_ref, r2_ref, e2_ref):
    sub8 = lax.broadcasted_iota(I32, (8, 1), 0).astype(F32)
    sub16 = lax.broadcasted_iota(I32, (PEER_TOPK, 1), 0).astype(F32)

    def count(mask):
        return jnp.sum(jnp.where(mask, 1.0, 0.0), axis=0, keepdims=True)

    def head_body(hh, carry):
        s1 = s1_ref[hh]
        s2 = s2_ref[hh]
        v1, r1 = _top16(s1, True)
        v2, r2 = _top16(s2, True)
        ranked = jnp.maximum(count(r1 < PEER_TOPK), count(r2 < PEER_TOPK))
        v1, r1, v2, r2 = lax.cond(
            jnp.max(ranked) > PEER_TOPK,
            lambda: _top16(s1, False) + _top16(s2, False),
            lambda: (v1, r1, v2, r2))
        pieces = [v1[0:1] + v2]
        for a in range(1, 8):
            piece = v1[a:a + 1] + v2[0:8]
            pieces.append(jnp.where(sub8 < PEER_TOPK // (a + 1), piece, -jnp.inf))
        pieces.append(v1[8:16] + v2[0:1])
        cand = jnp.concatenate(pieces, axis=0)
        rem = jnp.full((1, cand.shape[1]), float(PEER_TOPK), F32)
        tau = jnp.zeros((1, cand.shape[1]), F32)
        x = cand
        for _ in range(PEER_TOPK):
            m = jnp.max(x, axis=0, keepdims=True)
            hit = x == m
            tau = jnp.where(rem > 0, m, tau)
            rem = rem - count(hit)
            x = jnp.where(hit, -jnp.inf, x)
        need = PEER_TOPK - count(cand > tau)
        kept = []
        for a in range(PEER_TOPK):
            row = pieces[a] if a < 8 else pieces[8][a - 8:a - 7]
            ties = count(row == tau)
            kept.append(count(row > tau) + jnp.minimum(ties, jnp.maximum(need, 0.0)))
            need = need - ties
        smax = v1[0:1] + v2[0:1]
        z = jnp.sum(jnp.where(sub16 < kept[0], jnp.exp(pieces[0] - smax), 0.0), axis=0, keepdims=True)
        for a in range(1, 8):
            z = z + jnp.sum(jnp.where(sub8 < kept[a], jnp.exp(pieces[a] - smax), 0.0),
                            axis=0, keepdims=True)
        tail = jnp.concatenate(kept[8:], axis=0)
        z = z + jnp.sum(jnp.where(tail > 0, jnp.exp(pieces[8] - smax), 0.0), axis=0, keepdims=True)
        cnt = jnp.zeros(s1.shape, F32)
        for a in range(PEER_TOPK):
            cnt = jnp.where(r1 == a, kept[a], cnt)
        cnt_ref[hh] = cnt
        e1_ref[hh] = jnp.exp(s1 - v1[0:1]) * (1.0 / z)
        r2b = r2.astype(F32).astype(BF16)
        e2b = jnp.exp(s2 - v2[0:1]).astype(BF16)
        for c in range(s2.shape[1] // LANES):
            cs = slice(c * LANES, (c + 1) * LANES)
            r2_ref[hh, c] = _pack_rows(r2b[:, cs])
            e2_ref[hh, c] = _pack_rows(e2b[:, cs])
        return carry

    lax.fori_loop(0, PEER_HEADS, head_body, 0)


def _select(s1, s2):
    n = s1.shape[-1]
    tn = SEL_TILE
    blk = pl.BlockSpec((PEER_HEADS, N_KEYS, tn), lambda i: (0, 0, i))
    packed = jax.ShapeDtypeStruct((PEER_HEADS, n // LANES, N_KEYS // 2, LANES), jnp.uint32)
    pblk = pl.BlockSpec((PEER_HEADS, tn // LANES, N_KEYS // 2, LANES), lambda i: (0, i, 0, 0))
    return pl.pallas_call(
        _select_kernel,
        out_shape=(
            jax.ShapeDtypeStruct(s1.shape, F32),
            jax.ShapeDtypeStruct(s1.shape, F32),
            packed, packed,
        ),
        grid=(n // tn,),
        in_specs=[blk, blk],
        out_specs=(blk, blk, pblk, pblk),
        compiler_params=_cparams(("parallel",)), name="select",
    )(s1, s2)


def _peer_kernel(h2T_ref, u_ref, vT_ref, cnt_ref, e1_ref, r2_ref, e2_ref, x2_ref,
                 y_ref, acc_ref, wa0_ref, wa1_ref, z0_ref, z1_ref, rowc_ref, rowe_ref):
    e = pl.program_id(1)
    zero = jnp.minimum(e, 0)
    z_refs = (z0_ref, z1_ref)
    wa_refs = (wa0_ref, wa1_ref)

    @pl.when(e == 0)
    def _():
        acc_ref[...] = jnp.zeros_like(acc_ref)

    tn = h2T_ref.shape[1]
    nsub = PEER_EB // N_KEYS
    reps = N_KEYS // BF16_ROWS
    for sub in range(nsub):
        for hh in range(PEER_HEADS):
            rowc_ref[sub, hh] = _pack_rows(
                jnp.broadcast_to(cnt_ref[hh, sub:sub + 1, :], (BF16_ROWS, tn)).astype(BF16))
            rowe_ref[sub, hh] = _pack_rows(
                jnp.broadcast_to(e1_ref[hh, sub:sub + 1, :], (BF16_ROWS, tn)).astype(BF16))

    ngrp = PEER_EB // PEER_GRP

    def scores(g, slot):
        start = g * PEER_GRP if isinstance(g, int) else pl.multiple_of(g * PEER_GRP, PEER_GRP)
        z_refs[slot][0] = jnp.dot(u_ref[pl.ds(start, PEER_GRP), :], h2T_ref[...],
                                  preferred_element_type=F32)

    def weights(g, slot):
        for s in range(PEER_GRP // N_KEYS):
            sub = g * (PEER_GRP // N_KEYS) + s
            for c in range(tn // LANES):
                cs = slice(c * LANES, (c + 1) * LANES)
                zc = z_refs[slot][zero, s * N_KEYS:(s + 1) * N_KEYS, cs]
                a = (0.5 * zc * (1.0 + lax.erf(zc * (2.0 ** -0.5)))).astype(BF16)
                w = jnp.zeros((N_KEYS, LANES), BF16)
                for hh in range(PEER_HEADS):
                    cnt = jnp.concatenate([_unpack_rows(rowc_ref[sub + zero, hh, :, cs])] * reps, axis=0)
                    e1 = jnp.concatenate([_unpack_rows(rowe_ref[sub + zero, hh, :, cs])] * reps, axis=0)
                    w = w + jnp.where(_unpack_rows(r2_ref[hh, c]) < cnt,
                                      _unpack_rows(e2_ref[hh, c]) * e1, jnp.zeros((), BF16))
                wa_refs[slot][0, s * (N_KEYS // 2):(s + 1) * (N_KEYS // 2), cs] = _pack_rows(w * a)

    def down(g, slot):
        acc_ref[...] += jnp.dot(vT_ref[g], _unpack_rows(wa_refs[slot][zero]),
                                preferred_element_type=F32)

    scores(0, 0)
    for g in range(ngrp):
        if g + 1 < ngrp:
            scores(g + 1, (g + 1) % 2)
        weights(g, g % 2)
        if g > 0:
            down(g - 1, (g - 1) % 2)
    down(ngrp - 1, (ngrp - 1) % 2)

    @pl.when(e == pl.num_programs(1) - 1)
    def _():
        y_ref[...] = x2_ref[...] + acc_ref[...].T


def _peer(h2T, u, vT, cnt, e1, r2, e2, x2):
    n = x2.shape[0]
    tn = PEER_TOK
    nsub = PEER_EB // N_KEYS
    tok3 = pl.BlockSpec((PEER_HEADS, tn // LANES, N_KEYS // 2, LANES), lambda i, e: (0, i, 0, 0))
    key1 = pl.BlockSpec((PEER_HEADS, nsub, tn), lambda i, e: (0, e, i))
    return pl.pallas_call(
        _peer_kernel,
        out_shape=jax.ShapeDtypeStruct((n, D_MODEL), F32),
        grid=(n // tn, N_EXPERTS // PEER_EB),
        in_specs=[
            pl.BlockSpec((D_MODEL, tn), lambda i, e: (0, i)),
            pl.BlockSpec((PEER_EB, D_MODEL), lambda i, e: (e, 0)),
            pl.BlockSpec((PEER_EB // PEER_GRP, D_MODEL, PEER_GRP), lambda i, e: (e, 0, 0)),
            key1, key1, tok3, tok3,
            pl.BlockSpec((tn, D_MODEL), lambda i, e: (i, 0)),
        ],
        out_specs=pl.BlockSpec((tn, D_MODEL), lambda i, e: (i, 0)),
        scratch_shapes=[
            pltpu.VMEM((D_MODEL, tn), F32),
            pltpu.VMEM((1, PEER_GRP // 2, tn), jnp.uint32),
            pltpu.VMEM((1, PEER_GRP // 2, tn), jnp.uint32),
            pltpu.VMEM((1, PEER_GRP, tn), F32),
            pltpu.VMEM((1, PEER_GRP, tn), F32),
            pltpu.VMEM((nsub, PEER_HEADS, BF16_ROWS // 2, tn), jnp.uint32),
            pltpu.VMEM((nsub, PEER_HEADS, BF16_ROWS // 2, tn), jnp.uint32),
        ],
        compiler_params=_cparams(("parallel", "arbitrary")), name="peer",
    )(h2T, u, vT, cnt, e1, r2, e2, x2)


def _rope_tables(pos):
    posf = pos.astype(F32)[:, None]
    t = pos.shape[0]

    def tab(half, reps):
        inv = jnp.power(ROPE_THETA, -jnp.arange(half, dtype=F32) / half)
        ang = posf * inv[None, :]
        cos = jnp.cos(ang)
        sin = jnp.sin(ang)
        return (jnp.tile(jnp.concatenate([cos, cos], axis=1), (1, reps)),
                jnp.tile(jnp.concatenate([-sin, sin], axis=1), (1, reps)))

    cosq, sinq = tab(HEAD_DIM // 2, N_HEADS)
    cosi, sini = tab(IDX_DIM // 2, IDX_HEADS + 1)
    pad = IDX_W - cosi.shape[1]
    cosi = jnp.concatenate([cosi, jnp.ones((t, pad), F32)], axis=1)
    sini = jnp.concatenate([sini, jnp.zeros((t, pad), F32)], axis=1)
    return cosq, sinq, cosi, sini


def _band_bias(rel_bias):
    qi = np.arange(LANES)[:, None]
    kj = np.arange(BAND_KEYS)[None, :]
    ok = (kj // CHUNK >= qi // CHUNK) & (kj // CHUNK <= qi // CHUNK + BAND_PAST // CHUNK)
    n = BAND_KEYS + LANES - 1
    m = np.arange(n)
    diag = rel_bias.astype(F32)[:, np.clip(BAND_PAST + LANES - 1 - m, -REL_CLIP, REL_CLIP) + REL_CLIP]
    wrapped = jnp.tile(jnp.pad(diag, ((0, 0), (0, 1))), (1, LANES))[:, :LANES * n]
    toep = wrapped.reshape(-1, LANES, n)[:, :, LANES - 1:LANES - 1 + BAND_KEYS]
    return jnp.where(jnp.asarray(ok)[None], toep, NEG)


def _layer_weights(g_attn, w_in, g_qa, g_ka, g_ik, g_qb, g_kb, rel_bias, w_br_a, w_br_b,
                   w_out, g_ffn, w_pq, c1, c2, u, v):
    a0 = 3 * ATT_W
    i0 = a0 + IDX_HEADS * IDX_DIM + IDX_DIM + IDX_HEADS
    b0 = i0 + 3 * ATT_W
    idx_cols = i0 - a0
    w1 = jnp.concatenate(
        [w_in[:, :a0], w_in[:, i0:b0], w_in[:, a0:i0],
         jnp.zeros((D_MODEL, IDX_W - idx_cols), w_in.dtype)], axis=1).astype(BF16)
    wg = w_in[:, b0:].astype(BF16)
    hd = jnp.asarray(np.kron(np.eye(N_HEADS), np.ones((HEAD_DIM, HEAD_DIM))) / HEAD_DIM, BF16)
    gh = jnp.stack([jnp.tile(g, N_HEADS) for g in (g_qa, g_ka, g_qb, g_kb)]).astype(F32)
    gidx = jnp.concatenate(
        [g_ik.astype(F32), jnp.full((IDX_HEADS,), IDX_HEADS ** -0.5, F32),
         jnp.ones((LANES - IDX_DIM - IDX_HEADS,), F32)])[None, :]
    tri = jnp.asarray(np.tril(np.ones((KEY_TILE, KEY_TILE)), -1), BF16)
    return dict(
        w1=w1, wg=wg, hd=hd, gh=gh, gidx=gidx, tri=tri,
        gattn=g_attn.astype(F32)[None, :], gffn=g_ffn.astype(F32)[None, :],
        bias=_band_bias(rel_bias),
        wa=w_br_a.astype(BF16), wb=w_br_b.astype(BF16), wo=w_out.astype(BF16),
        wpqT=w_pq.T.astype(BF16), c1=c1.astype(BF16), c2=c2.astype(BF16),
        u=u.astype(BF16),
        vT=v.astype(BF16).reshape(N_EXPERTS // PEER_GRP, PEER_GRP, D_MODEL).transpose(0, 2, 1))


def _ffn(x, oa, ob, w):
    x2, h2T, s1, s2 = _merge(x, oa, ob, w["gattn"], w["wg"], w["wa"], w["wb"], w["wo"],
                             w["gffn"], w["wpqT"], w["c1"], w["c2"])
    cnt, e1, r2, e2 = _select(s1, s2)
    return _peer(h2T, w["u"], w["vT"], cnt, e1, r2, e2, x2)


def _prompt_layer(x, w):
    b, t, _ = x.shape
    xf = x.reshape(b * t, D_MODEL)
    tabs = _rope_tables(jnp.arange(t))
    (aqT, ak, akb, av, avT, iqT, ik, ikb, iwT, bq, bk, bkb, bv, bvb) = _proj(
        xf, b, t, w["w1"], w["gattn"], w["hd"], w["gh"], w["gidx"], *tabs)
    oa = _dsa(aqT, iqT, iwT, akb.reshape(b, t, ATT_W), avT, ikb.reshape(b, t, LANES),
              w["tri"], q0=0, n_valid=t)
    front = ((0, 0), (BAND_PAST, 0), (0, 0))
    ob = _band(bq.reshape(b, t, ATT_W), jnp.pad(bkb.reshape(b, t, ATT_W), front),
               jnp.pad(bvb.reshape(b, t, ATT_W), front), w["bias"],
               row_lo=BAND_PAST, row_hi=BAND_PAST + t)
    y = _ffn(xf, oa.reshape(b * t, ATT_W), ob.reshape(b * t, ATT_W), w)
    keep = min(BAND_PAST, t)
    heads = lambda a: a.reshape(b, t, N_HEADS, HEAD_DIM)
    return y.reshape(b, t, D_MODEL), (
        heads(ak), heads(av), ik.reshape(b, t, IDX_DIM),
        heads(bk)[:, t - keep:], heads(bv)[:, t - keep:])


def _sample_layer(x, ca_k, ca_v, ca_ik, cb_k, cb_v, w):
    b, t, _ = x.shape
    n = b * t
    past = ca_k.shape[1]
    xf = x.reshape(n, D_MODEL)
    tabs = _rope_tables(jnp.tile(past + jnp.arange(t), b))
    (aqT, ak, akb, av, avT, iqT, ik, ikb, iwT, bq, bk, bkb, bv, bvb) = _proj(
        xf, 1, n, w["w1"], w["gattn"], w["hd"], w["gh"], w["gidx"], *tabs)

    def per_seq_T(aT, rows):
        a = aT[0].T.reshape(b, t, rows)
        return jnp.pad(a, ((0, 0), (0, LANES - t), (0, 0))).transpose(0, 2, 1)

    lk = past + t
    lp = -(-lk // KEY_TILE) * KEY_TILE
    padk = ((0, 0), (0, lp - lk), (0, 0))
    k_all = jnp.pad(jnp.concatenate(
        [ca_k.reshape(b, past, ATT_W).astype(BF16), akb.reshape(b, t, ATT_W)], axis=1), padk)
    v_all = jnp.pad(jnp.concatenate(
        [ca_v.reshape(b, past, ATT_W).astype(BF16), av.astype(BF16).reshape(b, t, ATT_W)],
        axis=1), padk)
    vT_all = v_all.reshape(b, lp // KEY_TILE, KEY_TILE, ATT_W).transpose(0, 1, 3, 2)
    ik_all = jnp.pad(jnp.concatenate(
        [jnp.pad(ca_ik.astype(BF16), ((0, 0), (0, 0), (0, LANES - IDX_DIM))),
         ikb.reshape(b, t, LANES)], axis=1), padk)
    oa = _dsa(per_seq_T(aqT, ATT_W), per_seq_T(iqT, 2 * LANES), per_seq_T(iwT, IDX_HEADS),
              k_all, vT_all, ik_all, w["tri"], q0=past, n_valid=lk)[:, :t]

    pb = cb_k.shape[1]
    padb = ((0, 0), (BAND_PAST - pb, LANES - t), (0, 0))
    qpad = jnp.pad(bq.reshape(b, t, ATT_W), ((0, 0), (0, LANES - t), (0, 0)))
    kb_all = jnp.pad(jnp.concatenate(
        [cb_k.reshape(b, pb, ATT_W).astype(BF16), bkb.reshape(b, t, ATT_W)], axis=1), padb)
    vb_all = jnp.pad(jnp.concatenate(
        [cb_v.reshape(b, pb, ATT_W).astype(BF16), bvb.reshape(b, t, ATT_W)], axis=1), padb)
    ob = _band(qpad, kb_all, vb_all, w["bias"],
               row_lo=BAND_PAST - pb, row_hi=BAND_PAST + t)[:, :t]

    y = _ffn(xf, oa.reshape(n, ATT_W), ob.reshape(n, ATT_W), w)
    keep = min(BAND_PAST, t)
    heads = lambda a: a.reshape(b, t, N_HEADS, HEAD_DIM)
    return y.reshape(b, t, D_MODEL), (
        heads(ak), heads(av), ik.reshape(b, t, IDX_DIM),
        heads(bk)[:, t - keep:], heads(bv)[:, t - keep:])


def kernel(x_prompt, x_sample, cache_a_k, cache_a_v, cache_a_ik, cache_b_k, cache_b_v,
           g_attn, w_in, g_qa, g_ka, g_ik, g_qb, g_kb, rel_bias, w_br_a, w_br_b, w_out,
           g_ffn, w_pq, peer_c1, peer_c2, peer_u, peer_v):
    depth = w_in.shape[0]
    xp, xs = x_prompt, x_sample
    sp, ss = [], []
    for l in range(depth):
        w = _layer_weights(g_attn[l], w_in[l], g_qa[l], g_ka[l], g_ik[l], g_qb[l], g_kb[l],
                           rel_bias[l], w_br_a[l], w_br_b[l], w_out[l], g_ffn[l], w_pq[l],
                           peer_c1[l], peer_c2[l], peer_u[l], peer_v[l])
        xp, st_p = _prompt_layer(xp, w)
        xs, st_s = _sample_layer(xs, cache_a_k[l], cache_a_v[l], cache_a_ik[l],
                                 cache_b_k[l], cache_b_v[l], w)
        sp.append(st_p)
        ss.append(st_s)
    stack = lambda sts, i: jnp.stack([s[i] for s in sts])
    return (xp, xs) + tuple(stack(sp, i) for i in range(5)) + tuple(stack(ss, i) for i in range(5))
```

```python
import functools

import jax
import jax.numpy as jnp
import numpy as np
from jax import lax
from jax.experimental import pallas as pl
from jax.experimental.pallas import tpu as pltpu

F32 = jnp.float32
BF16 = jnp.bfloat16
I32 = jnp.int32

D_MODEL = 1024
HEAD_DIM = 64
N_HEADS = 8
ATT_W = N_HEADS * HEAD_DIM
IDX_HEADS = 8
IDX_DIM = 32
IDX_W = 384
CHUNK = 64
TOPK_MAX = 256
BAND_PAST = 512
BAND_KEYS = BAND_PAST + 128
REL_CLIP = 128
PEER_HEADS = 8
PEER_DQ = 256
N_KEYS = 128
N_EXPERTS = N_KEYS * N_KEYS
PEER_TOPK = 16
ROPE_THETA = 10000.0
EPS = 1e-6
NEG = -1e30
INT_MIN = -(2 ** 31)

LANES = 128
BF16_ROWS = 16
KEY_TILE = 256
ROW_TILE = 512
SEL_TILE = 256
PEER_TOK = 512
PEER_EB = 2048
PEER_GRP = 256
VMEM_LIMIT = 56 * 1024 * 1024


def _cparams(sem):
    return pltpu.CompilerParams(dimension_semantics=sem, vmem_limit_bytes=VMEM_LIMIT)


def _rope_chunks(y, cos, sin, half):
    lane = lax.broadcasted_iota(I32, (1, LANES), 1)
    lo = (lane % (2 * half)) < half
    out = []
    for c in range(y.shape[1] // LANES):
        sl = slice(c * LANES, (c + 1) * LANES)
        yc = y[:, sl]
        partner = jnp.where(lo, pltpu.roll(yc, LANES - half, 1), pltpu.roll(yc, half, 1))
        out.append(yc * cos[:, sl] + partner * sin[:, sl])
    return jnp.concatenate(out, axis=1)


def _proj_kernel(x_ref, gattn_ref, w1_ref, hd_ref, gh_ref, gidx_ref,
                 cosq_ref, sinq_ref, cosi_ref, sini_ref,
                 aqT_ref, ak_ref, akb_ref, av_ref, avT_ref, iqT_ref, ik_ref, ikb_ref,
                 iwT_ref, bq_ref, bk_ref, bkb_ref, bv_ref, bvb_ref):
    x = x_ref[...]
    ms = jnp.mean(x * x, axis=-1, keepdims=True)
    h = (x * lax.rsqrt(ms + EPS) * gattn_ref[...]).astype(BF16)

    def seg(i):
        return jnp.dot(h, w1_ref[:, i * ATT_W:(i + 1) * ATT_W], preferred_element_type=F32)

    def headnorm(y, gi):
        hms = jnp.dot((y * y).astype(BF16), hd_ref[...], preferred_element_type=F32)
        return y * lax.rsqrt(hms + EPS) * gh_ref[gi:gi + 1, :]

    cosq = cosq_ref[...]
    sinq = sinq_ref[...]

    aq = _rope_chunks(headnorm(seg(0), 0), cosq, sinq, HEAD_DIM // 2) * (HEAD_DIM ** -0.5)
    aqT_ref[...] = aq.T.astype(BF16)

    ak = _rope_chunks(headnorm(seg(1), 1), cosq, sinq, HEAD_DIM // 2)
    ak_ref[...] = ak
    akb_ref[...] = ak.astype(BF16)

    av = seg(2)
    av_ref[...] = av
    for c in range(av.shape[0] // KEY_TILE):
        avT_ref[c] = av[c * KEY_TILE:(c + 1) * KEY_TILE, :].T.astype(BF16)

    bq_ref[...] = (headnorm(seg(3), 2) * (HEAD_DIM ** -0.5)).astype(BF16)
    bk = headnorm(seg(4), 3)
    bk_ref[...] = bk
    bkb_ref[...] = bk.astype(BF16)
    bv = seg(5)
    bv_ref[...] = bv
    bvb_ref[...] = bv.astype(BF16)

    s = jnp.dot(h, w1_ref[:, 6 * ATT_W:6 * ATT_W + IDX_W], preferred_element_type=F32)
    lane = lax.broadcasted_iota(I32, (1, LANES), 1)
    is_ik = lane < IDX_DIM
    c2 = s[:, 2 * LANES:3 * LANES]
    ikms = jnp.sum(jnp.where(is_ik, c2 * c2, 0.0), axis=-1, keepdims=True) * (1.0 / IDX_DIM)
    c2n = c2 * jnp.where(is_ik, lax.rsqrt(ikms + EPS), 1.0) * gidx_ref[...]
    sn = jnp.concatenate([s[:, :2 * LANES], c2n], axis=1)
    r = _rope_chunks(sn, cosi_ref[...], sini_ref[...], IDX_DIM // 2)
    iqT_ref[...] = r[:, :2 * LANES].T.astype(BF16)
    c2r = r[:, 2 * LANES:]
    ik_ref[...] = c2r[:, :IDX_DIM]
    ikb_ref[...] = jnp.where(is_ik, c2r, 0.0).astype(BF16)
    iwT_ref[...] = c2r.T[IDX_DIM:IDX_DIM + IDX_HEADS, :]


def _proj(x, nb, tb, w1, gattn, hd, gh, gidx, cosq, sinq, cosi, sini):
    n = x.shape[0]
    tm = ROW_TILE
    tpb = tb // tm
    row = lambda i: (i, 0)
    tab = lambda i: (i % tpb, 0)
    const = lambda i: (0, 0)
    grp = lambda i: (i // tpb, 0, i % tpb)
    out_shape = (
        jax.ShapeDtypeStruct((nb, ATT_W, tb), BF16),
        jax.ShapeDtypeStruct((n, ATT_W), F32),
        jax.ShapeDtypeStruct((n, ATT_W), BF16),
        jax.ShapeDtypeStruct((n, ATT_W), F32),
        jax.ShapeDtypeStruct((nb, tb // KEY_TILE, ATT_W, KEY_TILE), BF16),
        jax.ShapeDtypeStruct((nb, 2 * LANES, tb), BF16),
        jax.ShapeDtypeStruct((n, IDX_DIM), F32),
        jax.ShapeDtypeStruct((n, LANES), BF16),
        jax.ShapeDtypeStruct((nb, IDX_HEADS, tb), F32),
        jax.ShapeDtypeStruct((n, ATT_W), BF16),
        jax.ShapeDtypeStruct((n, ATT_W), F32),
        jax.ShapeDtypeStruct((n, ATT_W), BF16),
        jax.ShapeDtypeStruct((n, ATT_W), F32),
        jax.ShapeDtypeStruct((n, ATT_W), BF16),
    )
    out_specs = (
        pl.BlockSpec((None, ATT_W, tm), grp),
        pl.BlockSpec((tm, ATT_W), row),
        pl.BlockSpec((tm, ATT_W), row),
        pl.BlockSpec((tm, ATT_W), row),
        pl.BlockSpec((None, tm // KEY_TILE, ATT_W, KEY_TILE),
                     lambda i: (i // tpb, i % tpb, 0, 0)),
        pl.BlockSpec((None, 2 * LANES, tm), grp),
        pl.BlockSpec((tm, IDX_DIM), row),
        pl.BlockSpec((tm, LANES), row),
        pl.BlockSpec((None, IDX_HEADS, tm), grp),
        pl.BlockSpec((tm, ATT_W), row),
        pl.BlockSpec((tm, ATT_W), row),
        pl.BlockSpec((tm, ATT_W), row),
        pl.BlockSpec((tm, ATT_W), row),
        pl.BlockSpec((tm, ATT_W), row),
    )
    in_specs = [
        pl.BlockSpec((tm, D_MODEL), row),
        pl.BlockSpec((1, D_MODEL), const),
        pl.BlockSpec(w1.shape, const),
        pl.BlockSpec(hd.shape, const),
        pl.BlockSpec(gh.shape, const),
        pl.BlockSpec(gidx.shape, const),
        pl.BlockSpec((tm, ATT_W), tab),
        pl.BlockSpec((tm, ATT_W), tab),
        pl.BlockSpec((tm, IDX_W), tab),
        pl.BlockSpec((tm, IDX_W), tab),
    ]
    return pl.pallas_call(
        _proj_kernel, out_shape=out_shape, grid=(n // tm,),
        in_specs=in_specs, out_specs=out_specs,
        compiler_params=_cparams(("parallel",)), name="proj",
    )(x, gattn, w1, hd, gh, gidx, cosq, sinq, cosi, sini)


def _dsa_kernel(qT_ref, iqT_ref, iwT_ref, k_ref, vT_ref, ikb_ref, tri_ref,
                o_ref, sc_ref, am_ref, s_ref, oT_ref, acc_ref, *, q0, n_valid, n_key_tiles, n_sel):
    j = pl.program_id(1)
    qbase = q0 + j * LANES
    n_t = jnp.minimum(lax.shift_right_logical(qbase + LANES + KEY_TILE - 1, 8), n_key_tiles)
    qchunk = lax.shift_right_logical(
        qbase + lax.broadcasted_iota(I32, (1, LANES), 1), 6)

    def rows(t):
        return pl.ds(pl.multiple_of(t * KEY_TILE, KEY_TILE), KEY_TILE)

    def admissible(t):
        kidx = t * KEY_TILE + lax.broadcasted_iota(I32, (KEY_TILE, LANES), 0)
        return jnp.where(kidx < n_valid, lax.shift_right_logical(kidx, 6), 1 << 30) <= qchunk

    def fold8(v):
        return v.reshape(KEY_TILE // 8, 8, v.shape[-1])

    zpad = jnp.zeros((LANES - IDX_DIM, LANES), BF16)
    rhs_idx = jnp.concatenate(
        [jnp.concatenate([iqT_ref[h * IDX_DIM:(h + 1) * IDX_DIM, :], zpad], axis=0)
         for h in range(IDX_HEADS)], axis=1)

    def score_body(t, carry):
        res = jnp.dot(ikb_ref[rows(t), :], rhs_idx, preferred_element_type=F32)
        acc = jnp.zeros((KEY_TILE, LANES), F32)
        for h in range(IDX_HEADS):
            acc = acc + jnp.maximum(res[:, h * LANES:(h + 1) * LANES], 0.0) * iwT_ref[h:h + 1, :]
        sc_ref[rows(t), :] = jnp.where(admissible(t), acc, NEG)
        return carry

    lax.fori_loop(0, n_t, score_body, 0)

    def count(pred):
        def body(t, c8):
            return c8 + jnp.sum(fold8(jnp.where(pred(sc_ref[rows(t), :]), 1, 0)), axis=0)
        c8 = lax.fori_loop(0, n_t, body, jnp.zeros((8, LANES), I32))
        return jnp.sum(c8, axis=0, keepdims=True)

    def bisect_body(it, p):
        c = p + lax.shift_left(jnp.int32(1), 31 - it)
        cf = pltpu.bitcast(jnp.where(c >= 0, c, c ^ 0x7FFFFFFF), F32)
        return jnp.where(count(lambda v: v >= cf) >= n_sel, c, p)

    p = lax.fori_loop(0, 32, bisect_body, jnp.full((1, LANES), INT_MIN, I32))
    thr = pltpu.bitcast(jnp.where(p >= 0, p, p ^ 0x7FFFFFFF), F32)

    def mask_simple():
        def body(t, carry):
            keep = jnp.where(sc_ref[rows(t), :] >= thr, 0.0, NEG)
            am_ref[rows(t), :] = jnp.where(admissible(t), keep, NEG)
            return carry
        lax.fori_loop(0, n_t, body, 0)

    def mask_ties():
        need = (n_sel - count(lambda v: v > thr)).astype(F32)

        def body(t, carry):
            v = sc_ref[rows(t), :]
            eq = v == thr
            eqf = jnp.where(eq, 1.0, 0.0)
            pre = jnp.dot(tri_ref[...], eqf.astype(BF16), preferred_element_type=F32) + carry
            tie = jnp.where(eq, jnp.where(pre < need, 0.0, NEG), NEG)
            keep = jnp.where(v > thr, 0.0, tie)
            am_ref[rows(t), :] = jnp.where(admissible(t), keep, NEG)
            return carry + jnp.sum(jnp.sum(fold8(eqf), axis=0), axis=0, keepdims=True)
        lax.fori_loop(0, n_t, body, jnp.zeros((1, LANES), F32))

    lax.cond(jnp.max(count(lambda v: v >= thr)) == n_sel, mask_simple, mask_ties)

    npair = N_HEADS // 2
    pw = 2 * LANES
    zq = jnp.zeros((HEAD_DIM, LANES), BF16)
    rhs = [jnp.concatenate(
        [jnp.concatenate([qT_ref[pr * LANES:pr * LANES + HEAD_DIM, :], zq], axis=1),
         jnp.concatenate([zq, qT_ref[pr * LANES + HEAD_DIM:(pr + 1) * LANES, :]], axis=1)], axis=0)
        for pr in range(npair)]

    def logit_body(t, m8):
        am = am_ref[rows(t), :]
        am2 = jnp.concatenate([am, am], axis=1)
        tops = []
        for pr in range(npair):
            s = jnp.dot(k_ref[rows(t), pr * LANES:(pr + 1) * LANES], rhs[pr],
                        preferred_element_type=F32) + am2
            s_ref[rows(t), pr * pw:(pr + 1) * pw] = s
            tops.append(jnp.max(fold8(s), axis=0))
        return jnp.maximum(m8, jnp.concatenate(tops, axis=1))

    m8 = lax.fori_loop(0, n_t, logit_body, jnp.full((8, npair * pw), NEG, F32))
    m = jnp.max(m8, axis=0, keepdims=True)
    acc_ref[...] = jnp.zeros_like(acc_ref)

    def pv_body(t, l8):
        sums = []
        for pr in range(npair):
            cols = slice(pr * pw, (pr + 1) * pw)
            pexp = jnp.exp(s_ref[rows(t), cols] - m[:, cols])
            sums.append(jnp.sum(fold8(pexp), axis=0))
            acc_ref[pr] += jnp.dot(vT_ref[t, pr * LANES:(pr + 1) * LANES, :], pexp.astype(BF16),
                                   preferred_element_type=F32)
        return l8 + jnp.concatenate(sums, axis=1)

    l8 = lax.fori_loop(0, n_t, pv_body, jnp.zeros((8, npair * pw), F32))
    inv = 1.0 / jnp.sum(l8, axis=0, keepdims=True)
    for pr in range(npair):
        lo = pr * LANES
        acc = acc_ref[pr]
        oT_ref[lo:lo + HEAD_DIM, :] = acc[:HEAD_DIM, :LANES] * inv[:, pr * pw:pr * pw + LANES]
        oT_ref[lo + HEAD_DIM:lo + LANES, :] = acc[HEAD_DIM:, LANES:] * inv[:, pr * pw + LANES:(pr + 1) * pw]

    o_ref[...] = oT_ref[...].T.astype(BF16)


def _dsa(qT, iqT, iwT, k, vT, ikb, tri, *, q0, n_valid):
    nb, _, tq = qT.shape
    lp = k.shape[1]
    qblk = lambda b, j: (b, 0, j)
    seq3 = lambda b, j: (b, 0, 0)
    seq4 = lambda b, j: (b, 0, 0, 0)
    kern = functools.partial(_dsa_kernel, q0=q0, n_valid=n_valid, n_key_tiles=lp // KEY_TILE,
                             n_sel=min(TOPK_MAX, n_valid // 4))
    return pl.pallas_call(
        kern, out_shape=jax.ShapeDtypeStruct((nb, tq, ATT_W), BF16),
        grid=(nb, tq // LANES),
        in_specs=[
            pl.BlockSpec((None, ATT_W, LANES), qblk),
            pl.BlockSpec((None, 2 * LANES, LANES), qblk),
            pl.BlockSpec((None, IDX_HEADS, LANES), qblk),
            pl.BlockSpec((None, lp, ATT_W), seq3),
            pl.BlockSpec((None, lp // KEY_TILE, ATT_W, KEY_TILE), seq4),
            pl.BlockSpec((None, lp, LANES), seq3),
            pl.BlockSpec((KEY_TILE, KEY_TILE), lambda b, j: (0, 0)),
        ],
        out_specs=pl.BlockSpec((None, LANES, ATT_W), lambda b, j: (b, j, 0)),
        scratch_shapes=[
            pltpu.VMEM((lp, LANES), F32),
            pltpu.VMEM((lp, LANES), F32),
            pltpu.VMEM((lp, N_HEADS * LANES), F32),
            pltpu.VMEM((ATT_W, LANES), F32),
            pltpu.VMEM((N_HEADS // 2, LANES, 2 * LANES), F32),
        ],
        compiler_params=_cparams(("parallel", "arbitrary")), name="dsa",
    )(qT, iqT, iwT, k, vT, ikb, tri)


def _band_kernel(q_ref, k0, k1, k2, k3, k4, v0, v1, v2, v3, v4, bias_ref, o_ref,
                 *, row_lo, row_hi):
    j = pl.program_id(1)
    kb = jnp.concatenate([r[...] for r in (k0, k1, k2, k3, k4)], axis=0)
    vb = jnp.concatenate([r[...] for r in (v0, v1, v2, v3, v4)], axis=0)
    row = j * LANES + lax.broadcasted_iota(I32, (1, BAND_KEYS), 1)
    valid = jnp.where(row >= row_lo, row, row_hi) < row_hi
    lane = lax.broadcasted_iota(I32, (1, LANES), 1)
    lo_half = lane < HEAD_DIM
    zero = jnp.zeros((), BF16)
    for pr in range(N_HEADS // 2):
        sl = slice(pr * LANES, (pr + 1) * LANES)
        qp = q_ref[:, sl]
        lhs = jnp.concatenate([jnp.where(lo_half, qp, zero), jnp.where(lo_half, zero, qp)], axis=0)
        s = lax.dot_general(lhs, kb[:, sl], (((1,), (1,)), ((), ())),
                            preferred_element_type=F32)
        bias2 = jnp.concatenate([bias_ref[2 * pr], bias_ref[2 * pr + 1]], axis=0)
        s = jnp.where(valid, s + bias2, NEG)
        m = jnp.max(s, axis=-1, keepdims=True)
        pexp = jnp.exp(s - m)
        inv = 1.0 / jnp.sum(pexp, axis=-1, keepdims=True)
        o = jnp.dot(pexp.astype(BF16), vb[:, sl], preferred_element_type=F32) * inv
        o_ref[:, sl] = jnp.where(lo_half, o[:LANES], o[LANES:]).astype(BF16)


def _band(q, kp, vp, bias, *, row_lo, row_hi):
    nb, tq, _ = q.shape
    kspecs = [pl.BlockSpec((None, LANES, ATT_W), (lambda b, j, c=c: (b, j + c, 0)))
              for c in range(BAND_KEYS // LANES)]
    kern = functools.partial(_band_kernel, row_lo=row_lo, row_hi=row_hi)
    return pl.pallas_call(
        kern, out_shape=jax.ShapeDtypeStruct((nb, tq, ATT_W), BF16),
        grid=(nb, tq // LANES),
        in_specs=[pl.BlockSpec((None, LANES, ATT_W), lambda b, j: (b, j, 0))]
        + kspecs + kspecs
        + [pl.BlockSpec(bias.shape, lambda b, j: (0, 0, 0))],
        out_specs=pl.BlockSpec((None, LANES, ATT_W), lambda b, j: (b, j, 0)),
        compiler_params=_cparams(("parallel", "parallel")), name="band",
    )(q, *([kp] * 5), *([vp] * 5), bias)


def _merge_kernel(x_ref, oa_ref, ob_ref, gattn_ref, wg_ref, wa_ref, wb_ref, wo_ref,
                  gffn_ref, wpqT_ref, c1_ref, c2_ref,
                  x2_ref, h2T_ref, s1_ref, s2_ref):
    x = x_ref[...]
    ms = jnp.mean(x * x, axis=-1, keepdims=True)
    h = (x * lax.rsqrt(ms + EPS) * gattn_ref[...]).astype(BF16)
    ga = jax.nn.sigmoid(jnp.dot(h, wg_ref[:, :D_MODEL], preferred_element_type=F32))
    gb = jax.nn.sigmoid(jnp.dot(h, wg_ref[:, D_MODEL:], preferred_element_type=F32))
    ma = jnp.dot(oa_ref[...], wa_ref[...], preferred_element_type=F32)
    mb = jnp.dot(ob_ref[...], wb_ref[...], preferred_element_type=F32)
    merged = (ga * ma + gb * mb).astype(BF16)
    x2 = x + jnp.dot(merged, wo_ref[...], preferred_element_type=F32)
    x2_ref[...] = x2
    ms2 = jnp.mean(x2 * x2, axis=-1, keepdims=True)
    h2T = (x2 * lax.rsqrt(ms2 + EPS) * gffn_ref[...]).T.astype(BF16)
    h2T_ref[...] = h2T
    half = PEER_DQ // 2
    for hh in range(PEER_HEADS):
        qT = jnp.dot(wpqT_ref[hh * PEER_DQ:(hh + 1) * PEER_DQ, :], h2T,
                     preferred_element_type=F32).astype(BF16)
        s1_ref[hh] = jnp.dot(c1_ref[...], qT[:half], preferred_element_type=F32)
        s2_ref[hh] = jnp.dot(c2_ref[...], qT[half:], preferred_element_type=F32)


def _merge(x, oa, ob, gattn, wg, wa, wb, wo, gffn, wpqT, c1, c2):
    n = x.shape[0]
    tm = ROW_TILE
    row = lambda i: (i, 0)
    const = lambda i: (0, 0)
    return pl.pallas_call(
        _merge_kernel,
        out_shape=(
            jax.ShapeDtypeStruct((n, D_MODEL), F32),
            jax.ShapeDtypeStruct((D_MODEL, n), BF16),
            jax.ShapeDtypeStruct((PEER_HEADS, N_KEYS, n), F32),
            jax.ShapeDtypeStruct((PEER_HEADS, N_KEYS, n), F32),
        ),
        grid=(n // tm,),
        in_specs=[
            pl.BlockSpec((tm, D_MODEL), row),
            pl.BlockSpec((tm, ATT_W), row),
            pl.BlockSpec((tm, ATT_W), row),
            pl.BlockSpec((1, D_MODEL), const),
            pl.BlockSpec(wg.shape, const),
            pl.BlockSpec(wa.shape, const),
            pl.BlockSpec(wb.shape, const),
            pl.BlockSpec(wo.shape, const),
            pl.BlockSpec((1, D_MODEL), const),
            pl.BlockSpec(wpqT.shape, const),
            pl.BlockSpec(c1.shape, const),
            pl.BlockSpec(c2.shape, const),
        ],
        out_specs=(
            pl.BlockSpec((tm, D_MODEL), row),
            pl.BlockSpec((D_MODEL, tm), lambda i: (0, i)),
            pl.BlockSpec((PEER_HEADS, N_KEYS, tm), lambda i: (0, 0, i)),
            pl.BlockSpec((PEER_HEADS, N_KEYS, tm), lambda i: (0, 0, i)),
        ),
        compiler_params=_cparams(("parallel",)), name="merge",
    )(x, oa, ob, gattn, wg, wa, wb, wo, gffn, wpqT, c1, c2)


def _pack_rows(x):
    return pltpu.bitcast(x, jnp.uint32)


def _unpack_rows(x):
    return pltpu.bitcast(x, BF16)


def _top16(x, distinct):
    idx = lax.broadcasted_iota(I32, x.shape, 0)
    rank = jnp.full(x.shape, PEER_TOPK, I32)
    rows = []
    for r in range(PEER_TOPK):
        m = jnp.max(x, axis=0, keepdims=True)
        rows.append(m)
        hit = x == m
        if not distinct:
            hit = idx == jnp.min(jnp.where(hit, idx, N_KEYS), axis=0, keepdims=True)
        rank = jnp.where(hit, r, rank)
        x = jnp.where(hit, -jnp.inf, x)
    return jnp.concatenate(rows, axis=0), rank


def _select_kernel(s1_ref, s2_ref, cnt_ref, e1_ref, r2_ref, e2_ref):
    sub8 = lax.broadcasted_iota(I32, (8, 1), 0).astype(F32)
    sub16 = lax.broadcasted_iota(I32, (PEER_TOPK, 1), 0).astype(F32)

    def count(mask):
        return jnp.sum(jnp.where(mask, 1.0, 0.0), axis=0, keepdims=True)

    def head_body(hh, carry):
        s1 = s1_ref[hh]
        s2 = s2_ref[hh]
        v1, r1 = _top16(s1, True)
        v2, r2 = _top16(s2, True)
        ranked = jnp.maximum(count(r1 < PEER_TOPK), count(r2 < PEER_TOPK))
        v1, r1, v2, r2 = lax.cond(
            jnp.max(ranked) > PEER_TOPK,
            lambda: _top16(s1, False) + _top16(s2, False),
            lambda: (v1, r1, v2, r2))
        pieces = [v1[0:1] + v2]
        for a in range(1, 8):
            piece = v1[a:a + 1] + v2[0:8]
            pieces.append(jnp.where(sub8 < PEER_TOPK // (a + 1), piece, -jnp.inf))
        pieces.append(v1[8:16] + v2[0:1])
        cand = jnp.concatenate(pieces, axis=0)
        rem = jnp.full((1, cand.shape[1]), float(PEER_TOPK), F32)
        tau = jnp.zeros((1, cand.shape[1]), F32)
        x = cand
        for _ in range(PEER_TOPK):
            m = jnp.max(x, axis=0, keepdims=True)
            hit = x == m
            tau = jnp.where(rem > 0, m, tau)
            rem = rem - count(hit)
            x = jnp.where(hit, -jnp.inf, x)
        need = PEER_TOPK - count(cand > tau)
        kept = []
        for a in range(PEER_TOPK):
            row = pieces[a] if a < 8 else pieces[8][a - 8:a - 7]
            ties = count(row == tau)
            kept.append(count(row > tau) + jnp.minimum(ties, jnp.maximum(need, 0.0)))
            need = need - ties
        smax = v1[0:1] + v2[0:1]
        z = jnp.sum(jnp.where(sub16 < kept[0], jnp.exp(pieces[0] - smax), 0.0), axis=0, keepdims=True)
        for a in range(1, 8):
            z = z + jnp.sum(jnp.where(sub8 < kept[a], jnp.exp(pieces[a] - smax), 0.0),
                            axis=0, keepdims=True)
        tail = jnp.concatenate(kept[8:], axis=0)
        z = z + jnp.sum(jnp.where(tail > 0, jnp.exp(pieces[8] - smax), 0.0), axis=0, keepdims=True)
        cnt = jnp.zeros(s1.shape, F32)
        for a in range(PEER_TOPK):
            cnt = jnp.where(r1 == a, kept[a], cnt)
        cnt_ref[hh] = cnt
        e1_ref[hh] = jnp.exp(s1 - v1[0:1]) * (1.0 / z)
        r2b = r2.astype(F32).astype(BF16)
        e2b = jnp.exp(s2 - v2[0:1]).astype(BF16)
        for c in range(s2.shape[1] // LANES):
            cs = slice(c * LANES, (c + 1) * LANES)
            r2_ref[hh, c] = _pack_rows(r2b[:, cs])
            e2_ref[hh, c] = _pack_rows(e2b[:, cs])
        return carry

    lax.fori_loop(0, PEER_HEADS, head_body, 0)


def _select(s1, s2):
    n = s1.shape[-1]
    tn = SEL_TILE
    blk = pl.BlockSpec((PEER_HEADS, N_KEYS, tn), lambda i: (0, 0, i))
    packed = jax.ShapeDtypeStruct((PEER_HEADS, n // LANES, N_KEYS // 2, LANES), jnp.uint32)
    pblk = pl.BlockSpec((PEER_HEADS, tn // LANES, N_KEYS // 2, LANES), lambda i: (0, i, 0, 0))
    return pl.pallas_call(
        _select_kernel,
        out_shape=(
            jax.ShapeDtypeStruct(s1.shape, F32),
            jax.ShapeDtypeStruct(s1.shape, F32),
            packed, packed,
        ),
        grid=(n // tn,),
        in_specs=[blk, blk],
        out_specs=(blk, blk, pblk, pblk),
        compiler_params=_cparams(("parallel",)), name="select",
    )(s1, s2)


def _peer_kernel(h2T_ref, u_ref, vT_ref, cnt_ref, e1_ref, r2_ref, e2_ref, x2_ref,
                 y_ref, acc_ref, wa0_ref, wa1_ref, z0_ref, z1_ref, rowc_ref, rowe_ref):
    e = pl.program_id(1)
    zero = jnp.minimum(e, 0)
    z_refs = (z0_ref, z1_ref)
    wa_refs = (wa0_ref, wa1_ref)

    @pl.when(e == 0)
    def _():
        acc_ref[...] = jnp.zeros_like(acc_ref)

    tn = h2T_ref.shape[1]
    nsub = PEER_EB // N_KEYS
    reps = N_KEYS // BF16_ROWS
    for sub in range(nsub):
        for hh in range(PEER_HEADS):
            rowc_ref[sub, hh] = _pack_rows(
                jnp.broadcast_to(cnt_ref[hh, sub:sub + 1, :], (BF16_ROWS, tn)).astype(BF16))
            rowe_ref[sub, hh] = _pack_rows(
                jnp.broadcast_to(e1_ref[hh, sub:sub + 1, :], (BF16_ROWS, tn)).astype(BF16))

    ngrp = PEER_EB // PEER_GRP

    def scores(g, slot):
        start = g * PEER_GRP if isinstance(g, int) else pl.multiple_of(g * PEER_GRP, PEER_GRP)
        z_refs[slot][0] = jnp.dot(u_ref[pl.ds(start, PEER_GRP), :], h2T_ref[...],
                                  preferred_element_type=F32)

    def weights(g, slot):
        for s in range(PEER_GRP // N_KEYS):
            sub = g * (PEER_GRP // N_KEYS) + s
            for c in range(tn // LANES):
                cs = slice(c * LANES, (c + 1) * LANES)
                zc = z_refs[slot][zero, s * N_KEYS:(s + 1) * N_KEYS, cs]
                a = (0.5 * zc * (1.0 + lax.erf(zc * (2.0 ** -0.5)))).astype(BF16)
                w = jnp.zeros((N_KEYS, LANES), BF16)
                for hh in range(PEER_HEADS):
                    cnt = jnp.concatenate([_unpack_rows(rowc_ref[sub + zero, hh, :, cs])] * reps, axis=0)
                    e1 = jnp.concatenate([_unpack_rows(rowe_ref[sub + zero, hh, :, cs])] * reps, axis=0)
                    w = w + jnp.where(_unpack_rows(r2_ref[hh, c]) < cnt,
                                      _unpack_rows(e2_ref[hh, c]) * e1, jnp.zeros((), BF16))
                wa_refs[slot][0, s * (N_KEYS // 2):(s + 1) * (N_KEYS // 2), cs] = _pack_rows(w * a)

    def down(g, slot):
        acc_ref[...] += jnp.dot(vT_ref[g], _unpack_rows(wa_refs[slot][zero]),
                                preferred_element_type=F32)

    scores(0, 0)
    for g in range(ngrp):
        if g + 1 < ngrp:
            scores(g + 1, (g + 1) % 2)
        weights(g, g % 2)
        if g > 0:
            down(g - 1, (g - 1) % 2)
    down(ngrp - 1, (ngrp - 1) % 2)

    @pl.when(e == pl.num_programs(1) - 1)
    def _():
        y_ref[...] = x2_ref[...] + acc_ref[...].T


def _peer(h2T, u, vT, cnt, e1, r2, e2, x2):
    n = x2.shape[0]
    tn = PEER_TOK
    nsub = PEER_EB // N_KEYS
    tok3 = pl.BlockSpec((PEER_HEADS, tn // LANES, N_KEYS // 2, LANES), lambda i, e: (0, i, 0, 0))
    key1 = pl.BlockSpec((PEER_HEADS, nsub, tn), lambda i, e: (0, e, i))
    return pl.pallas_call(
        _peer_kernel,
        out_shape=jax.ShapeDtypeStruct((n, D_MODEL), F32),
        grid=(n // tn, N_EXPERTS // PEER_EB),
        in_specs=[
            pl.BlockSpec((D_MODEL, tn), lambda i, e: (0, i)),
            pl.BlockSpec((PEER_EB, D_MODEL), lambda i, e: (e, 0)),
            pl.BlockSpec((PEER_EB // PEER_GRP, D_MODEL, PEER_GRP), lambda i, e: (e, 0, 0)),
            key1, key1, tok3, tok3,
            pl.BlockSpec((tn, D_MODEL), lambda i, e: (i, 0)),
        ],
        out_specs=pl.BlockSpec((tn, D_MODEL), lambda i, e: (i, 0)),
        scratch_shapes=[
            pltpu.VMEM((D_MODEL, tn), F32),
            pltpu.VMEM((1, PEER_GRP // 2, tn), jnp.uint32),
            pltpu.VMEM((1, PEER_GRP // 2, tn), jnp.uint32),
            pltpu.VMEM((1, PEER_GRP, tn), F32),
            pltpu.VMEM((1, PEER_GRP, tn), F32),
            pltpu.VMEM((nsub, PEER_HEADS, BF16_ROWS // 2, tn), jnp.uint32),
            pltpu.VMEM((nsub, PEER_HEADS, BF16_ROWS // 2, tn), jnp.uint32),
        ],
        compiler_params=_cparams(("parallel", "arbitrary")), name="peer",
    )(h2T, u, vT, cnt, e1, r2, e2, x2)


def _rope_tables(pos):
    posf = pos.astype(F32)[:, None]
    t = pos.shape[0]

    def tab(half, reps):
        inv = jnp.power(ROPE_THETA, -jnp.arange(half, dtype=F32) / half)
        ang = posf * inv[None, :]
        cos = jnp.cos(ang)
        sin = jnp.sin(ang)
        return (jnp.tile(jnp.concatenate([cos, cos], axis=1), (1, reps)),
                jnp.tile(jnp.concatenate([-sin, sin], axis=1), (1, reps)))

    cosq, sinq = tab(HEAD_DIM // 2, N_HEADS)
    cosi, sini = tab(IDX_DIM // 2, IDX_HEADS + 1)
    pad = IDX_W - cosi.shape[1]
    cosi = jnp.concatenate([cosi, jnp.ones((t, pad), F32)], axis=1)
    sini = jnp.concatenate([sini, jnp.zeros((t, pad), F32)], axis=1)
    return cosq, sinq, cosi, sini


def _band_bias(rel_bias):
    qi = np.arange(LANES)[:, None]
    kj = np.arange(BAND_KEYS)[None, :]
    ok = (kj // CHUNK >= qi // CHUNK) & (kj // CHUNK <= qi // CHUNK + BAND_PAST // CHUNK)
    n = BAND_KEYS + LANES - 1
    m = np.arange(n)
    diag = rel_bias.astype(F32)[:, np.clip(BAND_PAST + LANES - 1 - m, -REL_CLIP, REL_CLIP) + REL_CLIP]
    wrapped = jnp.tile(jnp.pad(diag, ((0, 0), (0, 1))), (1, LANES))[:, :LANES * n]
    toep = wrapped.reshape(-1, LANES, n)[:, :, LANES - 1:LANES - 1 + BAND_KEYS]
    return jnp.where(jnp.asarray(ok)[None], toep, NEG)


def _layer_weights(g_attn, w_in, g_qa, g_ka, g_ik, g_qb, g_kb, rel_bias, w_br_a, w_br_b,
                   w_out, g_ffn, w_pq, c1, c2, u, v):
    a0 = 3 * ATT_W
    i0 = a0 + IDX_HEADS * IDX_DIM + IDX_DIM + IDX_HEADS
    b0 = i0 + 3 * ATT_W
    idx_cols = i0 - a0
    w1 = jnp.concatenate(
        [w_in[:, :a0], w_in[:, i0:b0], w_in[:, a0:i0],
         jnp.zeros((D_MODEL, IDX_W - idx_cols), w_in.dtype)], axis=1).astype(BF16)
    wg = w_in[:, b0:].astype(BF16)
    hd = jnp.asarray(np.kron(np.eye(N_HEADS), np.ones((HEAD_DIM, HEAD_DIM))) / HEAD_DIM, BF16)
    gh = jnp.stack([jnp.tile(g, N_HEADS) for g in (g_qa, g_ka, g_qb, g_kb)]).astype(F32)
    gidx = jnp.concatenate(
        [g_ik.astype(F32), jnp.full((IDX_HEADS,), IDX_HEADS ** -0.5, F32),
         jnp.ones((LANES - IDX_DIM - IDX_HEADS,), F32)])[None, :]
    tri = jnp.asarray(np.tril(np.ones((KEY_TILE, KEY_TILE)), -1), BF16)
    return dict(
        w1=w1, wg=wg, hd=hd, gh=gh, gidx=gidx, tri=tri,
        gattn=g_attn.astype(F32)[None, :], gffn=g_ffn.astype(F32)[None, :],
        bias=_band_bias(rel_bias),
        wa=w_br_a.astype(BF16), wb=w_br_b.astype(BF16), wo=w_out.astype(BF16),
        wpqT=w_pq.T.astype(BF16), c1=c1.astype(BF16), c2=c2.astype(BF16),
        u=u.astype(BF16),
        vT=v.astype(BF16).reshape(N_EXPERTS // PEER_GRP, PEER_GRP, D_MODEL).transpose(0, 2, 1))


def _ffn(x, oa, ob, w):
    x2, h2T, s1, s2 = _merge(x, oa, ob, w["gattn"], w["wg"], w["wa"], w["wb"], w["wo"],
                             w["gffn"], w["wpqT"], w["c1"], w["c2"])
    cnt, e1, r2, e2 = _select(s1, s2)
    return _peer(h2T, w["u"], w["vT"], cnt, e1, r2, e2, x2)


def _prompt_layer(x, w):
    b, t, _ = x.shape
    xf = x.reshape(b * t, D_MODEL)
    tabs = _rope_tables(jnp.arange(t))
    (aqT, ak, akb, av, avT, iqT, ik, ikb, iwT, bq, bk, bkb, bv, bvb) = _proj(
        xf, b, t, w["w1"], w["gattn"], w["hd"], w["gh"], w["gidx"], *tabs)
    oa = _dsa(aqT, iqT, iwT, akb.reshape(b, t, ATT_W), avT, ikb.reshape(b, t, LANES),
              w["tri"], q0=0, n_valid=t)
    front = ((0, 0), (BAND_PAST, 0), (0, 0))
    ob = _band(bq.reshape(b, t, ATT_W), jnp.pad(bkb.reshape(b, t, ATT_W), front),
               jnp.pad(bvb.reshape(b, t, ATT_W), front), w["bias"],
               row_lo=BAND_PAST, row_hi=BAND_PAST + t)
    y = _ffn(xf, oa.reshape(b * t, ATT_W), ob.reshape(b * t, ATT_W), w)
    keep = min(BAND_PAST, t)
    heads = lambda a: a.reshape(b, t, N_HEADS, HEAD_DIM)
    return y.reshape(b, t, D_MODEL), (
        heads(ak), heads(av), ik.reshape(b, t, IDX_DIM),
        heads(bk)[:, t - keep:], heads(bv)[:, t - keep:])


def _sample_layer(x, ca_k, ca_v, ca_ik, cb_k, cb_v, w):
    b, t, _ = x.shape
    n = b * t
    past = ca_k.shape[1]
    xf = x.reshape(n, D_MODEL)
    tabs = _rope_tables(jnp.tile(past + jnp.arange(t), b))
    (aqT, ak, akb, av, avT, iqT, ik, ikb, iwT, bq, bk, bkb, bv, bvb) = _proj(
        xf, 1, n, w["w1"], w["gattn"], w["hd"], w["gh"], w["gidx"], *tabs)

    def per_seq_T(aT, rows):
        a = aT[0].T.reshape(b, t, rows)
        return jnp.pad(a, ((0, 0), (0, LANES - t), (0, 0))).transpose(0, 2, 1)

    lk = past + t
    lp = -(-lk // KEY_TILE) * KEY_TILE
    padk = ((0, 0), (0, lp - lk), (0, 0))
    k_all = jnp.pad(jnp.concatenate(
        [ca_k.reshape(b, past, ATT_W).astype(BF16), akb.reshape(b, t, ATT_W)], axis=1), padk)
    v_all = jnp.pad(jnp.concatenate(
        [ca_v.reshape(b, past, ATT_W).astype(BF16), av.astype(BF16).reshape(b, t, ATT_W)],
        axis=1), padk)
    vT_all = v_all.reshape(b, lp // KEY_TILE, KEY_TILE, ATT_W).transpose(0, 1, 3, 2)
    ik_all = jnp.pad(jnp.concatenate(
        [jnp.pad(ca_ik.astype(BF16), ((0, 0), (0, 0), (0, LANES - IDX_DIM))),
         ikb.reshape(b, t, LANES)], axis=1), padk)
    oa = _dsa(per_seq_T(aqT, ATT_W), per_seq_T(iqT, 2 * LANES), per_seq_T(iwT, IDX_HEADS),
              k_all, vT_all, ik_all, w["tri"], q0=past, n_valid=lk)[:, :t]

    pb = cb_k.shape[1]
    padb = ((0, 0), (BAND_PAST - pb, LANES - t), (0, 0))
    qpad = jnp.pad(bq.reshape(b, t, ATT_W), ((0, 0), (0, LANES - t), (0, 0)))
    kb_all = jnp.pad(jnp.concatenate(
        [cb_k.reshape(b, pb, ATT_W).astype(BF16), bkb.reshape(b, t, ATT_W)], axis=1), padb)
    vb_all = jnp.pad(jnp.concatenate(
        [cb_v.reshape(b, pb, ATT_W).astype(BF16), bvb.reshape(b, t, ATT_W)], axis=1), padb)
    ob = _band(qpad, kb_all, vb_all, w["bias"],
               row_lo=BAND_PAST - pb, row_hi=BAND_PAST + t)[:, :t]

    y = _ffn(xf, oa.reshape(n, ATT_W), ob.reshape(n, ATT_W), w)
    keep = min(BAND_PAST, t)
    heads = lambda a: a.reshape(b, t, N_HEADS, HEAD_DIM)
    return y.reshape(b, t, D_MODEL), (
        heads(ak), heads(av), ik.reshape(b, t, IDX_DIM),
        heads(bk)[:, t - keep:], heads(bv)[:, t - keep:])


def kernel(x_prompt, x_sample, cache_a_k, cache_a_v, cache_a_ik, cache_b_k, cache_b_v,
           g_attn, w_in, g_qa, g_ka, g_ik, g_qb, g_kb, rel_bias, w_br_a, w_br_b, w_out,
           g_ffn, w_pq, peer_c1, peer_c2, peer_u, peer_v):
    depth = w_in.shape[0]
    xp, xs = x_prompt, x_sample
    sp, ss = [], []
    for l in range(depth):
        w = _layer_weights(g_attn[l], w_in[l], g_qa[l], g_ka[l], g_ik[l], g_qb[l], g_kb[l],
                           rel_bias[l], w_br_a[l], w_br_b[l], w_out[l], g_ffn[l], w_pq[l],
                           peer_c1[l], peer_c2[l], peer_u[l], peer_v[l])
        xp, st_p = _prompt_layer(xp, w)
        xs, st_s = _sample_layer(xs, cache_a_k[l], cache_a_v[l], cache_a_ik[l],
                                 cache_b_k[l], cache_b_v[l], w)
        sp.append(st_p)
        ss.append(st_s)
    stack = lambda sts, i: jnp.stack([s[i] for s in sts])
    return (xp, xs) + tuple(stack(sp, i) for i in range(5)) + tuple(stack(ss, i) for i in range(5))
```

```python
import functools

import jax
import jax.numpy as jnp
import numpy as np
from jax import lax
from jax.experimental import pallas as pl
from jax.experimental.pallas import tpu as pltpu

F32 = jnp.float32
BF16 = jnp.bfloat16
I32 = jnp.int32

D_MODEL = 1024
HEAD_DIM = 64
N_HEADS = 8
ATT_W = N_HEADS * HEAD_DIM
IDX_HEADS = 8
IDX_DIM = 32
IDX_W = 384
CHUNK = 64
TOPK_MAX = 256
BAND_PAST = 512
BAND_KEYS = BAND_PAST + 128
REL_CLIP = 128
PEER_HEADS = 8
PEER_DQ = 256
N_KEYS = 128
N_EXPERTS = N_KEYS * N_KEYS
PEER_TOPK = 16
ROPE_THETA = 10000.0
EPS = 1e-6
NEG = -1e30
INT_MIN = -(2 ** 31)

LANES = 128
BF16_ROWS = 16
KEY_TILE = 256
ROW_TILE = 512
SEL_TILE = 256
PEER_TOK = 1024
PEER_EB = 1024
PEER_GRP = 256
VMEM_LIMIT = 56 * 1024 * 1024


def _cparams(sem):
    return pltpu.CompilerParams(dimension_semantics=sem, vmem_limit_bytes=VMEM_LIMIT)


def _rope_chunks(y, cos, sin, half):
    lane = lax.broadcasted_iota(I32, (1, LANES), 1)
    lo = (lane % (2 * half)) < half
    out = []
    for c in range(y.shape[1] // LANES):
        sl = slice(c * LANES, (c + 1) * LANES)
        yc = y[:, sl]
        partner = jnp.where(lo, pltpu.roll(yc, LANES - half, 1), pltpu.roll(yc, half, 1))
        out.append(yc * cos[:, sl] + partner * sin[:, sl])
    return jnp.concatenate(out, axis=1)


def _proj_kernel(x_ref, gattn_ref, w1_ref, hd_ref, gh_ref, gidx_ref,
                 cosq_ref, sinq_ref, cosi_ref, sini_ref,
                 aqT_ref, ak_ref, akb_ref, av_ref, avT_ref, iqT_ref, ik_ref, ikb_ref,
                 iwT_ref, bq_ref, bk_ref, bkb_ref, bv_ref, bvb_ref):
    x = x_ref[...]
    ms = jnp.mean(x * x, axis=-1, keepdims=True)
    h = (x * lax.rsqrt(ms + EPS) * gattn_ref[...]).astype(BF16)

    def seg(i):
        return jnp.dot(h, w1_ref[:, i * ATT_W:(i + 1) * ATT_W], preferred_element_type=F32)

    def headnorm(y, gi):
        hms = jnp.dot((y * y).astype(BF16), hd_ref[...], preferred_element_type=F32)
        return y * lax.rsqrt(hms + EPS) * gh_ref[gi:gi + 1, :]

    cosq = cosq_ref[...]
    sinq = sinq_ref[...]

    aq = _rope_chunks(headnorm(seg(0), 0), cosq, sinq, HEAD_DIM // 2) * (HEAD_DIM ** -0.5)
    aqT_ref[...] = aq.T.astype(BF16)

    ak = _rope_chunks(headnorm(seg(1), 1), cosq, sinq, HEAD_DIM // 2)
    ak_ref[...] = ak
    akb_ref[...] = ak.astype(BF16)

    av = seg(2)
    av_ref[...] = av
    for c in range(av.shape[0] // KEY_TILE):
        avT_ref[c] = av[c * KEY_TILE:(c + 1) * KEY_TILE, :].T.astype(BF16)

    bq_ref[...] = (headnorm(seg(3), 2) * (HEAD_DIM ** -0.5)).astype(BF16)
    bk = headnorm(seg(4), 3)
    bk_ref[...] = bk
    bkb_ref[...] = bk.astype(BF16)
    bv = seg(5)
    bv_ref[...] = bv
    bvb_ref[...] = bv.astype(BF16)

    s = jnp.dot(h, w1_ref[:, 6 * ATT_W:6 * ATT_W + IDX_W], preferred_element_type=F32)
    lane = lax.broadcasted_iota(I32, (1, LANES), 1)
    is_ik = lane < IDX_DIM
    c2 = s[:, 2 * LANES:3 * LANES]
    ikms = jnp.sum(jnp.where(is_ik, c2 * c2, 0.0), axis=-1, keepdims=True) * (1.0 / IDX_DIM)
    c2n = c2 * jnp.where(is_ik, lax.rsqrt(ikms + EPS), 1.0) * gidx_ref[...]
    sn = jnp.concatenate([s[:, :2 * LANES], c2n], axis=1)
    r = _rope_chunks(sn, cosi_ref[...], sini_ref[...], IDX_DIM // 2)
    iqT_ref[...] = r[:, :2 * LANES].T.astype(BF16)
    c2r = r[:, 2 * LANES:]
    ik_ref[...] = c2r[:, :IDX_DIM]
    ikb_ref[...] = jnp.where(is_ik, c2r, 0.0).astype(BF16)
    iwT_ref[...] = c2r.T[IDX_DIM:IDX_DIM + IDX_HEADS, :]


def _proj(x, nb, tb, w1, gattn, hd, gh, gidx, cosq, sinq, cosi, sini):
    n = x.shape[0]
    tm = ROW_TILE
    tpb = tb // tm
    row = lambda i: (i, 0)
    tab = lambda i: (i % tpb, 0)
    const = lambda i: (0, 0)
    grp = lambda i: (i // tpb, 0, i % tpb)
    out_shape = (
        jax.ShapeDtypeStruct((nb, ATT_W, tb), BF16),
        jax.ShapeDtypeStruct((n, ATT_W), F32),
        jax.ShapeDtypeStruct((n, ATT_W), BF16),
        jax.ShapeDtypeStruct((n, ATT_W), F32),
        jax.ShapeDtypeStruct((nb, tb // KEY_TILE, ATT_W, KEY_TILE), BF16),
        jax.ShapeDtypeStruct((nb, 2 * LANES, tb), BF16),
        jax.ShapeDtypeStruct((n, IDX_DIM), F32),
        jax.ShapeDtypeStruct((n, LANES), BF16),
        jax.ShapeDtypeStruct((nb, IDX_HEADS, tb), F32),
        jax.ShapeDtypeStruct((n, ATT_W), BF16),
        jax.ShapeDtypeStruct((n, ATT_W), F32),
        jax.ShapeDtypeStruct((n, ATT_W), BF16),
        jax.ShapeDtypeStruct((n, ATT_W), F32),
        jax.ShapeDtypeStruct((n, ATT_W), BF16),
    )
    out_specs = (
        pl.BlockSpec((None, ATT_W, tm), grp),
        pl.BlockSpec((tm, ATT_W), row),
        pl.BlockSpec((tm, ATT_W), row),
        pl.BlockSpec((tm, ATT_W), row),
        pl.BlockSpec((None, tm // KEY_TILE, ATT_W, KEY_TILE),
                     lambda i: (i // tpb, i % tpb, 0, 0)),
        pl.BlockSpec((None, 2 * LANES, tm), grp),
        pl.BlockSpec((tm, IDX_DIM), row),
        pl.BlockSpec((tm, LANES), row),
        pl.BlockSpec((None, IDX_HEADS, tm), grp),
        pl.BlockSpec((tm, ATT_W), row),
        pl.BlockSpec((tm, ATT_W), row),
        pl.BlockSpec((tm, ATT_W), row),
        pl.BlockSpec((tm, ATT_W), row),
        pl.BlockSpec((tm, ATT_W), row),
    )
    in_specs = [
        pl.BlockSpec((tm, D_MODEL), row),
        pl.BlockSpec((1, D_MODEL), const),
        pl.BlockSpec(w1.shape, const),
        pl.BlockSpec(hd.shape, const),
        pl.BlockSpec(gh.shape, const),
        pl.BlockSpec(gidx.shape, const),
        pl.BlockSpec((tm, ATT_W), tab),
        pl.BlockSpec((tm, ATT_W), tab),
        pl.BlockSpec((tm, IDX_W), tab),
        pl.BlockSpec((tm, IDX_W), tab),
    ]
    return pl.pallas_call(
        _proj_kernel, out_shape=out_shape, grid=(n // tm,),
        in_specs=in_specs, out_specs=out_specs,
        compiler_params=_cparams(("parallel",)), name="proj",
    )(x, gattn, w1, hd, gh, gidx, cosq, sinq, cosi, sini)


def _dsa_kernel(qT_ref, iqT_ref, iwT_ref, k_ref, vT_ref, ikb_ref, tri_ref,
                o_ref, sc_ref, am_ref, s_ref, oT_ref, acc_ref, *, q0, n_valid, n_key_tiles, n_sel):
    j = pl.program_id(1)
    qbase = q0 + j * LANES
    n_t = jnp.minimum(lax.shift_right_logical(qbase + LANES + KEY_TILE - 1, 8), n_key_tiles)
    qchunk = lax.shift_right_logical(
        qbase + lax.broadcasted_iota(I32, (1, LANES), 1), 6)

    def rows(t):
        return pl.ds(pl.multiple_of(t * KEY_TILE, KEY_TILE), KEY_TILE)

    def admissible(t):
        kidx = t * KEY_TILE + lax.broadcasted_iota(I32, (KEY_TILE, LANES), 0)
        return jnp.where(kidx < n_valid, lax.shift_right_logical(kidx, 6), 1 << 30) <= qchunk

    def fold8(v):
        return v.reshape(KEY_TILE // 8, 8, v.shape[-1])

    zpad = jnp.zeros((LANES - IDX_DIM, LANES), BF16)
    rhs_idx = jnp.concatenate(
        [jnp.concatenate([iqT_ref[h * IDX_DIM:(h + 1) * IDX_DIM, :], zpad], axis=0)
         for h in range(IDX_HEADS)], axis=1)

    def score_body(t, carry):
        res = jnp.dot(ikb_ref[rows(t), :], rhs_idx, preferred_element_type=F32)
        acc = jnp.zeros((KEY_TILE, LANES), F32)
        for h in range(IDX_HEADS):
            acc = acc + jnp.maximum(res[:, h * LANES:(h + 1) * LANES], 0.0) * iwT_ref[h:h + 1, :]
        sc_ref[rows(t), :] = jnp.where(admissible(t), acc, NEG)
        return carry

    lax.fori_loop(0, n_t, score_body, 0)
    sc_ref[rows(n_t), :] = jnp.full((KEY_TILE, LANES), -jnp.inf, F32)

    def count(pred):
        def body(t2, c8):
            two = pl.ds(pl.multiple_of(t2 * (2 * KEY_TILE), 2 * KEY_TILE), 2 * KEY_TILE)
            hits = jnp.where(pred(sc_ref[two, :]), 1, 0)
            return c8 + jnp.sum(hits.reshape(2 * KEY_TILE // 8, 8, LANES), axis=0)
        c8 = lax.fori_loop(0, lax.shift_right_logical(n_t + 1, 1), body, jnp.zeros((8, LANES), I32))
        return jnp.sum(c8, axis=0, keepdims=True)

    def bisect_body(it, p):
        c = p + lax.shift_left(jnp.int32(1), 31 - it)
        cf = pltpu.bitcast(jnp.where(c >= 0, c, c ^ 0x7FFFFFFF), F32)
        return jnp.where(count(lambda v: v >= cf) >= n_sel, c, p)

    p = lax.fori_loop(0, 32, bisect_body, jnp.full((1, LANES), INT_MIN, I32))
    thr = pltpu.bitcast(jnp.where(p >= 0, p, p ^ 0x7FFFFFFF), F32)

    def mask_simple():
        def body(t, carry):
            keep = jnp.where(sc_ref[rows(t), :] >= thr, 0.0, NEG)
            am_ref[rows(t), :] = jnp.where(admissible(t), keep, NEG)
            return carry
        lax.fori_loop(0, n_t, body, 0)

    def mask_ties():
        need = (n_sel - count(lambda v: v > thr)).astype(F32)

        def body(t, carry):
            v = sc_ref[rows(t), :]
            eq = v == thr
            eqf = jnp.where(eq, 1.0, 0.0)
            pre = jnp.dot(tri_ref[...], eqf.astype(BF16), preferred_element_type=F32) + carry
            tie = jnp.where(eq, jnp.where(pre < need, 0.0, NEG), NEG)
            keep = jnp.where(v > thr, 0.0, tie)
            am_ref[rows(t), :] = jnp.where(admissible(t), keep, NEG)
            return carry + jnp.sum(jnp.sum(fold8(eqf), axis=0), axis=0, keepdims=True)
        lax.fori_loop(0, n_t, body, jnp.zeros((1, LANES), F32))

    lax.cond(jnp.max(count(lambda v: v >= thr)) == n_sel, mask_simple, mask_ties)

    npair = N_HEADS // 2
    pw = 2 * LANES
    zq = jnp.zeros((HEAD_DIM, LANES), BF16)
    rhs = [jnp.concatenate(
        [jnp.concatenate([qT_ref[pr * LANES:pr * LANES + HEAD_DIM, :], zq], axis=1),
         jnp.concatenate([zq, qT_ref[pr * LANES + HEAD_DIM:(pr + 1) * LANES, :]], axis=1)], axis=0)
        for pr in range(npair)]

    def logit_body(t, m8):
        am = am_ref[rows(t), :]
        am2 = jnp.concatenate([am, am], axis=1)
        tops = []
        for pr in range(npair):
            s = jnp.dot(k_ref[rows(t), pr * LANES:(pr + 1) * LANES], rhs[pr],
                        preferred_element_type=F32) + am2
            s_ref[rows(t), pr * pw:(pr + 1) * pw] = s
            tops.append(jnp.max(fold8(s), axis=0))
        return jnp.maximum(m8, jnp.concatenate(tops, axis=1))

    m8 = lax.fori_loop(0, n_t, logit_body, jnp.full((8, npair * pw), NEG, F32))
    m = jnp.max(m8, axis=0, keepdims=True)
    acc_ref[...] = jnp.zeros_like(acc_ref)

    def pv_body(t, l8):
        sums = []
        for pr in range(npair):
            cols = slice(pr * pw, (pr + 1) * pw)
            pexp = jnp.exp(s_ref[rows(t), cols] - m[:, cols])
            sums.append(jnp.sum(fold8(pexp), axis=0))
            acc_ref[pr] += jnp.dot(vT_ref[t, pr * LANES:(pr + 1) * LANES, :], pexp.astype(BF16),
                                   preferred_element_type=F32)
        return l8 + jnp.concatenate(sums, axis=1)

    l8 = lax.fori_loop(0, n_t, pv_body, jnp.zeros((8, npair * pw), F32))
    inv = 1.0 / jnp.sum(l8, axis=0, keepdims=True)
    for pr in range(npair):
        lo = pr * LANES
        acc = acc_ref[pr]
        oT_ref[lo:lo + HEAD_DIM, :] = acc[:HEAD_DIM, :LANES] * inv[:, pr * pw:pr * pw + LANES]
        oT_ref[lo + HEAD_DIM:lo + LANES, :] = acc[HEAD_DIM:, LANES:] * inv[:, pr * pw + LANES:(pr + 1) * pw]

    o_ref[...] = oT_ref[...].T.astype(BF16)


def _dsa(qT, iqT, iwT, k, vT, ikb, tri, *, q0, n_valid):
    nb, _, tq = qT.shape
    lp = k.shape[1]
    qblk = lambda b, j: (b, 0, j)
    seq3 = lambda b, j: (b, 0, 0)
    seq4 = lambda b, j: (b, 0, 0, 0)
    kern = functools.partial(_dsa_kernel, q0=q0, n_valid=n_valid, n_key_tiles=lp // KEY_TILE,
                             n_sel=min(TOPK_MAX, n_valid // 4))
    return pl.pallas_call(
        kern, out_shape=jax.ShapeDtypeStruct((nb, tq, ATT_W), BF16),
        grid=(nb, tq // LANES),
        in_specs=[
            pl.BlockSpec((None, ATT_W, LANES), qblk),
            pl.BlockSpec((None, 2 * LANES, LANES), qblk),
            pl.BlockSpec((None, IDX_HEADS, LANES), qblk),
            pl.BlockSpec((None, lp, ATT_W), seq3),
            pl.BlockSpec((None, lp // KEY_TILE, ATT_W, KEY_TILE), seq4),
            pl.BlockSpec((None, lp, LANES), seq3),
            pl.BlockSpec((KEY_TILE, KEY_TILE), lambda b, j: (0, 0)),
        ],
        out_specs=pl.BlockSpec((None, LANES, ATT_W), lambda b, j: (b, j, 0)),
        scratch_shapes=[
            pltpu.VMEM((lp + KEY_TILE, LANES), F32),
            pltpu.VMEM((lp, LANES), F32),
            pltpu.VMEM((lp, N_HEADS * LANES), F32),
            pltpu.VMEM((ATT_W, LANES), F32),
            pltpu.VMEM((N_HEADS // 2, LANES, 2 * LANES), F32),
        ],
        compiler_params=_cparams(("parallel", "arbitrary")), name="dsa",
    )(qT, iqT, iwT, k, vT, ikb, tri)


def _band_kernel(*refs, row_lo, row_hi, qpb):
    nkb = BAND_PAST // LANES + qpb
    q_ref = refs[0]
    k_refs = refs[1:1 + nkb]
    v_refs = refs[1 + nkb:1 + 2 * nkb]
    bias_ref, o_ref = refs[1 + 2 * nkb:]
    j = pl.program_id(1)
    kb = jnp.concatenate([r[...] for r in k_refs], axis=0)
    vb = jnp.concatenate([r[...] for r in v_refs], axis=0)
    lane = lax.broadcasted_iota(I32, (1, LANES), 1)
    lo_half = lane < HEAD_DIM
    zero = jnp.zeros((), BF16)
    for i in range(qpb):
        row = (j * qpb + i) * LANES + lax.broadcasted_iota(I32, (1, BAND_KEYS), 1)
        valid = jnp.where(row >= row_lo, row, row_hi) < row_hi
        qrows = slice(i * LANES, (i + 1) * LANES)
        krows = slice(i * LANES, i * LANES + BAND_KEYS)
        for pr in range(N_HEADS // 2):
            sl = slice(pr * LANES, (pr + 1) * LANES)
            qp = q_ref[qrows, sl]
            lhs = jnp.concatenate([jnp.where(lo_half, qp, zero), jnp.where(lo_half, zero, qp)], axis=0)
            s = lax.dot_general(lhs, kb[krows, sl], (((1,), (1,)), ((), ())),
                                preferred_element_type=F32)
            bias2 = jnp.concatenate([bias_ref[2 * pr], bias_ref[2 * pr + 1]], axis=0)
            s = jnp.where(valid, s + bias2, NEG)
            m = jnp.max(s, axis=-1, keepdims=True)
            pexp = jnp.exp(s - m)
            inv = 1.0 / jnp.sum(pexp, axis=-1, keepdims=True)
            o = jnp.dot(pexp.astype(BF16), vb[krows, sl], preferred_element_type=F32) * inv
            o_ref[qrows, sl] = jnp.where(lo_half, o[:LANES], o[LANES:]).astype(BF16)


def _band(q, kp, vp, bias, *, row_lo, row_hi):
    nb, tq, _ = q.shape
    qpb = 2 if tq % (2 * LANES) == 0 else 1
    nkb = BAND_PAST // LANES + qpb
    kspecs = [pl.BlockSpec((None, LANES, ATT_W), (lambda b, j, c=c: (b, j * qpb + c, 0)))
              for c in range(nkb)]
    kern = functools.partial(_band_kernel, row_lo=row_lo, row_hi=row_hi, qpb=qpb)
    qspec = pl.BlockSpec((None, qpb * LANES, ATT_W), lambda b, j: (b, j, 0))
    return pl.pallas_call(
        kern, out_shape=jax.ShapeDtypeStruct((nb, tq, ATT_W), BF16),
        grid=(nb, tq // (qpb * LANES)),
        in_specs=[qspec] + kspecs + kspecs + [pl.BlockSpec(bias.shape, lambda b, j: (0, 0, 0))],
        out_specs=qspec,
        compiler_params=_cparams(("parallel", "parallel")), name="band",
    )(q, *([kp] * nkb), *([vp] * nkb), bias)


def _merge_kernel(x_ref, oa_ref, ob_ref, gattn_ref, wg_ref, wa_ref, wb_ref, wo_ref,
                  gffn_ref, wpqT_ref, c1_ref, c2_ref,
                  x2_ref, h2T_ref, s1_ref, s2_ref):
    x = x_ref[...]
    ms = jnp.mean(x * x, axis=-1, keepdims=True)
    h = (x * lax.rsqrt(ms + EPS) * gattn_ref[...]).astype(BF16)
    ga = jax.nn.sigmoid(jnp.dot(h, wg_ref[:, :D_MODEL], preferred_element_type=F32))
    gb = jax.nn.sigmoid(jnp.dot(h, wg_ref[:, D_MODEL:], preferred_element_type=F32))
    ma = jnp.dot(oa_ref[...], wa_ref[...], preferred_element_type=F32)
    mb = jnp.dot(ob_ref[...], wb_ref[...], preferred_element_type=F32)
    merged = (ga * ma + gb * mb).astype(BF16)
    x2 = x + jnp.dot(merged, wo_ref[...], preferred_element_type=F32)
    x2_ref[...] = x2
    ms2 = jnp.mean(x2 * x2, axis=-1, keepdims=True)
    h2T = (x2 * lax.rsqrt(ms2 + EPS) * gffn_ref[...]).T.astype(BF16)
    h2T_ref[...] = h2T
    half = PEER_DQ // 2
    for hh in range(PEER_HEADS):
        qT = jnp.dot(wpqT_ref[hh * PEER_DQ:(hh + 1) * PEER_DQ, :], h2T,
                     preferred_element_type=F32).astype(BF16)
        s1_ref[hh] = jnp.dot(c1_ref[...], qT[:half], preferred_element_type=F32)
        s2_ref[hh] = jnp.dot(c2_ref[...], qT[half:], preferred_element_type=F32)


def _merge(x, oa, ob, gattn, wg, wa, wb, wo, gffn, wpqT, c1, c2):
    n = x.shape[0]
    tm = ROW_TILE
    row = lambda i: (i, 0)
    const = lambda i: (0, 0)
    return pl.pallas_call(
        _merge_kernel,
        out_shape=(
            jax.ShapeDtypeStruct((n, D_MODEL), F32),
            jax.ShapeDtypeStruct((D_MODEL, n), BF16),
            jax.ShapeDtypeStruct((PEER_HEADS, N_KEYS, n), F32),
            jax.ShapeDtypeStruct((PEER_HEADS, N_KEYS, n), F32),
        ),
        grid=(n // tm,),
        in_specs=[
            pl.BlockSpec((tm, D_MODEL), row),
            pl.BlockSpec((tm, ATT_W), row),
            pl.BlockSpec((tm, ATT_W), row),
            pl.BlockSpec((1, D_MODEL), const),
            pl.BlockSpec(wg.shape, const),
            pl.BlockSpec(wa.shape, const),
            pl.BlockSpec(wb.shape, const),
            pl.BlockSpec(wo.shape, const),
            pl.BlockSpec((1, D_MODEL), const),
            pl.BlockSpec(wpqT.shape, const),
            pl.BlockSpec(c1.shape, const),
            pl.BlockSpec(c2.shape, const),
        ],
        out_specs=(
            pl.BlockSpec((tm, D_MODEL), row),
            pl.BlockSpec((D_MODEL, tm), lambda i: (0, i)),
            pl.BlockSpec((PEER_HEADS, N_KEYS, tm), lambda i: (0, 0, i)),
            pl.BlockSpec((PEER_HEADS, N_KEYS, tm), lambda i: (0, 0, i)),
        ),
        compiler_params=_cparams(("parallel",)), name="merge",
    )(x, oa, ob, gattn, wg, wa, wb, wo, gffn, wpqT, c1, c2)


def _pack_rows(x):
    return pltpu.bitcast(x, jnp.uint32)


def _unpack_rows(x):
    return pltpu.bitcast(x, BF16)


def _top16(x, distinct):
    idx = lax.broadcasted_iota(I32, x.shape, 0)
    rank = jnp.full(x.shape, PEER_TOPK, I32)
    rows = []
    for r in range(PEER_TOPK):
        m = jnp.max(x, axis=0, keepdims=True)
        rows.append(m)
        hit = x == m
        if not distinct:
            hit = idx == jnp.min(jnp.where(hit, idx, N_KEYS), axis=0, keepdims=True)
        rank = jnp.where(hit, r, rank)
        x = jnp.where(hit, -jnp.inf, x)
    return jnp.concatenate(rows, axis=0), rank


def _select_kernel(s1_ref, s2_ref, cnt_ref, e1_ref, r2_ref, e2_ref):
    sub8 = lax.broadcasted_iota(I32, (8, 1), 0).astype(F32)
    sub16 = lax.broadcasted_iota(I32, (PEER_TOPK, 1), 0).astype(F32)

    def count(mask):
        return jnp.sum(jnp.where(mask, 1.0, 0.0), axis=0, keepdims=True)

    def head_body(hh, carry):
        s1 = s1_ref[hh]
        s2 = s2_ref[hh]
        v1, r1 = _top16(s1, True)
        v2, r2 = _top16(s2, True)
        ranked = jnp.maximum(count(r1 < PEER_TOPK), count(r2 < PEER_TOPK))
        v1, r1, v2, r2 = lax.cond(
            jnp.max(ranked) > PEER_TOPK,
            lambda: _top16(s1, False) + _top16(s2, False),
            lambda: (v1, r1, v2, r2))
        pieces = [v1[0:1] + v2]
        for a in range(1, 8):
            piece = v1[a:a + 1] + v2[0:8]
            pieces.append(jnp.where(sub8 < PEER_TOPK // (a + 1), piece, -jnp.inf))
        pieces.append(v1[8:16] + v2[0:1])
        cand = jnp.concatenate(pieces, axis=0)
        rem = jnp.full((1, cand.shape[1]), float(PEER_TOPK), F32)
        tau = jnp.zeros((1, cand.shape[1]), F32)
        x = cand
        for _ in range(PEER_TOPK):
            m = jnp.max(x, axis=0, keepdims=True)
            hit = x == m
            tau = jnp.where(rem > 0, m, tau)
            rem = rem - count(hit)
            x = jnp.where(hit, -jnp.inf, x)
        need = PEER_TOPK - count(cand > tau)
        kept = []
        for a in range(PEER_TOPK):
            row = pieces[a] if a < 8 else pieces[8][a - 8:a - 7]
            ties = count(row == tau)
            kept.append(count(row > tau) + jnp.minimum(ties, jnp.maximum(need, 0.0)))
            need = need - ties
        smax = v1[0:1] + v2[0:1]
        z = jnp.sum(jnp.where(sub16 < kept[0], jnp.exp(pieces[0] - smax), 0.0), axis=0, keepdims=True)
        for a in range(1, 8):
            z = z + jnp.sum(jnp.where(sub8 < kept[a], jnp.exp(pieces[a] - smax), 0.0),
                            axis=0, keepdims=True)
        tail = jnp.concatenate(kept[8:], axis=0)
        z = z + jnp.sum(jnp.where(tail > 0, jnp.exp(pieces[8] - smax), 0.0), axis=0, keepdims=True)
        cnt = jnp.zeros(s1.shape, F32)
        for a in range(PEER_TOPK):
            cnt = jnp.where(r1 == a, kept[a], cnt)
        cnt_ref[hh] = cnt
        e1_ref[hh] = jnp.exp(s1 - v1[0:1]) * (1.0 / z)
        r2b = r2.astype(F32).astype(BF16)
        e2b = jnp.exp(s2 - v2[0:1]).astype(BF16)
        for c in range(s2.shape[1] // LANES):
            cs = slice(c * LANES, (c + 1) * LANES)
            r2_ref[hh, c] = _pack_rows(r2b[:, cs])
            e2_ref[hh, c] = _pack_rows(e2b[:, cs])
        return carry

    lax.fori_loop(0, PEER_HEADS, head_body, 0)


def _select(s1, s2):
    n = s1.shape[-1]
    tn = SEL_TILE
    blk = pl.BlockSpec((PEER_HEADS, N_KEYS, tn), lambda i: (0, 0, i))
    packed = jax.ShapeDtypeStruct((PEER_HEADS, n // LANES, N_KEYS // 2, LANES), jnp.uint32)
    pblk = pl.BlockSpec((PEER_HEADS, tn // LANES, N_KEYS // 2, LANES), lambda i: (0, i, 0, 0))
    return pl.pallas_call(
        _select_kernel,
        out_shape=(
            jax.ShapeDtypeStruct(s1.shape, F32),
            jax.ShapeDtypeStruct(s1.shape, F32),
            packed, packed,
        ),
        grid=(n // tn,),
        in_specs=[blk, blk],
        out_specs=(blk, blk, pblk, pblk),
        compiler_params=_cparams(("parallel",)), name="select",
    )(s1, s2)


def _peer_kernel(h2T_ref, u_ref, vT_ref, vTp_ref, cnt_ref, e1_ref, r2_ref, e2_ref, x2_ref,
                 y_ref, acc_ref, wa0_ref, wa1_ref, z0_ref, z1_ref, rowc_ref, rowe_ref):
    e = pl.program_id(1)
    zero = jnp.minimum(e, 0)
    z_refs = (z0_ref, z1_ref)
    wa_refs = (wa0_ref, wa1_ref)

    @pl.when(e == 0)
    def _():
        acc_ref[...] = jnp.zeros_like(acc_ref)
        wa_refs[(PEER_EB // PEER_GRP - 1) % 2][...] = jnp.zeros_like(wa1_ref)

    tn = h2T_ref.shape[1]
    nsub = PEER_EB // N_KEYS
    reps = N_KEYS // BF16_ROWS
    for sub in range(nsub):
        for hh in range(PEER_HEADS):
            rowc_ref[sub, hh] = _pack_rows(
                jnp.broadcast_to(cnt_ref[hh, sub:sub + 1, :], (BF16_ROWS, tn)).astype(BF16))
            rowe_ref[sub, hh] = _pack_rows(
                jnp.broadcast_to(e1_ref[hh, sub:sub + 1, :], (BF16_ROWS, tn)).astype(BF16))

    ngrp = PEER_EB // PEER_GRP

    def scores(g, slot):
        start = g * PEER_GRP if isinstance(g, int) else pl.multiple_of(g * PEER_GRP, PEER_GRP)
        z_refs[slot][0] = jnp.dot(u_ref[pl.ds(start, PEER_GRP), :], h2T_ref[...],
                                  preferred_element_type=F32)

    def weights(g, slot):
        for s in range(PEER_GRP // N_KEYS):
            sub = g * (PEER_GRP // N_KEYS) + s
            for c in range(tn // LANES):
                cs = slice(c * LANES, (c + 1) * LANES)
                zc = z_refs[slot][zero, s * N_KEYS:(s + 1) * N_KEYS, cs]
                a = (0.5 * zc * (1.0 + lax.erf(zc * (2.0 ** -0.5)))).astype(BF16)
                w = jnp.zeros((N_KEYS, LANES), BF16)
                for hh in range(PEER_HEADS):
                    cnt = jnp.concatenate([_unpack_rows(rowc_ref[sub + zero, hh, :, cs])] * reps, axis=0)
                    e1 = jnp.concatenate([_unpack_rows(rowe_ref[sub + zero, hh, :, cs])] * reps, axis=0)
                    w = w + jnp.where(_unpack_rows(r2_ref[hh, c]) < cnt,
                                      _unpack_rows(e2_ref[hh, c]) * e1, jnp.zeros((), BF16))
                wa_refs[slot][0, s * (N_KEYS // 2):(s + 1) * (N_KEYS // 2), cs] = _pack_rows(w * a)

    def down(vt, slot):
        acc_ref[...] += jnp.dot(vt, _unpack_rows(wa_refs[slot][zero]), preferred_element_type=F32)

    last = (ngrp - 1) % 2
    scores(0, 0)
    down(vTp_ref[0], last)
    for g in range(ngrp):
        if g + 1 < ngrp:
            scores(g + 1, (g + 1) % 2)
        weights(g, g % 2)
        if g > 0:
            down(vT_ref[g - 1], (g - 1) % 2)

    @pl.when(e == pl.num_programs(1) - 1)
    def _():
        down(vT_ref[ngrp - 1], last)
        y_ref[...] = x2_ref[...] + acc_ref[...].T


def _peer(h2T, u, vT, cnt, e1, r2, e2, x2):
    n = x2.shape[0]
    tn = min(PEER_TOK, n)
    nsub = PEER_EB // N_KEYS
    tok3 = pl.BlockSpec((PEER_HEADS, tn // LANES, N_KEYS // 2, LANES), lambda i, e: (0, i, 0, 0))
    key1 = pl.BlockSpec((PEER_HEADS, nsub, tn), lambda i, e: (0, e, i))
    return pl.pallas_call(
        _peer_kernel,
        out_shape=jax.ShapeDtypeStruct((n, D_MODEL), F32),
        grid=(n // tn, N_EXPERTS // PEER_EB),
        in_specs=[
            pl.BlockSpec((D_MODEL, tn), lambda i, e: (0, i)),
            pl.BlockSpec((PEER_EB, D_MODEL), lambda i, e: (e, 0)),
            pl.BlockSpec((PEER_EB // PEER_GRP, D_MODEL, PEER_GRP), lambda i, e: (e, 0, 0)),
            pl.BlockSpec((1, D_MODEL, PEER_GRP),
                         lambda i, e: (jnp.maximum(e * (PEER_EB // PEER_GRP) - 1, 0), 0, 0)),
            key1, key1, tok3, tok3,
            pl.BlockSpec((tn, D_MODEL), lambda i, e: (i, 0)),
        ],
        out_specs=pl.BlockSpec((tn, D_MODEL), lambda i, e: (i, 0)),
        scratch_shapes=[
            pltpu.VMEM((D_MODEL, tn), F32),
            pltpu.VMEM((1, PEER_GRP // 2, tn), jnp.uint32),
            pltpu.VMEM((1, PEER_GRP // 2, tn), jnp.uint32),
            pltpu.VMEM((1, PEER_GRP, tn), F32),
            pltpu.VMEM((1, PEER_GRP, tn), F32),
            pltpu.VMEM((nsub, PEER_HEADS, BF16_ROWS // 2, tn), jnp.uint32),
            pltpu.VMEM((nsub, PEER_HEADS, BF16_ROWS // 2, tn), jnp.uint32),
        ],
        compiler_params=_cparams(("parallel", "arbitrary")), name="peer",
    )(h2T, u, vT, vT, cnt, e1, r2, e2, x2)


def _rope_tables(pos):
    posf = pos.astype(F32)[:, None]
    t = pos.shape[0]

    def tab(half, reps):
        inv = jnp.power(ROPE_THETA, -jnp.arange(half, dtype=F32) / half)
        ang = posf * inv[None, :]
        cos = jnp.cos(ang)
        sin = jnp.sin(ang)
        return (jnp.tile(jnp.concatenate([cos, cos], axis=1), (1, reps)),
                jnp.tile(jnp.concatenate([-sin, sin], axis=1), (1, reps)))

    cosq, sinq = tab(HEAD_DIM // 2, N_HEADS)
    cosi, sini = tab(IDX_DIM // 2, IDX_HEADS + 1)
    pad = IDX_W - cosi.shape[1]
    cosi = jnp.concatenate([cosi, jnp.ones((t, pad), F32)], axis=1)
    sini = jnp.concatenate([sini, jnp.zeros((t, pad), F32)], axis=1)
    return cosq, sinq, cosi, sini


def _band_bias(rel_bias):
    qi = np.arange(LANES)[:, None]
    kj = np.arange(BAND_KEYS)[None, :]
    ok = (kj // CHUNK >= qi // CHUNK) & (kj // CHUNK <= qi // CHUNK + BAND_PAST // CHUNK)
    n = BAND_KEYS + LANES - 1
    m = np.arange(n)
    diag = rel_bias.astype(F32)[:, np.clip(BAND_PAST + LANES - 1 - m, -REL_CLIP, REL_CLIP) + REL_CLIP]
    wrapped = jnp.tile(jnp.pad(diag, ((0, 0), (0, 1))), (1, LANES))[:, :LANES * n]
    toep = wrapped.reshape(-1, LANES, n)[:, :, LANES - 1:LANES - 1 + BAND_KEYS]
    return jnp.where(jnp.asarray(ok)[None], toep, NEG)


def _layer_weights(g_attn, w_in, g_qa, g_ka, g_ik, g_qb, g_kb, rel_bias, w_br_a, w_br_b,
                   w_out, g_ffn, w_pq, c1, c2, u, v):
    a0 = 3 * ATT_W
    i0 = a0 + IDX_HEADS * IDX_DIM + IDX_DIM + IDX_HEADS
    b0 = i0 + 3 * ATT_W
    idx_cols = i0 - a0
    w1 = jnp.concatenate(
        [w_in[:, :a0], w_in[:, i0:b0], w_in[:, a0:i0],
         jnp.zeros((D_MODEL, IDX_W - idx_cols), w_in.dtype)], axis=1).astype(BF16)
    wg = w_in[:, b0:].astype(BF16)
    hd = jnp.asarray(np.kron(np.eye(N_HEADS), np.ones((HEAD_DIM, HEAD_DIM))) / HEAD_DIM, BF16)
    gh = jnp.stack([jnp.tile(g, N_HEADS) for g in (g_qa, g_ka, g_qb, g_kb)]).astype(F32)
    gidx = jnp.concatenate(
        [g_ik.astype(F32), jnp.full((IDX_HEADS,), IDX_HEADS ** -0.5, F32),
         jnp.ones((LANES - IDX_DIM - IDX_HEADS,), F32)])[None, :]
    tri = jnp.asarray(np.tril(np.ones((KEY_TILE, KEY_TILE)), -1), BF16)
    return dict(
        w1=w1, wg=wg, hd=hd, gh=gh, gidx=gidx, tri=tri,
        gattn=g_attn.astype(F32)[None, :], gffn=g_ffn.astype(F32)[None, :],
        bias=_band_bias(rel_bias),
        wa=w_br_a.astype(BF16), wb=w_br_b.astype(BF16), wo=w_out.astype(BF16),
        wpqT=w_pq.T.astype(BF16), c1=c1.astype(BF16), c2=c2.astype(BF16),
        u=u.astype(BF16),
        vT=v.astype(BF16).reshape(N_EXPERTS // PEER_GRP, PEER_GRP, D_MODEL).transpose(0, 2, 1))


def _ffn(x, oa, ob, w):
    x2, h2T, s1, s2 = _merge(x, oa, ob, w["gattn"], w["wg"], w["wa"], w["wb"], w["wo"],
                             w["gffn"], w["wpqT"], w["c1"], w["c2"])
    cnt, e1, r2, e2 = _select(s1, s2)
    return _peer(h2T, w["u"], w["vT"], cnt, e1, r2, e2, x2)


def _prompt_layer(x, w):
    b, t, _ = x.shape
    xf = x.reshape(b * t, D_MODEL)
    tabs = _rope_tables(jnp.arange(t))
    (aqT, ak, akb, av, avT, iqT, ik, ikb, iwT, bq, bk, bkb, bv, bvb) = _proj(
        xf, b, t, w["w1"], w["gattn"], w["hd"], w["gh"], w["gidx"], *tabs)
    oa = _dsa(aqT, iqT, iwT, akb.reshape(b, t, ATT_W), avT, ikb.reshape(b, t, LANES),
              w["tri"], q0=0, n_valid=t)
    front = ((0, 0), (BAND_PAST, 0), (0, 0))
    ob = _band(bq.reshape(b, t, ATT_W), jnp.pad(bkb.reshape(b, t, ATT_W), front),
               jnp.pad(bvb.reshape(b, t, ATT_W), front), w["bias"],
               row_lo=BAND_PAST, row_hi=BAND_PAST + t)
    y = _ffn(xf, oa.reshape(b * t, ATT_W), ob.reshape(b * t, ATT_W), w)
    keep = min(BAND_PAST, t)
    heads = lambda a: a.reshape(b, t, N_HEADS, HEAD_DIM)
    return y.reshape(b, t, D_MODEL), (
        heads(ak), heads(av), ik.reshape(b, t, IDX_DIM),
        heads(bk)[:, t - keep:], heads(bv)[:, t - keep:])


def _sample_layer(x, ca_k, ca_v, ca_ik, cb_k, cb_v, w):
    b, t, _ = x.shape
    n = b * t
    past = ca_k.shape[1]
    xf = x.reshape(n, D_MODEL)
    tabs = _rope_tables(jnp.tile(past + jnp.arange(t), b))
    (aqT, ak, akb, av, avT, iqT, ik, ikb, iwT, bq, bk, bkb, bv, bvb) = _proj(
        xf, 1, n, w["w1"], w["gattn"], w["hd"], w["gh"], w["gidx"], *tabs)

    def per_seq_T(aT, rows):
        a = aT[0].T.reshape(b, t, rows)
        return jnp.pad(a, ((0, 0), (0, LANES - t), (0, 0))).transpose(0, 2, 1)

    lk = past + t
    lp = -(-lk // KEY_TILE) * KEY_TILE
    padk = ((0, 0), (0, lp - lk), (0, 0))
    k_all = jnp.pad(jnp.concatenate(
        [ca_k.reshape(b, past, ATT_W).astype(BF16), akb.reshape(b, t, ATT_W)], axis=1), padk)
    v_all = jnp.pad(jnp.concatenate(
        [ca_v.reshape(b, past, ATT_W).astype(BF16), av.astype(BF16).reshape(b, t, ATT_W)],
        axis=1), padk)
    vT_all = v_all.reshape(b, lp // KEY_TILE, KEY_TILE, ATT_W).transpose(0, 1, 3, 2)
    ik_all = jnp.pad(jnp.concatenate(
        [jnp.pad(ca_ik.astype(BF16), ((0, 0), (0, 0), (0, LANES - IDX_DIM))),
         ikb.reshape(b, t, LANES)], axis=1), padk)
    oa = _dsa(per_seq_T(aqT, ATT_W), per_seq_T(iqT, 2 * LANES), per_seq_T(iwT, IDX_HEADS),
              k_all, vT_all, ik_all, w["tri"], q0=past, n_valid=lk)[:, :t]

    pb = cb_k.shape[1]
    padb = ((0, 0), (BAND_PAST - pb, LANES - t), (0, 0))
    qpad = jnp.pad(bq.reshape(b, t, ATT_W), ((0, 0), (0, LANES - t), (0, 0)))
    kb_all = jnp.pad(jnp.concatenate(
        [cb_k.reshape(b, pb, ATT_W).astype(BF16), bkb.reshape(b, t, ATT_W)], axis=1), padb)
    vb_all = jnp.pad(jnp.concatenate(
        [cb_v.reshape(b, pb, ATT_W).astype(BF16), bvb.reshape(b, t, ATT_W)], axis=1), padb)
    ob = _band(qpad, kb_all, vb_all, w["bias"],
               row_lo=BAND_PAST - pb, row_hi=BAND_PAST + t)[:, :t]

    y = _ffn(xf, oa.reshape(n, ATT_W), ob.reshape(n, ATT_W), w)
    keep = min(BAND_PAST, t)
    heads = lambda a: a.reshape(b, t, N_HEADS, HEAD_DIM)
    return y.reshape(b, t, D_MODEL), (
        heads(ak), heads(av), ik.reshape(b, t, IDX_DIM),
        heads(bk)[:, t - keep:], heads(bv)[:, t - keep:])


def kernel(x_prompt, x_sample, cache_a_k, cache_a_v, cache_a_ik, cache_b_k, cache_b_v,
           g_attn, w_in, g_qa, g_ka, g_ik, g_qb, g_kb, rel_bias, w_br_a, w_br_b, w_out,
           g_ffn, w_pq, peer_c1, peer_c2, peer_u, peer_v):
    depth = w_in.shape[0]
    xp, xs = x_prompt, x_sample
    sp, ss = [], []
    for l in range(depth):
        w = _layer_weights(g_attn[l], w_in[l], g_qa[l], g_ka[l], g_ik[l], g_qb[l], g_kb[l],
                           rel_bias[l], w_br_a[l], w_br_b[l], w_out[l], g_ffn[l], w_pq[l],
                           peer_c1[l], peer_c2[l], peer_u[l], peer_v[l])
        xp, st_p = _prompt_layer(xp, w)
        xs, st_s = _sample_layer(xs, cache_a_k[l], cache_a_v[l], cache_a_ik[l],
                                 cache_b_k[l], cache_b_v[l], w)
        sp.append(st_p)
        ss.append(st_s)
    stack = lambda sts, i: jnp.stack([s[i] for s in sts])
    return (xp, xs) + tuple(stack(sp, i) for i in range(5)) + tuple(stack(ss, i) for i in range(5))
```

```python
import functools

import jax
import jax.numpy as jnp
import numpy as np
from jax import lax
from jax.experimental import pallas as pl
from jax.experimental.pallas import tpu as pltpu

F32 = jnp.float32
BF16 = jnp.bfloat16
I32 = jnp.int32

D_MODEL = 1024
HEAD_DIM = 64
N_HEADS = 8
ATT_W = N_HEADS * HEAD_DIM
IDX_HEADS = 8
IDX_DIM = 32
IDX_W = 384
CHUNK = 64
TOPK_MAX = 256
BAND_PAST = 512
BAND_KEYS = BAND_PAST + 128
REL_CLIP = 128
PEER_HEADS = 8
PEER_DQ = 256
N_KEYS = 128
N_EXPERTS = N_KEYS * N_KEYS
PEER_TOPK = 16
ROPE_THETA = 10000.0
EPS = 1e-6
NEG = -1e30
INT_MIN = -(2 ** 31)

LANES = 128
BF16_ROWS = 16
KEY_TILE = 256
ROW_TILE = 512
SEL_TILE = 256
PEER_TOK = 1024
PEER_EB = 1024
PEER_GRP = 256
VMEM_LIMIT = 56 * 1024 * 1024


def _cparams(sem):
    return pltpu.CompilerParams(dimension_semantics=sem, vmem_limit_bytes=VMEM_LIMIT)


def _rope_chunks(y, cos, sin, half):
    lane = lax.broadcasted_iota(I32, (1, LANES), 1)
    lo = (lane % (2 * half)) < half
    out = []
    for c in range(y.shape[1] // LANES):
        sl = slice(c * LANES, (c + 1) * LANES)
        yc = y[:, sl]
        partner = jnp.where(lo, pltpu.roll(yc, LANES - half, 1), pltpu.roll(yc, half, 1))
        out.append(yc * cos[:, sl] + partner * sin[:, sl])
    return jnp.concatenate(out, axis=1)


def _proj_kernel(x_ref, gattn_ref, w1_ref, hd_ref, gh_ref, gidx_ref,
                 cosq_ref, sinq_ref, cosi_ref, sini_ref,
                 aqT_ref, ak_ref, akb_ref, av_ref, avT_ref, iqT_ref, ik_ref, ikb_ref,
                 iwT_ref, bq_ref, bk_ref, bkb_ref, bv_ref, bvb_ref):
    x = x_ref[...]
    ms = jnp.mean(x * x, axis=-1, keepdims=True)
    h = (x * lax.rsqrt(ms + EPS) * gattn_ref[...]).astype(BF16)

    def seg(i):
        return jnp.dot(h, w1_ref[:, i * ATT_W:(i + 1) * ATT_W], preferred_element_type=F32)

    def headnorm(y, gi):
        hms = jnp.dot((y * y).astype(BF16), hd_ref[...], preferred_element_type=F32)
        return y * lax.rsqrt(hms + EPS) * gh_ref[gi:gi + 1, :]

    cosq = cosq_ref[...]
    sinq = sinq_ref[...]

    aq = _rope_chunks(headnorm(seg(0), 0), cosq, sinq, HEAD_DIM // 2) * (HEAD_DIM ** -0.5)
    aqT_ref[...] = aq.T.astype(BF16)

    ak = _rope_chunks(headnorm(seg(1), 1), cosq, sinq, HEAD_DIM // 2)
    ak_ref[...] = ak
    akb_ref[...] = ak.astype(BF16)

    av = seg(2)
    av_ref[...] = av
    for c in range(av.shape[0] // KEY_TILE):
        avT_ref[c] = av[c * KEY_TILE:(c + 1) * KEY_TILE, :].T.astype(BF16)

    bq_ref[...] = (headnorm(seg(3), 2) * (HEAD_DIM ** -0.5)).astype(BF16)
    bk = headnorm(seg(4), 3)
    bk_ref[...] = bk
    bkb_ref[...] = bk.astype(BF16)
    bv = seg(5)
    bv_ref[...] = bv
    bvb_ref[...] = bv.astype(BF16)

    s = jnp.dot(h, w1_ref[:, 6 * ATT_W:6 * ATT_W + IDX_W], preferred_element_type=F32)
    lane = lax.broadcasted_iota(I32, (1, LANES), 1)
    is_ik = lane < IDX_DIM
    c2 = s[:, 2 * LANES:3 * LANES]
    ikms = jnp.sum(jnp.where(is_ik, c2 * c2, 0.0), axis=-1, keepdims=True) * (1.0 / IDX_DIM)
    c2n = c2 * jnp.where(is_ik, lax.rsqrt(ikms + EPS), 1.0) * gidx_ref[...]
    sn = jnp.concatenate([s[:, :2 * LANES], c2n], axis=1)
    r = _rope_chunks(sn, cosi_ref[...], sini_ref[...], IDX_DIM // 2)
    iqT_ref[...] = r[:, :2 * LANES].T.astype(BF16)
    c2r = r[:, 2 * LANES:]
    ik_ref[...] = c2r[:, :IDX_DIM]
    ikb_ref[...] = jnp.where(is_ik, c2r, 0.0).astype(BF16)
    iwT_ref[...] = c2r.T[IDX_DIM:IDX_DIM + IDX_HEADS, :]


def _proj(x, nb, tb, w1, gattn, hd, gh, gidx, cosq, sinq, cosi, sini):
    n = x.shape[0]
    tm = ROW_TILE
    tpb = tb // tm
    row = lambda i: (i, 0)
    tab = lambda i: (i % tpb, 0)
    const = lambda i: (0, 0)
    grp = lambda i: (i // tpb, 0, i % tpb)
    out_shape = (
        jax.ShapeDtypeStruct((nb, ATT_W, tb), BF16),
        jax.ShapeDtypeStruct((n, ATT_W), F32),
        jax.ShapeDtypeStruct((n, ATT_W), BF16),
        jax.ShapeDtypeStruct((n, ATT_W), F32),
        jax.ShapeDtypeStruct((nb, tb // KEY_TILE, ATT_W, KEY_TILE), BF16),
        jax.ShapeDtypeStruct((nb, 2 * LANES, tb), BF16),
        jax.ShapeDtypeStruct((n, IDX_DIM), F32),
        jax.ShapeDtypeStruct((n, LANES), BF16),
        jax.ShapeDtypeStruct((nb, IDX_HEADS, tb), F32),
        jax.ShapeDtypeStruct((n, ATT_W), BF16),
        jax.ShapeDtypeStruct((n, ATT_W), F32),
        jax.ShapeDtypeStruct((n, ATT_W), BF16),
        jax.ShapeDtypeStruct((n, ATT_W), F32),
        jax.ShapeDtypeStruct((n, ATT_W), BF16),
    )
    out_specs = (
        pl.BlockSpec((None, ATT_W, tm), grp),
        pl.BlockSpec((tm, ATT_W), row),
        pl.BlockSpec((tm, ATT_W), row),
        pl.BlockSpec((tm, ATT_W), row),
        pl.BlockSpec((None, tm // KEY_TILE, ATT_W, KEY_TILE),
                     lambda i: (i // tpb, i % tpb, 0, 0)),
        pl.BlockSpec((None, 2 * LANES, tm), grp),
        pl.BlockSpec((tm, IDX_DIM), row),
        pl.BlockSpec((tm, LANES), row),
        pl.BlockSpec((None, IDX_HEADS, tm), grp),
        pl.BlockSpec((tm, ATT_W), row),
        pl.BlockSpec((tm, ATT_W), row),
        pl.BlockSpec((tm, ATT_W), row),
        pl.BlockSpec((tm, ATT_W), row),
        pl.BlockSpec((tm, ATT_W), row),
    )
    in_specs = [
        pl.BlockSpec((tm, D_MODEL), row),
        pl.BlockSpec((1, D_MODEL), const),
        pl.BlockSpec(w1.shape, const),
        pl.BlockSpec(hd.shape, const),
        pl.BlockSpec(gh.shape, const),
        pl.BlockSpec(gidx.shape, const),
        pl.BlockSpec((tm, ATT_W), tab),
        pl.BlockSpec((tm, ATT_W), tab),
        pl.BlockSpec((tm, IDX_W), tab),
        pl.BlockSpec((tm, IDX_W), tab),
    ]
    return pl.pallas_call(
        _proj_kernel, out_shape=out_shape, grid=(n // tm,),
        in_specs=in_specs, out_specs=out_specs,
        compiler_params=_cparams(("parallel",)), name="proj",
    )(x, gattn, w1, hd, gh, gidx, cosq, sinq, cosi, sini)


def _dsa_kernel(qT_ref, iqT_ref, iwT_ref, k_ref, vT_ref, ikb_ref, tri_ref,
                o_ref, sc_ref, am_ref, s_ref, oT_ref, acc_ref, *, q0, n_valid, n_key_tiles, n_sel):
    j = pl.program_id(1)
    qbase = q0 + j * LANES
    n_t = jnp.minimum(lax.shift_right_logical(qbase + LANES + KEY_TILE - 1, 8), n_key_tiles)
    qchunk = lax.shift_right_logical(
        qbase + lax.broadcasted_iota(I32, (1, LANES), 1), 6)

    def rows(t):
        return pl.ds(pl.multiple_of(t * KEY_TILE, KEY_TILE), KEY_TILE)

    def admissible(t):
        kidx = t * KEY_TILE + lax.broadcasted_iota(I32, (KEY_TILE, LANES), 0)
        return jnp.where(kidx < n_valid, lax.shift_right_logical(kidx, 6), 1 << 30) <= qchunk

    def fold8(v):
        return v.reshape(KEY_TILE // 8, 8, v.shape[-1])

    zpad = jnp.zeros((LANES - IDX_DIM, LANES), BF16)
    rhs_idx = jnp.concatenate(
        [jnp.concatenate([iqT_ref[h * IDX_DIM:(h + 1) * IDX_DIM, :], zpad], axis=0)
         for h in range(IDX_HEADS)], axis=1)

    def score_body(t, carry):
        res = jnp.dot(ikb_ref[rows(t), :], rhs_idx, preferred_element_type=F32)
        acc = jnp.zeros((KEY_TILE, LANES), F32)
        for h in range(IDX_HEADS):
            acc = acc + jnp.maximum(res[:, h * LANES:(h + 1) * LANES], 0.0) * iwT_ref[h:h + 1, :]
        sc_ref[rows(t), :] = jnp.where(admissible(t), acc, NEG)
        return carry

    lax.fori_loop(0, n_t, score_body, 0)
    sc_ref[rows(n_t), :] = jnp.full((KEY_TILE, LANES), -jnp.inf, F32)

    def count(pred):
        def body(t2, c8):
            two = pl.ds(pl.multiple_of(t2 * (2 * KEY_TILE), 2 * KEY_TILE), 2 * KEY_TILE)
            hits = jnp.where(pred(sc_ref[two, :]), 1, 0)
            return c8 + jnp.sum(hits.reshape(2 * KEY_TILE // 8, 8, LANES), axis=0)
        c8 = lax.fori_loop(0, lax.shift_right_logical(n_t + 1, 1), body, jnp.zeros((8, LANES), I32))
        return jnp.sum(c8, axis=0, keepdims=True)

    def bisect_body(it, p):
        c = p + lax.shift_left(jnp.int32(1), 31 - it)
        cf = pltpu.bitcast(jnp.where(c >= 0, c, c ^ 0x7FFFFFFF), F32)
        return jnp.where(count(lambda v: v >= cf) >= n_sel, c, p)

    p = lax.fori_loop(0, 32, bisect_body, jnp.full((1, LANES), INT_MIN, I32))
    thr = pltpu.bitcast(jnp.where(p >= 0, p, p ^ 0x7FFFFFFF), F32)

    def mask_simple():
        def body(t, carry):
            keep = jnp.where(sc_ref[rows(t), :] >= thr, 0.0, NEG)
            am_ref[rows(t), :] = jnp.where(admissible(t), keep, NEG)
            return carry
        lax.fori_loop(0, n_t, body, 0)

    def mask_ties():
        need = (n_sel - count(lambda v: v > thr)).astype(F32)

        def body(t, carry):
            v = sc_ref[rows(t), :]
            eq = v == thr
            eqf = jnp.where(eq, 1.0, 0.0)
            pre = jnp.dot(tri_ref[...], eqf.astype(BF16), preferred_element_type=F32) + carry
            tie = jnp.where(eq, jnp.where(pre < need, 0.0, NEG), NEG)
            keep = jnp.where(v > thr, 0.0, tie)
            am_ref[rows(t), :] = jnp.where(admissible(t), keep, NEG)
            return carry + jnp.sum(jnp.sum(fold8(eqf), axis=0), axis=0, keepdims=True)
        lax.fori_loop(0, n_t, body, jnp.zeros((1, LANES), F32))

    lax.cond(jnp.max(count(lambda v: v >= thr)) == n_sel, mask_simple, mask_ties)

    npair = N_HEADS // 2
    pw = 2 * LANES
    zq = jnp.zeros((HEAD_DIM, LANES), BF16)
    rhs = [jnp.concatenate(
        [jnp.concatenate([qT_ref[pr * LANES:pr * LANES + HEAD_DIM, :], zq], axis=1),
         jnp.concatenate([zq, qT_ref[pr * LANES + HEAD_DIM:(pr + 1) * LANES, :]], axis=1)], axis=0)
        for pr in range(npair)]

    n_two = lax.shift_right_logical(n_t, 1)

    def span(t, width):
        return pl.ds(pl.multiple_of(t * KEY_TILE, KEY_TILE), width * KEY_TILE)

    def logit_body(t, m8, width):
        am = am_ref[span(t, width), :]
        am2 = jnp.concatenate([am, am], axis=1)
        tops = []
        for pr in range(npair):
            s = jnp.dot(k_ref[span(t, width), pr * LANES:(pr + 1) * LANES], rhs[pr],
                        preferred_element_type=F32) + am2
            s_ref[span(t, width), pr * pw:(pr + 1) * pw] = s
            tops.append(jnp.max(s.reshape(width * KEY_TILE // 8, 8, pw), axis=0))
        return jnp.maximum(m8, jnp.concatenate(tops, axis=1))

    m8 = lax.fori_loop(0, n_two, lambda i, c: logit_body(2 * i, c, 2),
                       jnp.full((8, npair * pw), NEG, F32))
    m8 = lax.fori_loop(2 * n_two, n_t, lambda t, c: logit_body(t, c, 1), m8)
    m = jnp.max(m8, axis=0, keepdims=True)
    acc_ref[...] = jnp.zeros_like(acc_ref)

    def pv_body(t, l8, width):
        sums = []
        for pr in range(npair):
            cols = slice(pr * pw, (pr + 1) * pw)
            pexp = jnp.exp(s_ref[span(t, width), cols] - m[:, cols])
            sums.append(jnp.sum(pexp.reshape(width * KEY_TILE // 8, 8, pw), axis=0))
            pb = pexp.astype(BF16)
            out = acc_ref[pr]
            for w in range(width):
                out = out + jnp.dot(vT_ref[t + w, pr * LANES:(pr + 1) * LANES, :],
                                    pb[w * KEY_TILE:(w + 1) * KEY_TILE],
                                    preferred_element_type=F32)
            acc_ref[pr] = out
        return l8 + jnp.concatenate(sums, axis=1)

    l8 = lax.fori_loop(0, n_two, lambda i, c: pv_body(2 * i, c, 2),
                       jnp.zeros((8, npair * pw), F32))
    l8 = lax.fori_loop(2 * n_two, n_t, lambda t, c: pv_body(t, c, 1), l8)
    inv = 1.0 / jnp.sum(l8, axis=0, keepdims=True)
    for pr in range(npair):
        lo = pr * LANES
        acc = acc_ref[pr]
        oT_ref[lo:lo + HEAD_DIM, :] = acc[:HEAD_DIM, :LANES] * inv[:, pr * pw:pr * pw + LANES]
        oT_ref[lo + HEAD_DIM:lo + LANES, :] = acc[HEAD_DIM:, LANES:] * inv[:, pr * pw + LANES:(pr + 1) * pw]

    o_ref[...] = oT_ref[...].T.astype(BF16)


def _dsa(qT, iqT, iwT, k, vT, ikb, tri, *, q0, n_valid):
    nb, _, tq = qT.shape
    lp = k.shape[1]
    qblk = lambda b, j: (b, 0, j)
    seq3 = lambda b, j: (b, 0, 0)
    seq4 = lambda b, j: (b, 0, 0, 0)
    kern = functools.partial(_dsa_kernel, q0=q0, n_valid=n_valid, n_key_tiles=lp // KEY_TILE,
                             n_sel=min(TOPK_MAX, n_valid // 4))
    return pl.pallas_call(
        kern, out_shape=jax.ShapeDtypeStruct((nb, tq, ATT_W), BF16),
        grid=(nb, tq // LANES),
        in_specs=[
            pl.BlockSpec((None, ATT_W, LANES), qblk),
            pl.BlockSpec((None, 2 * LANES, LANES), qblk),
            pl.BlockSpec((None, IDX_HEADS, LANES), qblk),
            pl.BlockSpec((None, lp, ATT_W), seq3),
            pl.BlockSpec((None, lp // KEY_TILE, ATT_W, KEY_TILE), seq4),
            pl.BlockSpec((None, lp, LANES), seq3),
            pl.BlockSpec((KEY_TILE, KEY_TILE), lambda b, j: (0, 0)),
        ],
        out_specs=pl.BlockSpec((None, LANES, ATT_W), lambda b, j: (b, j, 0)),
        scratch_shapes=[
            pltpu.VMEM((lp + KEY_TILE, LANES), F32),
            pltpu.VMEM((lp, LANES), F32),
            pltpu.VMEM((lp, N_HEADS * LANES), F32),
            pltpu.VMEM((ATT_W, LANES), F32),
            pltpu.VMEM((N_HEADS // 2, LANES, 2 * LANES), F32),
        ],
        compiler_params=_cparams(("parallel", "arbitrary")), name="dsa",
    )(qT, iqT, iwT, k, vT, ikb, tri)


def _band_kernel(*refs, row_lo, row_hi, qpb):
    nkb = BAND_PAST // LANES + qpb
    q_ref = refs[0]
    k_refs = refs[1:1 + nkb]
    v_refs = refs[1 + nkb:1 + 2 * nkb]
    bias_ref, o_ref = refs[1 + 2 * nkb:]
    j = pl.program_id(1)
    kb = jnp.concatenate([r[...] for r in k_refs], axis=0)
    vb = jnp.concatenate([r[...] for r in v_refs], axis=0)
    lane = lax.broadcasted_iota(I32, (1, LANES), 1)
    lo_half = lane < HEAD_DIM
    zero = jnp.zeros((), BF16)
    for i in range(qpb):
        row = (j * qpb + i) * LANES + lax.broadcasted_iota(I32, (1, BAND_KEYS), 1)
        valid = jnp.where(row >= row_lo, row, row_hi) < row_hi
        qrows = slice(i * LANES, (i + 1) * LANES)
        krows = slice(i * LANES, i * LANES + BAND_KEYS)
        for pr in range(N_HEADS // 2):
            sl = slice(pr * LANES, (pr + 1) * LANES)
            qp = q_ref[qrows, sl]
            lhs = jnp.concatenate([jnp.where(lo_half, qp, zero), jnp.where(lo_half, zero, qp)], axis=0)
            s = lax.dot_general(lhs, kb[krows, sl], (((1,), (1,)), ((), ())),
                                preferred_element_type=F32)
            bias2 = jnp.concatenate([bias_ref[2 * pr], bias_ref[2 * pr + 1]], axis=0)
            s = jnp.where(valid, s + bias2, NEG)
            m = jnp.max(s, axis=-1, keepdims=True)
            pexp = jnp.exp(s - m)
            inv = 1.0 / jnp.sum(pexp, axis=-1, keepdims=True)
            o = jnp.dot(pexp.astype(BF16), vb[krows, sl], preferred_element_type=F32) * inv
            o_ref[qrows, sl] = jnp.where(lo_half, o[:LANES], o[LANES:]).astype(BF16)


def _band(q, kp, vp, bias, *, row_lo, row_hi):
    nb, tq, _ = q.shape
    qpb = 2 if tq % (2 * LANES) == 0 else 1
    nkb = BAND_PAST // LANES + qpb
    kspecs = [pl.BlockSpec((None, LANES, ATT_W), (lambda b, j, c=c: (b, j * qpb + c, 0)))
              for c in range(nkb)]
    kern = functools.partial(_band_kernel, row_lo=row_lo, row_hi=row_hi, qpb=qpb)
    qspec = pl.BlockSpec((None, qpb * LANES, ATT_W), lambda b, j: (b, j, 0))
    return pl.pallas_call(
        kern, out_shape=jax.ShapeDtypeStruct((nb, tq, ATT_W), BF16),
        grid=(nb, tq // (qpb * LANES)),
        in_specs=[qspec] + kspecs + kspecs + [pl.BlockSpec(bias.shape, lambda b, j: (0, 0, 0))],
        out_specs=qspec,
        compiler_params=_cparams(("parallel", "parallel")), name="band",
    )(q, *([kp] * nkb), *([vp] * nkb), bias)


def _merge_kernel(x_ref, oa_ref, ob_ref, gattn_ref, wg_ref, wa_ref, wb_ref, wo_ref,
                  gffn_ref, wpqT_ref, c1_ref, c2_ref,
                  x2_ref, h2T_ref, s1_ref, s2_ref):
    x = x_ref[...]
    ms = jnp.mean(x * x, axis=-1, keepdims=True)
    h = (x * lax.rsqrt(ms + EPS) * gattn_ref[...]).astype(BF16)
    ga = jax.nn.sigmoid(jnp.dot(h, wg_ref[:, :D_MODEL], preferred_element_type=F32))
    gb = jax.nn.sigmoid(jnp.dot(h, wg_ref[:, D_MODEL:], preferred_element_type=F32))
    ma = jnp.dot(oa_ref[...], wa_ref[...], preferred_element_type=F32)
    mb = jnp.dot(ob_ref[...], wb_ref[...], preferred_element_type=F32)
    merged = (ga * ma + gb * mb).astype(BF16)
    x2 = x + jnp.dot(merged, wo_ref[...], preferred_element_type=F32)
    x2_ref[...] = x2
    ms2 = jnp.mean(x2 * x2, axis=-1, keepdims=True)
    h2T = (x2 * lax.rsqrt(ms2 + EPS) * gffn_ref[...]).T.astype(BF16)
    h2T_ref[...] = h2T
    half = PEER_DQ // 2
    for hh in range(PEER_HEADS):
        qT = jnp.dot(wpqT_ref[hh * PEER_DQ:(hh + 1) * PEER_DQ, :], h2T,
                     preferred_element_type=F32).astype(BF16)
        s1_ref[hh] = jnp.dot(c1_ref[...], qT[:half], preferred_element_type=F32)
        s2_ref[hh] = jnp.dot(c2_ref[...], qT[half:], preferred_element_type=F32)


def _merge(x, oa, ob, gattn, wg, wa, wb, wo, gffn, wpqT, c1, c2):
    n = x.shape[0]
    tm = ROW_TILE
    row = lambda i: (i, 0)
    const = lambda i: (0, 0)
    return pl.pallas_call(
        _merge_kernel,
        out_shape=(
            jax.ShapeDtypeStruct((n, D_MODEL), F32),
            jax.ShapeDtypeStruct((D_MODEL, n), BF16),
            jax.ShapeDtypeStruct((PEER_HEADS, N_KEYS, n), F32),
            jax.ShapeDtypeStruct((PEER_HEADS, N_KEYS, n), F32),
        ),
        grid=(n // tm,),
        in_specs=[
            pl.BlockSpec((tm, D_MODEL), row),
            pl.BlockSpec((tm, ATT_W), row),
            pl.BlockSpec((tm, ATT_W), row),
            pl.BlockSpec((1, D_MODEL), const),
            pl.BlockSpec(wg.shape, const),
            pl.BlockSpec(wa.shape, const),
            pl.BlockSpec(wb.shape, const),
            pl.BlockSpec(wo.shape, const),
            pl.BlockSpec((1, D_MODEL), const),
            pl.BlockSpec(wpqT.shape, const),
            pl.BlockSpec(c1.shape, const),
            pl.BlockSpec(c2.shape, const),
        ],
        out_specs=(
            pl.BlockSpec((tm, D_MODEL), row),
            pl.BlockSpec((D_MODEL, tm), lambda i: (0, i)),
            pl.BlockSpec((PEER_HEADS, N_KEYS, tm), lambda i: (0, 0, i)),
            pl.BlockSpec((PEER_HEADS, N_KEYS, tm), lambda i: (0, 0, i)),
        ),
        compiler_params=_cparams(("parallel",)), name="merge",
    )(x, oa, ob, gattn, wg, wa, wb, wo, gffn, wpqT, c1, c2)


def _pack_rows(x):
    return pltpu.bitcast(x, jnp.uint32)


def _unpack_rows(x):
    return pltpu.bitcast(x, BF16)


def _top16(x, distinct):
    idx = lax.broadcasted_iota(I32, x.shape, 0)
    rank = jnp.full(x.shape, PEER_TOPK, I32)
    rows = []
    for r in range(PEER_TOPK):
        m = jnp.max(x, axis=0, keepdims=True)
        rows.append(m)
        hit = x == m
        if not distinct:
            hit = idx == jnp.min(jnp.where(hit, idx, N_KEYS), axis=0, keepdims=True)
        rank = jnp.where(hit, r, rank)
        x = jnp.where(hit, -jnp.inf, x)
    return jnp.concatenate(rows, axis=0), rank


def _select_kernel(s1_ref, s2_ref, cnt_ref, e1_ref, r2_ref, e2_ref):
    sub8 = lax.broadcasted_iota(I32, (8, 1), 0).astype(F32)
    sub16 = lax.broadcasted_iota(I32, (PEER_TOPK, 1), 0).astype(F32)

    def count(mask):
        return jnp.sum(jnp.where(mask, 1.0, 0.0), axis=0, keepdims=True)

    def head_body(hh, carry):
        s1 = s1_ref[hh]
        s2 = s2_ref[hh]
        v1, r1 = _top16(s1, True)
        v2, r2 = _top16(s2, True)
        ranked = jnp.maximum(count(r1 < PEER_TOPK), count(r2 < PEER_TOPK))
        v1, r1, v2, r2 = lax.cond(
            jnp.max(ranked) > PEER_TOPK,
            lambda: _top16(s1, False) + _top16(s2, False),
            lambda: (v1, r1, v2, r2))
        pieces = [v1[0:1] + v2]
        for a in range(1, 8):
            piece = v1[a:a + 1] + v2[0:8]
            pieces.append(jnp.where(sub8 < PEER_TOPK // (a + 1), piece, -jnp.inf))
        pieces.append(v1[8:16] + v2[0:1])
        cand = jnp.concatenate(pieces, axis=0)
        rem = jnp.full((1, cand.shape[1]), float(PEER_TOPK), F32)
        tau = jnp.zeros((1, cand.shape[1]), F32)
        x = cand
        for _ in range(PEER_TOPK):
            m = jnp.max(x, axis=0, keepdims=True)
            hit = x == m
            tau = jnp.where(rem > 0, m, tau)
            rem = rem - count(hit)
            x = jnp.where(hit, -jnp.inf, x)
        need = PEER_TOPK - count(cand > tau)
        kept = []
        for a in range(PEER_TOPK):
            row = pieces[a] if a < 8 else pieces[8][a - 8:a - 7]
            ties = count(row == tau)
            kept.append(count(row > tau) + jnp.minimum(ties, jnp.maximum(need, 0.0)))
            need = need - ties
        smax = v1[0:1] + v2[0:1]
        z = jnp.sum(jnp.where(sub16 < kept[0], jnp.exp(pieces[0] - smax), 0.0), axis=0, keepdims=True)
        for a in range(1, 8):
            z = z + jnp.sum(jnp.where(sub8 < kept[a], jnp.exp(pieces[a] - smax), 0.0),
                            axis=0, keepdims=True)
        tail = jnp.concatenate(kept[8:], axis=0)
        z = z + jnp.sum(jnp.where(tail > 0, jnp.exp(pieces[8] - smax), 0.0), axis=0, keepdims=True)
        cnt = jnp.zeros(s1.shape, F32)
        for a in range(PEER_TOPK):
            cnt = jnp.where(r1 == a, kept[a], cnt)
        cnt_ref[hh] = cnt
        e1_ref[hh] = jnp.exp(s1 - v1[0:1]) * (1.0 / z)
        r2b = r2.astype(F32).astype(BF16)
        e2b = jnp.exp(s2 - v2[0:1]).astype(BF16)
        for c in range(s2.shape[1] // LANES):
            cs = slice(c * LANES, (c + 1) * LANES)
            r2_ref[hh, c] = _pack_rows(r2b[:, cs])
            e2_ref[hh, c] = _pack_rows(e2b[:, cs])
        return carry

    lax.fori_loop(0, PEER_HEADS, head_body, 0)


def _select(s1, s2):
    n = s1.shape[-1]
    tn = SEL_TILE
    blk = pl.BlockSpec((PEER_HEADS, N_KEYS, tn), lambda i: (0, 0, i))
    packed = jax.ShapeDtypeStruct((PEER_HEADS, n // LANES, N_KEYS // 2, LANES), jnp.uint32)
    pblk = pl.BlockSpec((PEER_HEADS, tn // LANES, N_KEYS // 2, LANES), lambda i: (0, i, 0, 0))
    return pl.pallas_call(
        _select_kernel,
        out_shape=(
            jax.ShapeDtypeStruct(s1.shape, F32),
            jax.ShapeDtypeStruct(s1.shape, F32),
            packed, packed,
        ),
        grid=(n // tn,),
        in_specs=[blk, blk],
        out_specs=(blk, blk, pblk, pblk),
        compiler_params=_cparams(("parallel",)), name="select",
    )(s1, s2)


def _peer_kernel(h2T_ref, u_ref, vT_ref, vTp_ref, cnt_ref, e1_ref, r2_ref, e2_ref, x2_ref,
                 y_ref, acc_ref, wa0_ref, wa1_ref, z0_ref, z1_ref, rowc_ref, rowe_ref):
    e = pl.program_id(1)
    zero = jnp.minimum(e, 0)
    z_refs = (z0_ref, z1_ref)
    wa_refs = (wa0_ref, wa1_ref)

    @pl.when(e == 0)
    def _():
        acc_ref[...] = jnp.zeros_like(acc_ref)
        wa_refs[(PEER_EB // PEER_GRP - 1) % 2][...] = jnp.zeros_like(wa1_ref)

    tn = h2T_ref.shape[1]
    nsub = PEER_EB // N_KEYS
    reps = N_KEYS // BF16_ROWS
    for sub in range(nsub):
        for hh in range(PEER_HEADS):
            rowc_ref[sub, hh] = _pack_rows(
                jnp.broadcast_to(cnt_ref[hh, sub:sub + 1, :], (BF16_ROWS, tn)).astype(BF16))
            rowe_ref[sub, hh] = _pack_rows(
                jnp.broadcast_to(e1_ref[hh, sub:sub + 1, :], (BF16_ROWS, tn)).astype(BF16))

    ngrp = PEER_EB // PEER_GRP

    def scores(g, slot):
        start = g * PEER_GRP if isinstance(g, int) else pl.multiple_of(g * PEER_GRP, PEER_GRP)
        z_refs[slot][0] = jnp.dot(u_ref[pl.ds(start, PEER_GRP), :], h2T_ref[...],
                                  preferred_element_type=F32)

    def weights(g, slot):
        for s in range(PEER_GRP // N_KEYS):
            sub = g * (PEER_GRP // N_KEYS) + s
            for c in range(tn // LANES):
                cs = slice(c * LANES, (c + 1) * LANES)
                zc = z_refs[slot][zero, s * N_KEYS:(s + 1) * N_KEYS, cs]
                a = (0.5 * zc * (1.0 + lax.erf(zc * (2.0 ** -0.5)))).astype(BF16)
                w = jnp.zeros((N_KEYS, LANES), BF16)
                for hh in range(PEER_HEADS):
                    cnt = jnp.concatenate([_unpack_rows(rowc_ref[sub + zero, hh, :, cs])] * reps, axis=0)
                    e1 = jnp.concatenate([_unpack_rows(rowe_ref[sub + zero, hh, :, cs])] * reps, axis=0)
                    w = w + jnp.where(_unpack_rows(r2_ref[hh, c]) < cnt,
                                      _unpack_rows(e2_ref[hh, c]) * e1, jnp.zeros((), BF16))
                wa_refs[slot][0, s * (N_KEYS // 2):(s + 1) * (N_KEYS // 2), cs] = _pack_rows(w * a)

    def down(vt, slot):
        acc_ref[...] += jnp.dot(vt, _unpack_rows(wa_refs[slot][zero]), preferred_element_type=F32)

    last = (ngrp - 1) % 2
    scores(0, 0)
    down(vTp_ref[0], last)
    for g in range(ngrp):
        if g + 1 < ngrp:
            scores(g + 1, (g + 1) % 2)
        weights(g, g % 2)
        if g > 0:
            down(vT_ref[g - 1], (g - 1) % 2)

    @pl.when(e == pl.num_programs(1) - 1)
    def _():
        down(vT_ref[ngrp - 1], last)
        y_ref[...] = x2_ref[...] + acc_ref[...].T


def _peer(h2T, u, vT, cnt, e1, r2, e2, x2):
    n = x2.shape[0]
    tn = min(PEER_TOK, n)
    nsub = PEER_EB // N_KEYS
    tok3 = pl.BlockSpec((PEER_HEADS, tn // LANES, N_KEYS // 2, LANES), lambda i, e: (0, i, 0, 0))
    key1 = pl.BlockSpec((PEER_HEADS, nsub, tn), lambda i, e: (0, e, i))
    return pl.pallas_call(
        _peer_kernel,
        out_shape=jax.ShapeDtypeStruct((n, D_MODEL), F32),
        grid=(n // tn, N_EXPERTS // PEER_EB),
        in_specs=[
            pl.BlockSpec((D_MODEL, tn), lambda i, e: (0, i)),
            pl.BlockSpec((PEER_EB, D_MODEL), lambda i, e: (e, 0)),
            pl.BlockSpec((PEER_EB // PEER_GRP, D_MODEL, PEER_GRP), lambda i, e: (e, 0, 0)),
            pl.BlockSpec((1, D_MODEL, PEER_GRP),
                         lambda i, e: (jnp.maximum(e * (PEER_EB // PEER_GRP) - 1, 0), 0, 0)),
            key1, key1, tok3, tok3,
            pl.BlockSpec((tn, D_MODEL), lambda i, e: (i, 0)),
        ],
        out_specs=pl.BlockSpec((tn, D_MODEL), lambda i, e: (i, 0)),
        scratch_shapes=[
            pltpu.VMEM((D_MODEL, tn), F32),
            pltpu.VMEM((1, PEER_GRP // 2, tn), jnp.uint32),
            pltpu.VMEM((1, PEER_GRP // 2, tn), jnp.uint32),
            pltpu.VMEM((1, PEER_GRP, tn), F32),
            pltpu.VMEM((1, PEER_GRP, tn), F32),
            pltpu.VMEM((nsub, PEER_HEADS, BF16_ROWS // 2, tn), jnp.uint32),
            pltpu.VMEM((nsub, PEER_HEADS, BF16_ROWS // 2, tn), jnp.uint32),
        ],
        compiler_params=_cparams(("parallel", "arbitrary")), name="peer",
    )(h2T, u, vT, vT, cnt, e1, r2, e2, x2)


def _rope_tables(pos):
    posf = pos.astype(F32)[:, None]
    t = pos.shape[0]

    def tab(half, reps):
        inv = jnp.power(ROPE_THETA, -jnp.arange(half, dtype=F32) / half)
        ang = posf * inv[None, :]
        cos = jnp.cos(ang)
        sin = jnp.sin(ang)
        return (jnp.tile(jnp.concatenate([cos, cos], axis=1), (1, reps)),
                jnp.tile(jnp.concatenate([-sin, sin], axis=1), (1, reps)))

    cosq, sinq = tab(HEAD_DIM // 2, N_HEADS)
    cosi, sini = tab(IDX_DIM // 2, IDX_HEADS + 1)
    pad = IDX_W - cosi.shape[1]
    cosi = jnp.concatenate([cosi, jnp.ones((t, pad), F32)], axis=1)
    sini = jnp.concatenate([sini, jnp.zeros((t, pad), F32)], axis=1)
    return cosq, sinq, cosi, sini


def _band_bias(rel_bias):
    qi = np.arange(LANES)[:, None]
    kj = np.arange(BAND_KEYS)[None, :]
    ok = (kj // CHUNK >= qi // CHUNK) & (kj // CHUNK <= qi // CHUNK + BAND_PAST // CHUNK)
    n = BAND_KEYS + LANES - 1
    m = np.arange(n)
    diag = rel_bias.astype(F32)[:, np.clip(BAND_PAST + LANES - 1 - m, -REL_CLIP, REL_CLIP) + REL_CLIP]
    wrapped = jnp.tile(jnp.pad(diag, ((0, 0), (0, 1))), (1, LANES))[:, :LANES * n]
    toep = wrapped.reshape(-1, LANES, n)[:, :, LANES - 1:LANES - 1 + BAND_KEYS]
    return jnp.where(jnp.asarray(ok)[None], toep, NEG)


def _layer_weights(g_attn, w_in, g_qa, g_ka, g_ik, g_qb, g_kb, rel_bias, w_br_a, w_br_b,
                   w_out, g_ffn, w_pq, c1, c2, u, v):
    a0 = 3 * ATT_W
    i0 = a0 + IDX_HEADS * IDX_DIM + IDX_DIM + IDX_HEADS
    b0 = i0 + 3 * ATT_W
    idx_cols = i0 - a0
    w1 = jnp.concatenate(
        [w_in[:, :a0], w_in[:, i0:b0], w_in[:, a0:i0],
         jnp.zeros((D_MODEL, IDX_W - idx_cols), w_in.dtype)], axis=1).astype(BF16)
    wg = w_in[:, b0:].astype(BF16)
    hd = jnp.asarray(np.kron(np.eye(N_HEADS), np.ones((HEAD_DIM, HEAD_DIM))) / HEAD_DIM, BF16)
    gh = jnp.stack([jnp.tile(g, N_HEADS) for g in (g_qa, g_ka, g_qb, g_kb)]).astype(F32)
    gidx = jnp.concatenate(
        [g_ik.astype(F32), jnp.full((IDX_HEADS,), IDX_HEADS ** -0.5, F32),
         jnp.ones((LANES - IDX_DIM - IDX_HEADS,), F32)])[None, :]
    tri = jnp.asarray(np.tril(np.ones((KEY_TILE, KEY_TILE)), -1), BF16)
    return dict(
        w1=w1, wg=wg, hd=hd, gh=gh, gidx=gidx, tri=tri,
        gattn=g_attn.astype(F32)[None, :], gffn=g_ffn.astype(F32)[None, :],
        bias=_band_bias(rel_bias),
        wa=w_br_a.astype(BF16), wb=w_br_b.astype(BF16), wo=w_out.astype(BF16),
        wpqT=w_pq.T.astype(BF16), c1=c1.astype(BF16), c2=c2.astype(BF16),
        u=u.astype(BF16),
        vT=v.astype(BF16).reshape(N_EXPERTS // PEER_GRP, PEER_GRP, D_MODEL).transpose(0, 2, 1))


def _ffn(x, oa, ob, w):
    x2, h2T, s1, s2 = _merge(x, oa, ob, w["gattn"], w["wg"], w["wa"], w["wb"], w["wo"],
                             w["gffn"], w["wpqT"], w["c1"], w["c2"])
    cnt, e1, r2, e2 = _select(s1, s2)
    return _peer(h2T, w["u"], w["vT"], cnt, e1, r2, e2, x2)


def _prompt_layer(x, w):
    b, t, _ = x.shape
    xf = x.reshape(b * t, D_MODEL)
    tabs = _rope_tables(jnp.arange(t))
    (aqT, ak, akb, av, avT, iqT, ik, ikb, iwT, bq, bk, bkb, bv, bvb) = _proj(
        xf, b, t, w["w1"], w["gattn"], w["hd"], w["gh"], w["gidx"], *tabs)
    oa = _dsa(aqT, iqT, iwT, akb.reshape(b, t, ATT_W), avT, ikb.reshape(b, t, LANES),
              w["tri"], q0=0, n_valid=t)
    front = ((0, 0), (BAND_PAST, 0), (0, 0))
    ob = _band(bq.reshape(b, t, ATT_W), jnp.pad(bkb.reshape(b, t, ATT_W), front),
               jnp.pad(bvb.reshape(b, t, ATT_W), front), w["bias"],
               row_lo=BAND_PAST, row_hi=BAND_PAST + t)
    y = _ffn(xf, oa.reshape(b * t, ATT_W), ob.reshape(b * t, ATT_W), w)
    keep = min(BAND_PAST, t)
    heads = lambda a: a.reshape(b, t, N_HEADS, HEAD_DIM)
    return y.reshape(b, t, D_MODEL), (
        heads(ak), heads(av), ik.reshape(b, t, IDX_DIM),
        heads(bk)[:, t - keep:], heads(bv)[:, t - keep:])


def _sample_layer(x, ca_k, ca_v, ca_ik, cb_k, cb_v, w):
    b, t, _ = x.shape
    n = b * t
    past = ca_k.shape[1]
    xf = x.reshape(n, D_MODEL)
    tabs = _rope_tables(jnp.tile(past + jnp.arange(t), b))
    (aqT, ak, akb, av, avT, iqT, ik, ikb, iwT, bq, bk, bkb, bv, bvb) = _proj(
        xf, 1, n, w["w1"], w["gattn"], w["hd"], w["gh"], w["gidx"], *tabs)

    def per_seq_T(aT, rows):
        a = aT[0].T.reshape(b, t, rows)
        return jnp.pad(a, ((0, 0), (0, LANES - t), (0, 0))).transpose(0, 2, 1)

    lk = past + t
    lp = -(-lk // KEY_TILE) * KEY_TILE
    padk = ((0, 0), (0, lp - lk), (0, 0))
    k_all = jnp.pad(jnp.concatenate(
        [ca_k.reshape(b, past, ATT_W).astype(BF16), akb.reshape(b, t, ATT_W)], axis=1), padk)
    v_all = jnp.pad(jnp.concatenate(
        [ca_v.reshape(b, past, ATT_W).astype(BF16), av.astype(BF16).reshape(b, t, ATT_W)],
        axis=1), padk)
    vT_all = v_all.reshape(b, lp // KEY_TILE, KEY_TILE, ATT_W).transpose(0, 1, 3, 2)
    ik_all = jnp.pad(jnp.concatenate(
        [jnp.pad(ca_ik.astype(BF16), ((0, 0), (0, 0), (0, LANES - IDX_DIM))),
         ikb.reshape(b, t, LANES)], axis=1), padk)
    oa = _dsa(per_seq_T(aqT, ATT_W), per_seq_T(iqT, 2 * LANES), per_seq_T(iwT, IDX_HEADS),
              k_all, vT_all, ik_all, w["tri"], q0=past, n_valid=lk)[:, :t]

    pb = cb_k.shape[1]
    padb = ((0, 0), (BAND_PAST - pb, LANES - t), (0, 0))
    qpad = jnp.pad(bq.reshape(b, t, ATT_W), ((0, 0), (0, LANES - t), (0, 0)))
    kb_all = jnp.pad(jnp.concatenate(
        [cb_k.reshape(b, pb, ATT_W).astype(BF16), bkb.reshape(b, t, ATT_W)], axis=1), padb)
    vb_all = jnp.pad(jnp.concatenate(
        [cb_v.reshape(b, pb, ATT_W).astype(BF16), bvb.reshape(b, t, ATT_W)], axis=1), padb)
    ob = _band(qpad, kb_all, vb_all, w["bias"],
               row_lo=BAND_PAST - pb, row_hi=BAND_PAST + t)[:, :t]

    y = _ffn(xf, oa.reshape(n, ATT_W), ob.reshape(n, ATT_W), w)
    keep = min(BAND_PAST, t)
    heads = lambda a: a.reshape(b, t, N_HEADS, HEAD_DIM)
    return y.reshape(b, t, D_MODEL), (
        heads(ak), heads(av), ik.reshape(b, t, IDX_DIM),
        heads(bk)[:, t - keep:], heads(bv)[:, t - keep:])


def kernel(x_prompt, x_sample, cache_a_k, cache_a_v, cache_a_ik, cache_b_k, cache_b_v,
           g_attn, w_in, g_qa, g_ka, g_ik, g_qb, g_kb, rel_bias, w_br_a, w_br_b, w_out,
           g_ffn, w_pq, peer_c1, peer_c2, peer_u, peer_v):
    depth = w_in.shape[0]
    xp, xs = x_prompt, x_sample
    sp, ss = [], []
    for l in range(depth):
        w = _layer_weights(g_attn[l], w_in[l], g_qa[l], g_ka[l], g_ik[l], g_qb[l], g_kb[l],
                           rel_bias[l], w_br_a[l], w_br_b[l], w_out[l], g_ffn[l], w_pq[l],
                           peer_c1[l], peer_c2[l], peer_u[l], peer_v[l])
        xp, st_p = _prompt_layer(xp, w)
        xs, st_s = _sample_layer(xs, cache_a_k[l], cache_a_v[l], cache_a_ik[l],
                                 cache_b_k[l], cache_b_v[l], w)
        sp.append(st_p)
        ss.append(st_s)
    stack = lambda sts, i: jnp.stack([s[i] for s in sts])
    return (xp, xs) + tuple(stack(sp, i) for i in range(5)) + tuple(stack(ss, i) for i in range(5))
```

```python
import functools

import jax
import jax.numpy as jnp
import numpy as np
from jax import lax
from jax.experimental import pallas as pl
from jax.experimental.pallas import tpu as pltpu

F32 = jnp.float32
BF16 = jnp.bfloat16
I32 = jnp.int32

D_MODEL = 1024
HEAD_DIM = 64
N_HEADS = 8
ATT_W = N_HEADS * HEAD_DIM
IDX_HEADS = 8
IDX_DIM = 32
IDX_W = 384
CHUNK = 64
TOPK_MAX = 256
BAND_PAST = 512
BAND_KEYS = BAND_PAST + 128
REL_CLIP = 128
PEER_HEADS = 8
PEER_DQ = 256
N_KEYS = 128
N_EXPERTS = N_KEYS * N_KEYS
PEER_TOPK = 16
ROPE_THETA = 10000.0
EPS = 1e-6
NEG = -1e30
INT_MIN = -(2 ** 31)

LANES = 128
BF16_ROWS = 16
KEY_TILE = 256
ROW_TILE = 512
SEL_TILE = 256
PEER_TOK = 1024
PEER_EB = 1024
PEER_GRP = 256
VMEM_LIMIT = 56 * 1024 * 1024


def _cparams(sem):
    return pltpu.CompilerParams(dimension_semantics=sem, vmem_limit_bytes=VMEM_LIMIT)


def _rope_chunks(y, cos, sin, half):
    lane = lax.broadcasted_iota(I32, (1, LANES), 1)
    lo = (lane % (2 * half)) < half
    out = []
    for c in range(y.shape[1] // LANES):
        sl = slice(c * LANES, (c + 1) * LANES)
        yc = y[:, sl]
        partner = jnp.where(lo, pltpu.roll(yc, LANES - half, 1), pltpu.roll(yc, half, 1))
        out.append(yc * cos[:, sl] + partner * sin[:, sl])
    return jnp.concatenate(out, axis=1)


def _proj_kernel(x_ref, gattn_ref, w1_ref, hd_ref, gh_ref, gidx_ref,
                 cosq_ref, sinq_ref, cosi_ref, sini_ref,
                 aqT_ref, ak_ref, akb_ref, av_ref, avT_ref, iqT_ref, ik_ref, ikb_ref,
                 iwT_ref, bq_ref, bk_ref, bkb_ref, bv_ref, bvb_ref):
    x = x_ref[...]
    ms = jnp.mean(x * x, axis=-1, keepdims=True)
    h = (x * lax.rsqrt(ms + EPS) * gattn_ref[...]).astype(BF16)

    def seg(i):
        return jnp.dot(h, w1_ref[:, i * ATT_W:(i + 1) * ATT_W], preferred_element_type=F32)

    def headnorm(y, gi):
        hms = jnp.dot((y * y).astype(BF16), hd_ref[...], preferred_element_type=F32)
        return y * lax.rsqrt(hms + EPS) * gh_ref[gi:gi + 1, :]

    cosq = cosq_ref[...]
    sinq = sinq_ref[...]

    aq = _rope_chunks(headnorm(seg(0), 0), cosq, sinq, HEAD_DIM // 2) * (HEAD_DIM ** -0.5)
    aqT_ref[...] = aq.T.astype(BF16)

    ak = _rope_chunks(headnorm(seg(1), 1), cosq, sinq, HEAD_DIM // 2)
    ak_ref[...] = ak
    akb_ref[...] = ak.astype(BF16)

    av = seg(2)
    av_ref[...] = av
    for c in range(av.shape[0] // KEY_TILE):
        avT_ref[c] = av[c * KEY_TILE:(c + 1) * KEY_TILE, :].T.astype(BF16)

    bq_ref[...] = (headnorm(seg(3), 2) * (HEAD_DIM ** -0.5)).astype(BF16)
    bk = headnorm(seg(4), 3)
    bk_ref[...] = bk
    bkb_ref[...] = bk.astype(BF16)
    bv = seg(5)
    bv_ref[...] = bv
    bvb_ref[...] = bv.astype(BF16)

    s = jnp.dot(h, w1_ref[:, 6 * ATT_W:6 * ATT_W + IDX_W], preferred_element_type=F32)
    lane = lax.broadcasted_iota(I32, (1, LANES), 1)
    is_ik = lane < IDX_DIM
    c2 = s[:, 2 * LANES:3 * LANES]
    ikms = jnp.sum(jnp.where(is_ik, c2 * c2, 0.0), axis=-1, keepdims=True) * (1.0 / IDX_DIM)
    c2n = c2 * jnp.where(is_ik, lax.rsqrt(ikms + EPS), 1.0) * gidx_ref[...]
    sn = jnp.concatenate([s[:, :2 * LANES], c2n], axis=1)
    r = _rope_chunks(sn, cosi_ref[...], sini_ref[...], IDX_DIM // 2)
    iqT_ref[...] = r[:, :2 * LANES].T.astype(BF16)
    c2r = r[:, 2 * LANES:]
    ik_ref[...] = c2r[:, :IDX_DIM]
    ikb_ref[...] = jnp.where(is_ik, c2r, 0.0).astype(BF16)
    iwT_ref[...] = c2r.T[IDX_DIM:IDX_DIM + IDX_HEADS, :]


def _proj(x, nb, tb, w1, gattn, hd, gh, gidx, cosq, sinq, cosi, sini):
    n = x.shape[0]
    tm = ROW_TILE
    tpb = tb // tm
    row = lambda i: (i, 0)
    tab = lambda i: (i % tpb, 0)
    const = lambda i: (0, 0)
    grp = lambda i: (i // tpb, 0, i % tpb)
    out_shape = (
        jax.ShapeDtypeStruct((nb, ATT_W, tb), BF16),
        jax.ShapeDtypeStruct((n, ATT_W), F32),
        jax.ShapeDtypeStruct((n, ATT_W), BF16),
        jax.ShapeDtypeStruct((n, ATT_W), F32),
        jax.ShapeDtypeStruct((nb, tb // KEY_TILE, ATT_W, KEY_TILE), BF16),
        jax.ShapeDtypeStruct((nb, 2 * LANES, tb), BF16),
        jax.ShapeDtypeStruct((n, IDX_DIM), F32),
        jax.ShapeDtypeStruct((n, LANES), BF16),
        jax.ShapeDtypeStruct((nb, IDX_HEADS, tb), F32),
        jax.ShapeDtypeStruct((n, ATT_W), BF16),
        jax.ShapeDtypeStruct((n, ATT_W), F32),
        jax.ShapeDtypeStruct((n, ATT_W), BF16),
        jax.ShapeDtypeStruct((n, ATT_W), F32),
        jax.ShapeDtypeStruct((n, ATT_W), BF16),
    )
    out_specs = (
        pl.BlockSpec((None, ATT_W, tm), grp),
        pl.BlockSpec((tm, ATT_W), row),
        pl.BlockSpec((tm, ATT_W), row),
        pl.BlockSpec((tm, ATT_W), row),
        pl.BlockSpec((None, tm // KEY_TILE, ATT_W, KEY_TILE),
                     lambda i: (i // tpb, i % tpb, 0, 0)),
        pl.BlockSpec((None, 2 * LANES, tm), grp),
        pl.BlockSpec((tm, IDX_DIM), row),
        pl.BlockSpec((tm, LANES), row),
        pl.BlockSpec((None, IDX_HEADS, tm), grp),
        pl.BlockSpec((tm, ATT_W), row),
        pl.BlockSpec((tm, ATT_W), row),
        pl.BlockSpec((tm, ATT_W), row),
        pl.BlockSpec((tm, ATT_W), row),
        pl.BlockSpec((tm, ATT_W), row),
    )
    in_specs = [
        pl.BlockSpec((tm, D_MODEL), row),
        pl.BlockSpec((1, D_MODEL), const),
        pl.BlockSpec(w1.shape, const),
        pl.BlockSpec(hd.shape, const),
        pl.BlockSpec(gh.shape, const),
        pl.BlockSpec(gidx.shape, const),
        pl.BlockSpec((tm, ATT_W), tab),
        pl.BlockSpec((tm, ATT_W), tab),
        pl.BlockSpec((tm, IDX_W), tab),
        pl.BlockSpec((tm, IDX_W), tab),
    ]
    return pl.pallas_call(
        _proj_kernel, out_shape=out_shape, grid=(n // tm,),
        in_specs=in_specs, out_specs=out_specs,
        compiler_params=_cparams(("parallel",)), name="proj",
    )(x, gattn, w1, hd, gh, gidx, cosq, sinq, cosi, sini)


def _dsa_kernel(qT_ref, iqT_ref, iwT_ref, k_ref, vT_ref, ikb_ref, tri_ref,
                o_ref, sc_ref, am_ref, s_ref, oT_ref, acc_ref, *, q0, n_valid, n_key_tiles, n_sel):
    j = pl.program_id(1)
    qbase = q0 + j * LANES
    n_t = jnp.minimum(lax.shift_right_logical(qbase + LANES + KEY_TILE - 1, 8), n_key_tiles)
    qchunk = lax.shift_right_logical(
        qbase + lax.broadcasted_iota(I32, (1, LANES), 1), 6)

    def rows(t):
        return pl.ds(pl.multiple_of(t * KEY_TILE, KEY_TILE), KEY_TILE)

    def admissible(t):
        kidx = t * KEY_TILE + lax.broadcasted_iota(I32, (KEY_TILE, LANES), 0)
        return jnp.where(kidx < n_valid, lax.shift_right_logical(kidx, 6), 1 << 30) <= qchunk

    def fold8(v):
        return v.reshape(KEY_TILE // 8, 8, v.shape[-1])

    zpad = jnp.zeros((LANES - IDX_DIM, LANES), BF16)
    rhs_idx = jnp.concatenate(
        [jnp.concatenate([iqT_ref[h * IDX_DIM:(h + 1) * IDX_DIM, :], zpad], axis=0)
         for h in range(IDX_HEADS)], axis=1)

    def score_body(t, carry, width):
        both = pl.ds(pl.multiple_of(t * KEY_TILE, KEY_TILE), width * KEY_TILE)
        res = jnp.dot(ikb_ref[both, :], rhs_idx, preferred_element_type=F32)
        for w in range(width):
            part = res[w * KEY_TILE:(w + 1) * KEY_TILE]
            acc = jnp.zeros((KEY_TILE, LANES), F32)
            for h in range(IDX_HEADS):
                acc = acc + jnp.maximum(part[:, h * LANES:(h + 1) * LANES], 0.0) * iwT_ref[h:h + 1, :]
            sc_ref[rows(t + w), :] = jnp.where(admissible(t + w), acc, NEG)
        return carry

    n_pairs = lax.shift_right_logical(n_t, 1)
    lax.fori_loop(0, n_pairs, lambda i, c: score_body(2 * i, c, 2), 0)
    lax.fori_loop(2 * n_pairs, n_t, lambda t, c: score_body(t, c, 1), 0)
    sc_ref[rows(n_t), :] = jnp.full((KEY_TILE, LANES), -jnp.inf, F32)

    def count(pred):
        def body(t2, c8):
            two = pl.ds(pl.multiple_of(t2 * (2 * KEY_TILE), 2 * KEY_TILE), 2 * KEY_TILE)
            hits = jnp.where(pred(sc_ref[two, :]), 1, 0)
            return c8 + jnp.sum(hits.reshape(2 * KEY_TILE // 8, 8, LANES), axis=0)
        c8 = lax.fori_loop(0, lax.shift_right_logical(n_t + 1, 1), body, jnp.zeros((8, LANES), I32))
        return jnp.sum(c8, axis=0, keepdims=True)

    def bisect_body(it, p):
        c = p + lax.shift_left(jnp.int32(1), 31 - it)
        cf = pltpu.bitcast(jnp.where(c >= 0, c, c ^ 0x7FFFFFFF), F32)
        return jnp.where(count(lambda v: v >= cf) >= n_sel, c, p)

    p = lax.fori_loop(0, 32, bisect_body, jnp.full((1, LANES), INT_MIN, I32))
    thr = pltpu.bitcast(jnp.where(p >= 0, p, p ^ 0x7FFFFFFF), F32)

    def mask_simple():
        def body(t, carry):
            keep = jnp.where(sc_ref[rows(t), :] >= thr, 0.0, NEG)
            am_ref[rows(t), :] = jnp.where(admissible(t), keep, NEG)
            return carry
        lax.fori_loop(0, n_t, body, 0)

    def mask_ties():
        need = (n_sel - count(lambda v: v > thr)).astype(F32)

        def body(t, carry):
            v = sc_ref[rows(t), :]
            eq = v == thr
            eqf = jnp.where(eq, 1.0, 0.0)
            pre = jnp.dot(tri_ref[...], eqf.astype(BF16), preferred_element_type=F32) + carry
            tie = jnp.where(eq, jnp.where(pre < need, 0.0, NEG), NEG)
            keep = jnp.where(v > thr, 0.0, tie)
            am_ref[rows(t), :] = jnp.where(admissible(t), keep, NEG)
            return carry + jnp.sum(jnp.sum(fold8(eqf), axis=0), axis=0, keepdims=True)
        lax.fori_loop(0, n_t, body, jnp.zeros((1, LANES), F32))

    lax.cond(jnp.max(count(lambda v: v >= thr)) == n_sel, mask_simple, mask_ties)

    npair = N_HEADS // 2
    pw = 2 * LANES
    zq = jnp.zeros((HEAD_DIM, LANES), BF16)
    rhs = [jnp.concatenate(
        [jnp.concatenate([qT_ref[pr * LANES:pr * LANES + HEAD_DIM, :], zq], axis=1),
         jnp.concatenate([zq, qT_ref[pr * LANES + HEAD_DIM:(pr + 1) * LANES, :]], axis=1)], axis=0)
        for pr in range(npair)]

    n_two = lax.shift_right_logical(n_t, 1)

    def span(t, width):
        return pl.ds(pl.multiple_of(t * KEY_TILE, KEY_TILE), width * KEY_TILE)

    def logit_body(t, m8, width):
        am = am_ref[span(t, width), :]
        am2 = jnp.concatenate([am, am], axis=1)
        tops = []
        for pr in range(npair):
            s = jnp.dot(k_ref[span(t, width), pr * LANES:(pr + 1) * LANES], rhs[pr],
                        preferred_element_type=F32) + am2
            s_ref[span(t, width), pr * pw:(pr + 1) * pw] = s
            tops.append(jnp.max(s.reshape(width * KEY_TILE // 8, 8, pw), axis=0))
        return jnp.maximum(m8, jnp.concatenate(tops, axis=1))

    m8 = lax.fori_loop(0, n_two, lambda i, c: logit_body(2 * i, c, 2),
                       jnp.full((8, npair * pw), NEG, F32))
    m8 = lax.fori_loop(2 * n_two, n_t, lambda t, c: logit_body(t, c, 1), m8)
    m = jnp.max(m8, axis=0, keepdims=True)
    acc_ref[...] = jnp.zeros_like(acc_ref)

    def pv_body(t, l8, width):
        sums = []
        for pr in range(npair):
            cols = slice(pr * pw, (pr + 1) * pw)
            pexp = jnp.exp(s_ref[span(t, width), cols] - m[:, cols])
            sums.append(jnp.sum(pexp.reshape(width * KEY_TILE // 8, 8, pw), axis=0))
            pb = pexp.astype(BF16)
            out = acc_ref[pr]
            for w in range(width):
                out = out + jnp.dot(vT_ref[t + w, pr * LANES:(pr + 1) * LANES, :],
                                    pb[w * KEY_TILE:(w + 1) * KEY_TILE],
                                    preferred_element_type=F32)
            acc_ref[pr] = out
        return l8 + jnp.concatenate(sums, axis=1)

    l8 = lax.fori_loop(0, n_two, lambda i, c: pv_body(2 * i, c, 2),
                       jnp.zeros((8, npair * pw), F32))
    l8 = lax.fori_loop(2 * n_two, n_t, lambda t, c: pv_body(t, c, 1), l8)
    inv = 1.0 / jnp.sum(l8, axis=0, keepdims=True)
    for pr in range(npair):
        lo = pr * LANES
        acc = acc_ref[pr]
        oT_ref[lo:lo + HEAD_DIM, :] = acc[:HEAD_DIM, :LANES] * inv[:, pr * pw:pr * pw + LANES]
        oT_ref[lo + HEAD_DIM:lo + LANES, :] = acc[HEAD_DIM:, LANES:] * inv[:, pr * pw + LANES:(pr + 1) * pw]

    o_ref[...] = oT_ref[...].T.astype(BF16)


def _dsa(qT, iqT, iwT, k, vT, ikb, tri, *, q0, n_valid):
    nb, _, tq = qT.shape
    lp = k.shape[1]
    qblk = lambda b, j: (b, 0, j)
    seq3 = lambda b, j: (b, 0, 0)
    seq4 = lambda b, j: (b, 0, 0, 0)
    kern = functools.partial(_dsa_kernel, q0=q0, n_valid=n_valid, n_key_tiles=lp // KEY_TILE,
                             n_sel=min(TOPK_MAX, n_valid // 4))
    return pl.pallas_call(
        kern, out_shape=jax.ShapeDtypeStruct((nb, tq, ATT_W), BF16),
        grid=(nb, tq // LANES),
        in_specs=[
            pl.BlockSpec((None, ATT_W, LANES), qblk),
            pl.BlockSpec((None, 2 * LANES, LANES), qblk),
            pl.BlockSpec((None, IDX_HEADS, LANES), qblk),
            pl.BlockSpec((None, lp, ATT_W), seq3),
            pl.BlockSpec((None, lp // KEY_TILE, ATT_W, KEY_TILE), seq4),
            pl.BlockSpec((None, lp, LANES), seq3),
            pl.BlockSpec((KEY_TILE, KEY_TILE), lambda b, j: (0, 0)),
        ],
        out_specs=pl.BlockSpec((None, LANES, ATT_W), lambda b, j: (b, j, 0)),
        scratch_shapes=[
            pltpu.VMEM((lp + KEY_TILE, LANES), F32),
            pltpu.VMEM((lp, LANES), F32),
            pltpu.VMEM((lp, N_HEADS * LANES), F32),
            pltpu.VMEM((ATT_W, LANES), F32),
            pltpu.VMEM((N_HEADS // 2, LANES, 2 * LANES), F32),
        ],
        compiler_params=_cparams(("parallel", "arbitrary")), name="dsa",
    )(qT, iqT, iwT, k, vT, ikb, tri)


def _band_kernel(*refs, row_lo, row_hi, qpb):
    nkb = BAND_PAST // LANES + qpb
    q_ref = refs[0]
    k_refs = refs[1:1 + nkb]
    v_refs = refs[1 + nkb:1 + 2 * nkb]
    bias_ref, o_ref = refs[1 + 2 * nkb:]
    j = pl.program_id(1)
    kb = jnp.concatenate([r[...] for r in k_refs], axis=0)
    vb = jnp.concatenate([r[...] for r in v_refs], axis=0)
    lane = lax.broadcasted_iota(I32, (1, LANES), 1)
    lo_half = lane < HEAD_DIM
    zero = jnp.zeros((), BF16)
    for i in range(qpb):
        row = (j * qpb + i) * LANES + lax.broadcasted_iota(I32, (1, BAND_KEYS), 1)
        valid = jnp.where(row >= row_lo, row, row_hi) < row_hi
        qrows = slice(i * LANES, (i + 1) * LANES)
        krows = slice(i * LANES, i * LANES + BAND_KEYS)
        for pr in range(N_HEADS // 2):
            sl = slice(pr * LANES, (pr + 1) * LANES)
            qp = q_ref[qrows, sl]
            lhs = jnp.concatenate([jnp.where(lo_half, qp, zero), jnp.where(lo_half, zero, qp)], axis=0)
            s = lax.dot_general(lhs, kb[krows, sl], (((1,), (1,)), ((), ())),
                                preferred_element_type=F32)
            bias2 = jnp.concatenate([bias_ref[2 * pr], bias_ref[2 * pr + 1]], axis=0)
            s = jnp.where(valid, s + bias2, NEG)
            m = jnp.max(s, axis=-1, keepdims=True)
            pexp = jnp.exp(s - m)
            inv = 1.0 / jnp.sum(pexp, axis=-1, keepdims=True)
            o = jnp.dot(pexp.astype(BF16), vb[krows, sl], preferred_element_type=F32) * inv
            o_ref[qrows, sl] = jnp.where(lo_half, o[:LANES], o[LANES:]).astype(BF16)


def _band(q, kp, vp, bias, *, row_lo, row_hi):
    nb, tq, _ = q.shape
    qpb = 2 if tq % (2 * LANES) == 0 else 1
    nkb = BAND_PAST // LANES + qpb
    kspecs = [pl.BlockSpec((None, LANES, ATT_W), (lambda b, j, c=c: (b, j * qpb + c, 0)))
              for c in range(nkb)]
    kern = functools.partial(_band_kernel, row_lo=row_lo, row_hi=row_hi, qpb=qpb)
    qspec = pl.BlockSpec((None, qpb * LANES, ATT_W), lambda b, j: (b, j, 0))
    return pl.pallas_call(
        kern, out_shape=jax.ShapeDtypeStruct((nb, tq, ATT_W), BF16),
        grid=(nb, tq // (qpb * LANES)),
        in_specs=[qspec] + kspecs + kspecs + [pl.BlockSpec(bias.shape, lambda b, j: (0, 0, 0))],
        out_specs=qspec,
        compiler_params=_cparams(("parallel", "parallel")), name="band",
    )(q, *([kp] * nkb), *([vp] * nkb), bias)


def _merge_kernel(x_ref, oa_ref, ob_ref, gattn_ref, wg_ref, wa_ref, wb_ref, wo_ref,
                  gffn_ref, wpqT_ref, c1_ref, c2_ref,
                  x2_ref, h2T_ref, s1_ref, s2_ref):
    x = x_ref[...]
    ms = jnp.mean(x * x, axis=-1, keepdims=True)
    h = (x * lax.rsqrt(ms + EPS) * gattn_ref[...]).astype(BF16)
    ga = jax.nn.sigmoid(jnp.dot(h, wg_ref[:, :D_MODEL], preferred_element_type=F32))
    gb = jax.nn.sigmoid(jnp.dot(h, wg_ref[:, D_MODEL:], preferred_element_type=F32))
    ma = jnp.dot(oa_ref[...], wa_ref[...], preferred_element_type=F32)
    mb = jnp.dot(ob_ref[...], wb_ref[...], preferred_element_type=F32)
    merged = (ga * ma + gb * mb).astype(BF16)
    x2 = x + jnp.dot(merged, wo_ref[...], preferred_element_type=F32)
    x2_ref[...] = x2
    ms2 = jnp.mean(x2 * x2, axis=-1, keepdims=True)
    h2T = (x2 * lax.rsqrt(ms2 + EPS) * gffn_ref[...]).T.astype(BF16)
    h2T_ref[...] = h2T
    half = PEER_DQ // 2
    for hh in range(PEER_HEADS):
        qT = jnp.dot(wpqT_ref[hh * PEER_DQ:(hh + 1) * PEER_DQ, :], h2T,
                     preferred_element_type=F32).astype(BF16)
        s1_ref[hh] = jnp.dot(c1_ref[...], qT[:half], preferred_element_type=F32)
        s2_ref[hh] = jnp.dot(c2_ref[...], qT[half:], preferred_element_type=F32)


def _merge(x, oa, ob, gattn, wg, wa, wb, wo, gffn, wpqT, c1, c2):
    n = x.shape[0]
    tm = ROW_TILE
    row = lambda i: (i, 0)
    const = lambda i: (0, 0)
    return pl.pallas_call(
        _merge_kernel,
        out_shape=(
            jax.ShapeDtypeStruct((n, D_MODEL), F32),
            jax.ShapeDtypeStruct((D_MODEL, n), BF16),
            jax.ShapeDtypeStruct((PEER_HEADS, N_KEYS, n), F32),
            jax.ShapeDtypeStruct((PEER_HEADS, N_KEYS, n), F32),
        ),
        grid=(n // tm,),
        in_specs=[
            pl.BlockSpec((tm, D_MODEL), row),
            pl.BlockSpec((tm, ATT_W), row),
            pl.BlockSpec((tm, ATT_W), row),
            pl.BlockSpec((1, D_MODEL), const),
            pl.BlockSpec(wg.shape, const),
            pl.BlockSpec(wa.shape, const),
            pl.BlockSpec(wb.shape, const),
            pl.BlockSpec(wo.shape, const),
            pl.BlockSpec((1, D_MODEL), const),
            pl.BlockSpec(wpqT.shape, const),
            pl.BlockSpec(c1.shape, const),
            pl.BlockSpec(c2.shape, const),
        ],
        out_specs=(
            pl.BlockSpec((tm, D_MODEL), row),
            pl.BlockSpec((D_MODEL, tm), lambda i: (0, i)),
            pl.BlockSpec((PEER_HEADS, N_KEYS, tm), lambda i: (0, 0, i)),
            pl.BlockSpec((PEER_HEADS, N_KEYS, tm), lambda i: (0, 0, i)),
        ),
        compiler_params=_cparams(("parallel",)), name="merge",
    )(x, oa, ob, gattn, wg, wa, wb, wo, gffn, wpqT, c1, c2)


def _pack_rows(x):
    return pltpu.bitcast(x, jnp.uint32)


def _unpack_rows(x):
    return pltpu.bitcast(x, BF16)


def _top16(x, distinct):
    idx = lax.broadcasted_iota(I32, x.shape, 0)
    rank = jnp.full(x.shape, PEER_TOPK, I32)
    rows = []
    for r in range(PEER_TOPK):
        m = jnp.max(x, axis=0, keepdims=True)
        rows.append(m)
        hit = x == m
        if not distinct:
            hit = idx == jnp.min(jnp.where(hit, idx, N_KEYS), axis=0, keepdims=True)
        rank = jnp.where(hit, r, rank)
        x = jnp.where(hit, -jnp.inf, x)
    return jnp.concatenate(rows, axis=0), rank


def _select_kernel(s1_ref, s2_ref, cnt_ref, e1_ref, r2_ref, e2_ref):
    sub8 = lax.broadcasted_iota(I32, (8, 1), 0).astype(F32)
    sub16 = lax.broadcasted_iota(I32, (PEER_TOPK, 1), 0).astype(F32)

    def count(mask):
        return jnp.sum(jnp.where(mask, 1.0, 0.0), axis=0, keepdims=True)

    def head_body(hh, carry):
        s1 = s1_ref[hh]
        s2 = s2_ref[hh]
        v1, r1 = _top16(s1, True)
        v2, r2 = _top16(s2, True)
        ranked = jnp.maximum(count(r1 < PEER_TOPK), count(r2 < PEER_TOPK))
        v1, r1, v2, r2 = lax.cond(
            jnp.max(ranked) > PEER_TOPK,
            lambda: _top16(s1, False) + _top16(s2, False),
            lambda: (v1, r1, v2, r2))
        pieces = [v1[0:1] + v2]
        for a in range(1, 8):
            piece = v1[a:a + 1] + v2[0:8]
            pieces.append(jnp.where(sub8 < PEER_TOPK // (a + 1), piece, -jnp.inf))
        pieces.append(v1[8:16] + v2[0:1])
        cand = jnp.concatenate(pieces, axis=0)
        rem = jnp.full((1, cand.shape[1]), float(PEER_TOPK), F32)
        tau = jnp.zeros((1, cand.shape[1]), F32)
        x = cand
        for _ in range(PEER_TOPK):
            m = jnp.max(x, axis=0, keepdims=True)
            hit = x == m
            tau = jnp.where(rem > 0, m, tau)
            rem = rem - count(hit)
            x = jnp.where(hit, -jnp.inf, x)
        need = PEER_TOPK - count(cand > tau)
        kept = []
        for a in range(PEER_TOPK):
            row = pieces[a] if a < 8 else pieces[8][a - 8:a - 7]
            ties = count(row == tau)
            kept.append(count(row > tau) + jnp.minimum(ties, jnp.maximum(need, 0.0)))
            need = need - ties
        smax = v1[0:1] + v2[0:1]
        z = jnp.sum(jnp.where(sub16 < kept[0], jnp.exp(pieces[0] - smax), 0.0), axis=0, keepdims=True)
        for a in range(1, 8):
            z = z + jnp.sum(jnp.where(sub8 < kept[a], jnp.exp(pieces[a] - smax), 0.0),
                            axis=0, keepdims=True)
        tail = jnp.concatenate(kept[8:], axis=0)
        z = z + jnp.sum(jnp.where(tail > 0, jnp.exp(pieces[8] - smax), 0.0), axis=0, keepdims=True)
        cnt = jnp.zeros(s1.shape, F32)
        for a in range(PEER_TOPK):
            cnt = jnp.where(r1 == a, kept[a], cnt)
        cnt_ref[hh] = cnt
        e1_ref[hh] = jnp.exp(s1 - v1[0:1]) * (1.0 / z)
        r2b = r2.astype(F32).astype(BF16)
        e2b = jnp.exp(s2 - v2[0:1]).astype(BF16)
        for c in range(s2.shape[1] // LANES):
            cs = slice(c * LANES, (c + 1) * LANES)
            r2_ref[hh, c] = _pack_rows(r2b[:, cs])
            e2_ref[hh, c] = _pack_rows(e2b[:, cs])
        return carry

    lax.fori_loop(0, PEER_HEADS, head_body, 0)


def _select(s1, s2):
    n = s1.shape[-1]
    tn = SEL_TILE
    blk = pl.BlockSpec((PEER_HEADS, N_KEYS, tn), lambda i: (0, 0, i))
    packed = jax.ShapeDtypeStruct((PEER_HEADS, n // LANES, N_KEYS // 2, LANES), jnp.uint32)
    pblk = pl.BlockSpec((PEER_HEADS, tn // LANES, N_KEYS // 2, LANES), lambda i: (0, i, 0, 0))
    return pl.pallas_call(
        _select_kernel,
        out_shape=(
            jax.ShapeDtypeStruct(s1.shape, F32),
            jax.ShapeDtypeStruct(s1.shape, F32),
            packed, packed,
        ),
        grid=(n // tn,),
        in_specs=[blk, blk],
        out_specs=(blk, blk, pblk, pblk),
        compiler_params=_cparams(("parallel",)), name="select",
    )(s1, s2)


def _peer_kernel(h2T_ref, u_ref, vT_ref, vTp_ref, cnt_ref, e1_ref, r2_ref, e2_ref, x2_ref,
                 y_ref, acc_ref, wa0_ref, wa1_ref, z0_ref, z1_ref, rowc_ref, rowe_ref):
    e = pl.program_id(1)
    zero = jnp.minimum(e, 0)
    z_refs = (z0_ref, z1_ref)
    wa_refs = (wa0_ref, wa1_ref)

    @pl.when(e == 0)
    def _():
        acc_ref[...] = jnp.zeros_like(acc_ref)
        wa_refs[(PEER_EB // PEER_GRP - 1) % 2][...] = jnp.zeros_like(wa1_ref)

    tn = h2T_ref.shape[1]
    nsub = PEER_EB // N_KEYS
    reps = N_KEYS // BF16_ROWS
    for sub in range(nsub):
        for hh in range(PEER_HEADS):
            rowc_ref[sub, hh] = _pack_rows(
                jnp.broadcast_to(cnt_ref[hh, sub:sub + 1, :], (BF16_ROWS, tn)).astype(BF16))
            rowe_ref[sub, hh] = _pack_rows(
                jnp.broadcast_to(e1_ref[hh, sub:sub + 1, :], (BF16_ROWS, tn)).astype(BF16))

    ngrp = PEER_EB // PEER_GRP

    def scores(g, slot):
        start = g * PEER_GRP if isinstance(g, int) else pl.multiple_of(g * PEER_GRP, PEER_GRP)
        z_refs[slot][0] = jnp.dot(u_ref[pl.ds(start, PEER_GRP), :], h2T_ref[...],
                                  preferred_element_type=F32)

    def weights(g, slot):
        for s in range(PEER_GRP // N_KEYS):
            sub = g * (PEER_GRP // N_KEYS) + s
            for c in range(tn // LANES):
                cs = slice(c * LANES, (c + 1) * LANES)
                zc = z_refs[slot][zero, s * N_KEYS:(s + 1) * N_KEYS, cs]
                a = (0.5 * zc * (1.0 + lax.erf(zc * (2.0 ** -0.5)))).astype(BF16)
                w = jnp.zeros((N_KEYS, LANES), BF16)
                for hh in range(PEER_HEADS):
                    cnt = jnp.concatenate([_unpack_rows(rowc_ref[sub + zero, hh, :, cs])] * reps, axis=0)
                    e1 = jnp.concatenate([_unpack_rows(rowe_ref[sub + zero, hh, :, cs])] * reps, axis=0)
                    w = w + jnp.where(_unpack_rows(r2_ref[hh, c]) < cnt,
                                      _unpack_rows(e2_ref[hh, c]) * e1, jnp.zeros((), BF16))
                wa_refs[slot][0, s * (N_KEYS // 2):(s + 1) * (N_KEYS // 2), cs] = _pack_rows(w * a)

    def down(vt, slot):
        acc_ref[...] += jnp.dot(vt, _unpack_rows(wa_refs[slot][zero]), preferred_element_type=F32)

    last = (ngrp - 1) % 2
    scores(0, 0)
    down(vTp_ref[0], last)
    for g in range(ngrp):
        if g + 1 < ngrp:
            scores(g + 1, (g + 1) % 2)
        weights(g, g % 2)
        if g > 0:
            down(vT_ref[g - 1], (g - 1) % 2)

    @pl.when(e == pl.num_programs(1) - 1)
    def _():
        down(vT_ref[ngrp - 1], last)
        y_ref[...] = x2_ref[...] + acc_ref[...].T


def _peer(h2T, u, vT, cnt, e1, r2, e2, x2):
    n = x2.shape[0]
    tn = min(PEER_TOK, n)
    nsub = PEER_EB // N_KEYS
    tok3 = pl.BlockSpec((PEER_HEADS, tn // LANES, N_KEYS // 2, LANES), lambda i, e: (0, i, 0, 0))
    key1 = pl.BlockSpec((PEER_HEADS, nsub, tn), lambda i, e: (0, e, i))
    return pl.pallas_call(
        _peer_kernel,
        out_shape=jax.ShapeDtypeStruct((n, D_MODEL), F32),
        grid=(n // tn, N_EXPERTS // PEER_EB),
        in_specs=[
            pl.BlockSpec((D_MODEL, tn), lambda i, e: (0, i)),
            pl.BlockSpec((PEER_EB, D_MODEL), lambda i, e: (e, 0)),
            pl.BlockSpec((PEER_EB // PEER_GRP, D_MODEL, PEER_GRP), lambda i, e: (e, 0, 0)),
            pl.BlockSpec((1, D_MODEL, PEER_GRP),
                         lambda i, e: (jnp.maximum(e * (PEER_EB // PEER_GRP) - 1, 0), 0, 0)),
            key1, key1, tok3, tok3,
            pl.BlockSpec((tn, D_MODEL), lambda i, e: (i, 0)),
        ],
        out_specs=pl.BlockSpec((tn, D_MODEL), lambda i, e: (i, 0)),
        scratch_shapes=[
            pltpu.VMEM((D_MODEL, tn), F32),
            pltpu.VMEM((1, PEER_GRP // 2, tn), jnp.uint32),
            pltpu.VMEM((1, PEER_GRP // 2, tn), jnp.uint32),
            pltpu.VMEM((1, PEER_GRP, tn), F32),
            pltpu.VMEM((1, PEER_GRP, tn), F32),
            pltpu.VMEM((nsub, PEER_HEADS, BF16_ROWS // 2, tn), jnp.uint32),
            pltpu.VMEM((nsub, PEER_HEADS, BF16_ROWS // 2, tn), jnp.uint32),
        ],
        compiler_params=_cparams(("parallel", "arbitrary")), name="peer",
    )(h2T, u, vT, vT, cnt, e1, r2, e2, x2)


def _rope_tables(pos):
    posf = pos.astype(F32)[:, None]
    t = pos.shape[0]

    def tab(half, reps):
        inv = jnp.power(ROPE_THETA, -jnp.arange(half, dtype=F32) / half)
        ang = posf * inv[None, :]
        cos = jnp.cos(ang)
        sin = jnp.sin(ang)
        return (jnp.tile(jnp.concatenate([cos, cos], axis=1), (1, reps)),
                jnp.tile(jnp.concatenate([-sin, sin], axis=1), (1, reps)))

    cosq, sinq = tab(HEAD_DIM // 2, N_HEADS)
    cosi, sini = tab(IDX_DIM // 2, IDX_HEADS + 1)
    pad = IDX_W - cosi.shape[1]
    cosi = jnp.concatenate([cosi, jnp.ones((t, pad), F32)], axis=1)
    sini = jnp.concatenate([sini, jnp.zeros((t, pad), F32)], axis=1)
    return cosq, sinq, cosi, sini


def _band_bias(rel_bias):
    qi = np.arange(LANES)[:, None]
    kj = np.arange(BAND_KEYS)[None, :]
    ok = (kj // CHUNK >= qi // CHUNK) & (kj // CHUNK <= qi // CHUNK + BAND_PAST // CHUNK)
    n = BAND_KEYS + LANES - 1
    m = np.arange(n)
    diag = rel_bias.astype(F32)[:, np.clip(BAND_PAST + LANES - 1 - m, -REL_CLIP, REL_CLIP) + REL_CLIP]
    wrapped = jnp.tile(jnp.pad(diag, ((0, 0), (0, 1))), (1, LANES))[:, :LANES * n]
    toep = wrapped.reshape(-1, LANES, n)[:, :, LANES - 1:LANES - 1 + BAND_KEYS]
    return jnp.where(jnp.asarray(ok)[None], toep, NEG)


def _layer_weights(g_attn, w_in, g_qa, g_ka, g_ik, g_qb, g_kb, rel_bias, w_br_a, w_br_b,
                   w_out, g_ffn, w_pq, c1, c2, u, v):
    a0 = 3 * ATT_W
    i0 = a0 + IDX_HEADS * IDX_DIM + IDX_DIM + IDX_HEADS
    b0 = i0 + 3 * ATT_W
    idx_cols = i0 - a0
    w1 = jnp.concatenate(
        [w_in[:, :a0], w_in[:, i0:b0], w_in[:, a0:i0],
         jnp.zeros((D_MODEL, IDX_W - idx_cols), w_in.dtype)], axis=1).astype(BF16)
    wg = w_in[:, b0:].astype(BF16)
    hd = jnp.asarray(np.kron(np.eye(N_HEADS), np.ones((HEAD_DIM, HEAD_DIM))) / HEAD_DIM, BF16)
    gh = jnp.stack([jnp.tile(g, N_HEADS) for g in (g_qa, g_ka, g_qb, g_kb)]).astype(F32)
    gidx = jnp.concatenate(
        [g_ik.astype(F32), jnp.full((IDX_HEADS,), IDX_HEADS ** -0.5, F32),
         jnp.ones((LANES - IDX_DIM - IDX_HEADS,), F32)])[None, :]
    tri = jnp.asarray(np.tril(np.ones((KEY_TILE, KEY_TILE)), -1), BF16)
    return dict(
        w1=w1, wg=wg, hd=hd, gh=gh, gidx=gidx, tri=tri,
        gattn=g_attn.astype(F32)[None, :], gffn=g_ffn.astype(F32)[None, :],
        bias=_band_bias(rel_bias),
        wa=w_br_a.astype(BF16), wb=w_br_b.astype(BF16), wo=w_out.astype(BF16),
        wpqT=w_pq.T.astype(BF16), c1=c1.astype(BF16), c2=c2.astype(BF16),
        u=u.astype(BF16),
        vT=v.astype(BF16).reshape(N_EXPERTS // PEER_GRP, PEER_GRP, D_MODEL).transpose(0, 2, 1))


def _ffn(x, oa, ob, w):
    x2, h2T, s1, s2 = _merge(x, oa, ob, w["gattn"], w["wg"], w["wa"], w["wb"], w["wo"],
                             w["gffn"], w["wpqT"], w["c1"], w["c2"])
    cnt, e1, r2, e2 = _select(s1, s2)
    return _peer(h2T, w["u"], w["vT"], cnt, e1, r2, e2, x2)


def _prompt_layer(x, w):
    b, t, _ = x.shape
    xf = x.reshape(b * t, D_MODEL)
    tabs = _rope_tables(jnp.arange(t))
    (aqT, ak, akb, av, avT, iqT, ik, ikb, iwT, bq, bk, bkb, bv, bvb) = _proj(
        xf, b, t, w["w1"], w["gattn"], w["hd"], w["gh"], w["gidx"], *tabs)
    oa = _dsa(aqT, iqT, iwT, akb.reshape(b, t, ATT_W), avT, ikb.reshape(b, t, LANES),
              w["tri"], q0=0, n_valid=t)
    front = ((0, 0), (BAND_PAST, 0), (0, 0))
    ob = _band(bq.reshape(b, t, ATT_W), jnp.pad(bkb.reshape(b, t, ATT_W), front),
               jnp.pad(bvb.reshape(b, t, ATT_W), front), w["bias"],
               row_lo=BAND_PAST, row_hi=BAND_PAST + t)
    y = _ffn(xf, oa.reshape(b * t, ATT_W), ob.reshape(b * t, ATT_W), w)
    keep = min(BAND_PAST, t)
    heads = lambda a: a.reshape(b, t, N_HEADS, HEAD_DIM)
    return y.reshape(b, t, D_MODEL), (
        heads(ak), heads(av), ik.reshape(b, t, IDX_DIM),
        heads(bk)[:, t - keep:], heads(bv)[:, t - keep:])


def _sample_layer(x, ca_k, ca_v, ca_ik, cb_k, cb_v, w):
    b, t, _ = x.shape
    n = b * t
    past = ca_k.shape[1]
    xf = x.reshape(n, D_MODEL)
    tabs = _rope_tables(jnp.tile(past + jnp.arange(t), b))
    (aqT, ak, akb, av, avT, iqT, ik, ikb, iwT, bq, bk, bkb, bv, bvb) = _proj(
        xf, 1, n, w["w1"], w["gattn"], w["hd"], w["gh"], w["gidx"], *tabs)

    def per_seq_T(aT, rows):
        a = aT[0].T.reshape(b, t, rows)
        return jnp.pad(a, ((0, 0), (0, LANES - t), (0, 0))).transpose(0, 2, 1)

    lk = past + t
    lp = -(-lk // KEY_TILE) * KEY_TILE
    padk = ((0, 0), (0, lp - lk), (0, 0))
    k_all = jnp.pad(jnp.concatenate(
        [ca_k.reshape(b, past, ATT_W).astype(BF16), akb.reshape(b, t, ATT_W)], axis=1), padk)
    v_all = jnp.pad(jnp.concatenate(
        [ca_v.reshape(b, past, ATT_W).astype(BF16), av.astype(BF16).reshape(b, t, ATT_W)],
        axis=1), padk)
    vT_all = v_all.reshape(b, lp // KEY_TILE, KEY_TILE, ATT_W).transpose(0, 1, 3, 2)
    ik_all = jnp.pad(jnp.concatenate(
        [jnp.pad(ca_ik.astype(BF16), ((0, 0), (0, 0), (0, LANES - IDX_DIM))),
         ikb.reshape(b, t, LANES)], axis=1), padk)
    oa = _dsa(per_seq_T(aqT, ATT_W), per_seq_T(iqT, 2 * LANES), per_seq_T(iwT, IDX_HEADS),
              k_all, vT_all, ik_all, w["tri"], q0=past, n_valid=lk)[:, :t]

    pb = cb_k.shape[1]
    padb = ((0, 0), (BAND_PAST - pb, LANES - t), (0, 0))
    qpad = jnp.pad(bq.reshape(b, t, ATT_W), ((0, 0), (0, LANES - t), (0, 0)))
    kb_all = jnp.pad(jnp.concatenate(
        [cb_k.reshape(b, pb, ATT_W).astype(BF16), bkb.reshape(b, t, ATT_W)], axis=1), padb)
    vb_all = jnp.pad(jnp.concatenate(
        [cb_v.reshape(b, pb, ATT_W).astype(BF16), bvb.reshape(b, t, ATT_W)], axis=1), padb)
    ob = _band(qpad, kb_all, vb_all, w["bias"],
               row_lo=BAND_PAST - pb, row_hi=BAND_PAST + t)[:, :t]

    y = _ffn(xf, oa.reshape(n, ATT_W), ob.reshape(n, ATT_W), w)
    keep = min(BAND_PAST, t)
    heads = lambda a: a.reshape(b, t, N_HEADS, HEAD_DIM)
    return y.reshape(b, t, D_MODEL), (
        heads(ak), heads(av), ik.reshape(b, t, IDX_DIM),
        heads(bk)[:, t - keep:], heads(bv)[:, t - keep:])


def kernel(x_prompt, x_sample, cache_a_k, cache_a_v, cache_a_ik, cache_b_k, cache_b_v,
           g_attn, w_in, g_qa, g_ka, g_ik, g_qb, g_kb, rel_bias, w_br_a, w_br_b, w_out,
           g_ffn, w_pq, peer_c1, peer_c2, peer_u, peer_v):
    depth = w_in.shape[0]
    xp, xs = x_prompt, x_sample
    sp, ss = [], []
    for l in range(depth):
        w = _layer_weights(g_attn[l], w_in[l], g_qa[l], g_ka[l], g_ik[l], g_qb[l], g_kb[l],
                           rel_bias[l], w_br_a[l], w_br_b[l], w_out[l], g_ffn[l], w_pq[l],
                           peer_c1[l], peer_c2[l], peer_u[l], peer_v[l])
        xp, st_p = _prompt_layer(xp, w)
        xs, st_s = _sample_layer(xs, cache_a_k[l], cache_a_v[l], cache_a_ik[l],
                                 cache_b_k[l], cache_b_v[l], w)
        sp.append(st_p)
        ss.append(st_s)
    stack = lambda sts, i: jnp.stack([s[i] for s in sts])
    return (xp, xs) + tuple(stack(sp, i) for i in range(5)) + tuple(stack(ss, i) for i in range(5))
```

```python
import functools

import jax
import jax.numpy as jnp
import numpy as np
from jax import lax
from jax.experimental import pallas as pl
from jax.experimental.pallas import tpu as pltpu

F32 = jnp.float32
BF16 = jnp.bfloat16
I32 = jnp.int32

D_MODEL = 1024
HEAD_DIM = 64
N_HEADS = 8
ATT_W = N_HEADS * HEAD_DIM
IDX_HEADS = 8
IDX_DIM = 32
IDX_W = 384
CHUNK = 64
TOPK_MAX = 256
BAND_PAST = 512
BAND_KEYS = BAND_PAST + 128
REL_CLIP = 128
PEER_HEADS = 8
PEER_DQ = 256
N_KEYS = 128
N_EXPERTS = N_KEYS * N_KEYS
PEER_TOPK = 16
ROPE_THETA = 10000.0
EPS = 1e-6
NEG = -1e30
INT_MIN = -(2 ** 31)

LANES = 128
BF16_ROWS = 16
KEY_TILE = 256
ROW_TILE = 512
SEL_TILE = 256
PEER_TOK = 1024
PEER_EB = 1024
PEER_GRP = 256
VMEM_LIMIT = 56 * 1024 * 1024


def _cparams(sem):
    return pltpu.CompilerParams(dimension_semantics=sem, vmem_limit_bytes=VMEM_LIMIT)


def _rope_chunks(y, cos, sin, half):
    lane = lax.broadcasted_iota(I32, (1, LANES), 1)
    lo = (lane % (2 * half)) < half
    out = []
    for c in range(y.shape[1] // LANES):
        sl = slice(c * LANES, (c + 1) * LANES)
        yc = y[:, sl]
        partner = jnp.where(lo, pltpu.roll(yc, LANES - half, 1), pltpu.roll(yc, half, 1))
        out.append(yc * cos[:, sl] + partner * sin[:, sl])
    return jnp.concatenate(out, axis=1)


def _proj_kernel(x_ref, gattn_ref, w1_ref, hd_ref, gh_ref, gidx_ref,
                 cosq_ref, sinq_ref, cosi_ref, sini_ref,
                 aqT_ref, ak_ref, akb_ref, av_ref, avT_ref, iqT_ref, ik_ref, ikb_ref,
                 iwT_ref, bq_ref, bk_ref, bkb_ref, bv_ref, bvb_ref):
    x = x_ref[...]
    ms = jnp.mean(x * x, axis=-1, keepdims=True)
    h = (x * lax.rsqrt(ms + EPS) * gattn_ref[...]).astype(BF16)

    def seg(i):
        return jnp.dot(h, w1_ref[:, i * ATT_W:(i + 1) * ATT_W], preferred_element_type=F32)

    def headnorm(y, gi):
        hms = jnp.dot((y * y).astype(BF16), hd_ref[...], preferred_element_type=F32)
        return y * lax.rsqrt(hms + EPS) * gh_ref[gi:gi + 1, :]

    cosq = cosq_ref[...]
    sinq = sinq_ref[...]

    aq = _rope_chunks(headnorm(seg(0), 0), cosq, sinq, HEAD_DIM // 2) * (HEAD_DIM ** -0.5)
    aqT_ref[...] = aq.T.astype(BF16)

    ak = _rope_chunks(headnorm(seg(1), 1), cosq, sinq, HEAD_DIM // 2)
    ak_ref[...] = ak
    akb_ref[...] = ak.astype(BF16)

    av = seg(2)
    av_ref[...] = av
    for c in range(av.shape[0] // KEY_TILE):
        avT_ref[c] = av[c * KEY_TILE:(c + 1) * KEY_TILE, :].T.astype(BF16)

    bq_ref[...] = (headnorm(seg(3), 2) * (HEAD_DIM ** -0.5)).astype(BF16)
    bk = headnorm(seg(4), 3)
    bk_ref[...] = bk
    bkb_ref[...] = bk.astype(BF16)
    bv = seg(5)
    bv_ref[...] = bv
    bvb_ref[...] = bv.astype(BF16)

    s = jnp.dot(h, w1_ref[:, 6 * ATT_W:6 * ATT_W + IDX_W], preferred_element_type=F32)
    lane = lax.broadcasted_iota(I32, (1, LANES), 1)
    is_ik = lane < IDX_DIM
    c2 = s[:, 2 * LANES:3 * LANES]
    ikms = jnp.sum(jnp.where(is_ik, c2 * c2, 0.0), axis=-1, keepdims=True) * (1.0 / IDX_DIM)
    c2n = c2 * jnp.where(is_ik, lax.rsqrt(ikms + EPS), 1.0) * gidx_ref[...]
    sn = jnp.concatenate([s[:, :2 * LANES], c2n], axis=1)
    r = _rope_chunks(sn, cosi_ref[...], sini_ref[...], IDX_DIM // 2)
    iqT_ref[...] = r[:, :2 * LANES].T.astype(BF16)
    c2r = r[:, 2 * LANES:]
    ik_ref[...] = c2r[:, :IDX_DIM]
    ikb_ref[...] = jnp.where(is_ik, c2r, 0.0).astype(BF16)
    iwT_ref[...] = c2r.T[IDX_DIM:IDX_DIM + IDX_HEADS, :]


def _proj(x, nb, tb, w1, gattn, hd, gh, gidx, cosq, sinq, cosi, sini):
    n = x.shape[0]
    tm = ROW_TILE
    tpb = tb // tm
    row = lambda i: (i, 0)
    tab = lambda i: (i % tpb, 0)
    const = lambda i: (0, 0)
    grp = lambda i: (i // tpb, 0, i % tpb)
    out_shape = (
        jax.ShapeDtypeStruct((nb, ATT_W, tb), BF16),
        jax.ShapeDtypeStruct((n, ATT_W), F32),
        jax.ShapeDtypeStruct((n, ATT_W), BF16),
        jax.ShapeDtypeStruct((n, ATT_W), F32),
        jax.ShapeDtypeStruct((nb, tb // KEY_TILE, ATT_W, KEY_TILE), BF16),
        jax.ShapeDtypeStruct((nb, 2 * LANES, tb), BF16),
        jax.ShapeDtypeStruct((n, IDX_DIM), F32),
        jax.ShapeDtypeStruct((n, LANES), BF16),
        jax.ShapeDtypeStruct((nb, IDX_HEADS, tb), F32),
        jax.ShapeDtypeStruct((n, ATT_W), BF16),
        jax.ShapeDtypeStruct((n, ATT_W), F32),
        jax.ShapeDtypeStruct((n, ATT_W), BF16),
        jax.ShapeDtypeStruct((n, ATT_W), F32),
        jax.ShapeDtypeStruct((n, ATT_W), BF16),
    )
    out_specs = (
        pl.BlockSpec((None, ATT_W, tm), grp),
        pl.BlockSpec((tm, ATT_W), row),
        pl.BlockSpec((tm, ATT_W), row),
        pl.BlockSpec((tm, ATT_W), row),
        pl.BlockSpec((None, tm // KEY_TILE, ATT_W, KEY_TILE),
                     lambda i: (i // tpb, i % tpb, 0, 0)),
        pl.BlockSpec((None, 2 * LANES, tm), grp),
        pl.BlockSpec((tm, IDX_DIM), row),
        pl.BlockSpec((tm, LANES), row),
        pl.BlockSpec((None, IDX_HEADS, tm), grp),
        pl.BlockSpec((tm, ATT_W), row),
        pl.BlockSpec((tm, ATT_W), row),
        pl.BlockSpec((tm, ATT_W), row),
        pl.BlockSpec((tm, ATT_W), row),
        pl.BlockSpec((tm, ATT_W), row),
    )
    in_specs = [
        pl.BlockSpec((tm, D_MODEL), row),
        pl.BlockSpec((1, D_MODEL), const),
        pl.BlockSpec(w1.shape, const),
        pl.BlockSpec(hd.shape, const),
        pl.BlockSpec(gh.shape, const),
        pl.BlockSpec(gidx.shape, const),
        pl.BlockSpec((tm, ATT_W), tab),
        pl.BlockSpec((tm, ATT_W), tab),
        pl.BlockSpec((tm, IDX_W), tab),
        pl.BlockSpec((tm, IDX_W), tab),
    ]
    return pl.pallas_call(
        _proj_kernel, out_shape=out_shape, grid=(n // tm,),
        in_specs=in_specs, out_specs=out_specs,
        compiler_params=_cparams(("parallel",)), name="proj",
    )(x, gattn, w1, hd, gh, gidx, cosq, sinq, cosi, sini)


def _dsa_kernel(qT_ref, iqT_ref, iwT_ref, k_ref, vT_ref, ikb_ref, tri_ref,
                o_ref, sc_ref, am_ref, s_ref, oT_ref, acc_ref, *, q0, n_valid, n_key_tiles, n_sel):
    j = pl.program_id(1)
    qbase = q0 + j * LANES
    n_t = jnp.minimum(lax.shift_right_logical(qbase + LANES + KEY_TILE - 1, 8), n_key_tiles)
    qchunk = lax.shift_right_logical(
        qbase + lax.broadcasted_iota(I32, (1, LANES), 1), 6)

    def rows(t):
        return pl.ds(pl.multiple_of(t * KEY_TILE, KEY_TILE), KEY_TILE)

    def admissible(t):
        kidx = t * KEY_TILE + lax.broadcasted_iota(I32, (KEY_TILE, LANES), 0)
        return jnp.where(kidx < n_valid, lax.shift_right_logical(kidx, 6), 1 << 30) <= qchunk

    def fold8(v):
        return v.reshape(KEY_TILE // 8, 8, v.shape[-1])

    zpad = jnp.zeros((LANES - IDX_DIM, LANES), BF16)
    rhs_idx = jnp.concatenate(
        [jnp.concatenate([iqT_ref[h * IDX_DIM:(h + 1) * IDX_DIM, :], zpad], axis=0)
         for h in range(IDX_HEADS)], axis=1)

    def score_body(t, carry, width):
        both = pl.ds(pl.multiple_of(t * KEY_TILE, KEY_TILE), width * KEY_TILE)
        res = jnp.dot(ikb_ref[both, :], rhs_idx, preferred_element_type=F32)
        for w in range(width):
            part = res[w * KEY_TILE:(w + 1) * KEY_TILE]
            acc = jnp.zeros((KEY_TILE, LANES), F32)
            for h in range(IDX_HEADS):
                acc = acc + jnp.maximum(part[:, h * LANES:(h + 1) * LANES], 0.0) * iwT_ref[h:h + 1, :]
            sc_ref[rows(t + w), :] = jnp.where(admissible(t + w), acc, NEG)
        return carry

    n_pairs = lax.shift_right_logical(n_t, 1)
    lax.fori_loop(0, n_pairs, lambda i, c: score_body(2 * i, c, 2), 0)
    lax.fori_loop(2 * n_pairs, n_t, lambda t, c: score_body(t, c, 1), 0)
    sc_ref[rows(n_t), :] = jnp.full((KEY_TILE, LANES), -jnp.inf, F32)

    def count(pred):
        def body(t2, c8):
            two = pl.ds(pl.multiple_of(t2 * (2 * KEY_TILE), 2 * KEY_TILE), 2 * KEY_TILE)
            hits = jnp.where(pred(sc_ref[two, :]), 1, 0)
            return c8 + jnp.sum(hits.reshape(2 * KEY_TILE // 8, 8, LANES), axis=0)
        c8 = lax.fori_loop(0, lax.shift_right_logical(n_t + 1, 1), body, jnp.zeros((8, LANES), I32))
        return jnp.sum(c8, axis=0, keepdims=True)

    def bisect_body(it, p):
        c = p + lax.shift_left(jnp.int32(1), 31 - it)
        cf = pltpu.bitcast(jnp.where(c >= 0, c, c ^ 0x7FFFFFFF), F32)
        return jnp.where(count(lambda v: v >= cf) >= n_sel, c, p)

    p = lax.fori_loop(0, 32, bisect_body, jnp.full((1, LANES), INT_MIN, I32))
    thr = pltpu.bitcast(jnp.where(p >= 0, p, p ^ 0x7FFFFFFF), F32)

    def mask_simple():
        def body(t, carry):
            keep = jnp.where(sc_ref[rows(t), :] >= thr, 0.0, NEG)
            am_ref[rows(t), :] = jnp.where(admissible(t), keep, NEG)
            return carry
        lax.fori_loop(0, n_t, body, 0)

    def mask_ties():
        need = (n_sel - count(lambda v: v > thr)).astype(F32)

        def body(t, carry):
            v = sc_ref[rows(t), :]
            eq = v == thr
            eqf = jnp.where(eq, 1.0, 0.0)
            pre = jnp.dot(tri_ref[...], eqf.astype(BF16), preferred_element_type=F32) + carry
            tie = jnp.where(eq, jnp.where(pre < need, 0.0, NEG), NEG)
            keep = jnp.where(v > thr, 0.0, tie)
            am_ref[rows(t), :] = jnp.where(admissible(t), keep, NEG)
            return carry + jnp.sum(jnp.sum(fold8(eqf), axis=0), axis=0, keepdims=True)
        lax.fori_loop(0, n_t, body, jnp.zeros((1, LANES), F32))

    lax.cond(jnp.max(count(lambda v: v >= thr)) == n_sel, mask_simple, mask_ties)

    npair = N_HEADS // 2
    pw = 2 * LANES
    zq = jnp.zeros((HEAD_DIM, LANES), BF16)
    rhs = [jnp.concatenate(
        [jnp.concatenate([qT_ref[pr * LANES:pr * LANES + HEAD_DIM, :], zq], axis=1),
         jnp.concatenate([zq, qT_ref[pr * LANES + HEAD_DIM:(pr + 1) * LANES, :]], axis=1)], axis=0)
        for pr in range(npair)]

    n_two = lax.shift_right_logical(n_t, 1)

    def span(t, width):
        return pl.ds(pl.multiple_of(t * KEY_TILE, KEY_TILE), width * KEY_TILE)

    def logit_body(t, m8, width):
        am = am_ref[span(t, width), :]
        am2 = jnp.concatenate([am, am], axis=1)
        tops = []
        for pr in range(npair):
            s = jnp.dot(k_ref[span(t, width), pr * LANES:(pr + 1) * LANES], rhs[pr],
                        preferred_element_type=F32) + am2
            s_ref[span(t, width), pr * pw:(pr + 1) * pw] = s
            tops.append(jnp.max(s.reshape(width * KEY_TILE // 8, 8, pw), axis=0))
        return jnp.maximum(m8, jnp.concatenate(tops, axis=1))

    m8 = lax.fori_loop(0, n_two, lambda i, c: logit_body(2 * i, c, 2),
                       jnp.full((8, npair * pw), NEG, F32))
    m8 = lax.fori_loop(2 * n_two, n_t, lambda t, c: logit_body(t, c, 1), m8)
    m = jnp.max(m8, axis=0, keepdims=True)
    acc_ref[...] = jnp.zeros_like(acc_ref)

    def pv_body(t, l8, width):
        sums = []
        for pr in range(npair):
            cols = slice(pr * pw, (pr + 1) * pw)
            pexp = jnp.exp(s_ref[span(t, width), cols] - m[:, cols])
            sums.append(jnp.sum(pexp.reshape(width * KEY_TILE // 8, 8, pw), axis=0))
            pb = pexp.astype(BF16)
            out = acc_ref[pr]
            for w in range(width):
                out = out + jnp.dot(vT_ref[t + w, pr * LANES:(pr + 1) * LANES, :],
                                    pb[w * KEY_TILE:(w + 1) * KEY_TILE],
                                    preferred_element_type=F32)
            acc_ref[pr] = out
        return l8 + jnp.concatenate(sums, axis=1)

    l8 = lax.fori_loop(0, n_two, lambda i, c: pv_body(2 * i, c, 2),
                       jnp.zeros((8, npair * pw), F32))
    l8 = lax.fori_loop(2 * n_two, n_t, lambda t, c: pv_body(t, c, 1), l8)
    inv = 1.0 / jnp.sum(l8, axis=0, keepdims=True)
    for pr in range(npair):
        lo = pr * LANES
        acc = acc_ref[pr]
        oT_ref[lo:lo + HEAD_DIM, :] = acc[:HEAD_DIM, :LANES] * inv[:, pr * pw:pr * pw + LANES]
        oT_ref[lo + HEAD_DIM:lo + LANES, :] = acc[HEAD_DIM:, LANES:] * inv[:, pr * pw + LANES:(pr + 1) * pw]

    o_ref[...] = oT_ref[...].T.astype(BF16)


def _dsa(qT, iqT, iwT, k, vT, ikb, tri, *, q0, n_valid):
    nb, _, tq = qT.shape
    lp = k.shape[1]
    qblk = lambda b, j: (b, 0, j)
    seq3 = lambda b, j: (b, 0, 0)
    seq4 = lambda b, j: (b, 0, 0, 0)
    kern = functools.partial(_dsa_kernel, q0=q0, n_valid=n_valid, n_key_tiles=lp // KEY_TILE,
                             n_sel=min(TOPK_MAX, n_valid // 4))
    return pl.pallas_call(
        kern, out_shape=jax.ShapeDtypeStruct((nb, tq, ATT_W), BF16),
        grid=(nb, tq // LANES),
        in_specs=[
            pl.BlockSpec((None, ATT_W, LANES), qblk),
            pl.BlockSpec((None, 2 * LANES, LANES), qblk),
            pl.BlockSpec((None, IDX_HEADS, LANES), qblk),
            pl.BlockSpec((None, lp, ATT_W), seq3),
            pl.BlockSpec((None, lp // KEY_TILE, ATT_W, KEY_TILE), seq4),
            pl.BlockSpec((None, lp, LANES), seq3),
            pl.BlockSpec((KEY_TILE, KEY_TILE), lambda b, j: (0, 0)),
        ],
        out_specs=pl.BlockSpec((None, LANES, ATT_W), lambda b, j: (b, j, 0)),
        scratch_shapes=[
            pltpu.VMEM((lp + KEY_TILE, LANES), F32),
            pltpu.VMEM((lp, LANES), F32),
            pltpu.VMEM((lp, N_HEADS * LANES), F32),
            pltpu.VMEM((ATT_W, LANES), F32),
            pltpu.VMEM((N_HEADS // 2, LANES, 2 * LANES), F32),
        ],
        compiler_params=_cparams(("parallel", "arbitrary")), name="dsa",
    )(qT, iqT, iwT, k, vT, ikb, tri)


def _band_kernel(*refs, row_lo, row_hi, qpb):
    nkb = BAND_PAST // LANES + qpb
    q_ref = refs[0]
    k_refs = refs[1:1 + nkb]
    v_refs = refs[1 + nkb:1 + 2 * nkb]
    bias_ref, o_ref = refs[1 + 2 * nkb:]
    j = pl.program_id(1)
    kb = jnp.concatenate([r[...] for r in k_refs], axis=0)
    vb = jnp.concatenate([r[...] for r in v_refs], axis=0)
    lane = lax.broadcasted_iota(I32, (1, LANES), 1)
    lo_half = lane < HEAD_DIM
    zero = jnp.zeros((), BF16)
    for i in range(qpb):
        row = (j * qpb + i) * LANES + lax.broadcasted_iota(I32, (1, BAND_KEYS), 1)
        valid = jnp.where(row >= row_lo, row, row_hi) < row_hi
        qrows = slice(i * LANES, (i + 1) * LANES)
        krows = slice(i * LANES, i * LANES + BAND_KEYS)
        for pr in range(N_HEADS // 2):
            sl = slice(pr * LANES, (pr + 1) * LANES)
            qp = q_ref[qrows, sl]
            lhs = jnp.concatenate([jnp.where(lo_half, qp, zero), jnp.where(lo_half, zero, qp)], axis=0)
            s = lax.dot_general(lhs, kb[krows, sl], (((1,), (1,)), ((), ())),
                                preferred_element_type=F32)
            bias2 = jnp.concatenate([bias_ref[2 * pr], bias_ref[2 * pr + 1]], axis=0)
            s = jnp.where(valid, s + bias2, NEG)
            m = jnp.max(s, axis=-1, keepdims=True)
            pexp = jnp.exp(s - m)
            inv = 1.0 / jnp.sum(pexp, axis=-1, keepdims=True)
            o = jnp.dot(pexp.astype(BF16), vb[krows, sl], preferred_element_type=F32) * inv
            o_ref[qrows, sl] = jnp.where(lo_half, o[:LANES], o[LANES:]).astype(BF16)


def _band(q, kp, vp, bias, *, row_lo, row_hi):
    nb, tq, _ = q.shape
    qpb = 2 if tq % (2 * LANES) == 0 else 1
    nkb = BAND_PAST // LANES + qpb
    kspecs = [pl.BlockSpec((None, LANES, ATT_W), (lambda b, j, c=c: (b, j * qpb + c, 0)))
              for c in range(nkb)]
    kern = functools.partial(_band_kernel, row_lo=row_lo, row_hi=row_hi, qpb=qpb)
    qspec = pl.BlockSpec((None, qpb * LANES, ATT_W), lambda b, j: (b, j, 0))
    return pl.pallas_call(
        kern, out_shape=jax.ShapeDtypeStruct((nb, tq, ATT_W), BF16),
        grid=(nb, tq // (qpb * LANES)),
        in_specs=[qspec] + kspecs + kspecs + [pl.BlockSpec(bias.shape, lambda b, j: (0, 0, 0))],
        out_specs=qspec,
        compiler_params=_cparams(("parallel", "parallel")), name="band",
    )(q, *([kp] * nkb), *([vp] * nkb), bias)


def _merge_kernel(x_ref, oa_ref, ob_ref, gattn_ref, wg_ref, wa_ref, wb_ref, wo_ref,
                  gffn_ref, wpqT_ref, c1_ref, c2_ref,
                  x2_ref, h2T_ref, s1_ref, s2_ref):
    x = x_ref[...]
    ms = jnp.mean(x * x, axis=-1, keepdims=True)
    h = (x * lax.rsqrt(ms + EPS) * gattn_ref[...]).astype(BF16)
    ga = jax.nn.sigmoid(jnp.dot(h, wg_ref[:, :D_MODEL], preferred_element_type=F32))
    gb = jax.nn.sigmoid(jnp.dot(h, wg_ref[:, D_MODEL:], preferred_element_type=F32))
    ma = jnp.dot(oa_ref[...], wa_ref[...], preferred_element_type=F32)
    mb = jnp.dot(ob_ref[...], wb_ref[...], preferred_element_type=F32)
    merged = (ga * ma + gb * mb).astype(BF16)
    x2 = x + jnp.dot(merged, wo_ref[...], preferred_element_type=F32)
    x2_ref[...] = x2
    ms2 = jnp.mean(x2 * x2, axis=-1, keepdims=True)
    h2T = (x2 * lax.rsqrt(ms2 + EPS) * gffn_ref[...]).T.astype(BF16)
    h2T_ref[...] = h2T
    half = PEER_DQ // 2
    for hh in range(PEER_HEADS):
        qT = jnp.dot(wpqT_ref[hh * PEER_DQ:(hh + 1) * PEER_DQ, :], h2T,
                     preferred_element_type=F32).astype(BF16)
        s1_ref[hh] = jnp.dot(c1_ref[...], qT[:half], preferred_element_type=F32)
        s2_ref[hh] = jnp.dot(c2_ref[...], qT[half:], preferred_element_type=F32)


def _merge(x, oa, ob, gattn, wg, wa, wb, wo, gffn, wpqT, c1, c2):
    n = x.shape[0]
    tm = ROW_TILE
    row = lambda i: (i, 0)
    const = lambda i: (0, 0)
    return pl.pallas_call(
        _merge_kernel,
        out_shape=(
            jax.ShapeDtypeStruct((n, D_MODEL), F32),
            jax.ShapeDtypeStruct((D_MODEL, n), BF16),
            jax.ShapeDtypeStruct((PEER_HEADS, N_KEYS, n), F32),
            jax.ShapeDtypeStruct((PEER_HEADS, N_KEYS, n), F32),
        ),
        grid=(n // tm,),
        in_specs=[
            pl.BlockSpec((tm, D_MODEL), row),
            pl.BlockSpec((tm, ATT_W), row),
            pl.BlockSpec((tm, ATT_W), row),
            pl.BlockSpec((1, D_MODEL), const),
            pl.BlockSpec(wg.shape, const),
            pl.BlockSpec(wa.shape, const),
            pl.BlockSpec(wb.shape, const),
            pl.BlockSpec(wo.shape, const),
            pl.BlockSpec((1, D_MODEL), const),
            pl.BlockSpec(wpqT.shape, const),
            pl.BlockSpec(c1.shape, const),
            pl.BlockSpec(c2.shape, const),
        ],
        out_specs=(
            pl.BlockSpec((tm, D_MODEL), row),
            pl.BlockSpec((D_MODEL, tm), lambda i: (0, i)),
            pl.BlockSpec((PEER_HEADS, N_KEYS, tm), lambda i: (0, 0, i)),
            pl.BlockSpec((PEER_HEADS, N_KEYS, tm), lambda i: (0, 0, i)),
        ),
        compiler_params=_cparams(("parallel",)), name="merge",
    )(x, oa, ob, gattn, wg, wa, wb, wo, gffn, wpqT, c1, c2)


def _pack_rows(x):
    return pltpu.bitcast(x, jnp.uint32)


def _unpack_rows(x):
    return pltpu.bitcast(x, BF16)


def _top16(x, distinct):
    idx = lax.broadcasted_iota(I32, x.shape, 0)
    rank = jnp.full(x.shape, PEER_TOPK, I32)
    rows = []
    for r in range(PEER_TOPK):
        m = jnp.max(x, axis=0, keepdims=True)
        rows.append(m)
        hit = x == m
        if not distinct:
            hit = idx == jnp.min(jnp.where(hit, idx, N_KEYS), axis=0, keepdims=True)
        rank = jnp.where(hit, r, rank)
        x = jnp.where(hit, -jnp.inf, x)
    return jnp.concatenate(rows, axis=0), rank


def _select_kernel(s1_ref, s2_ref, cnt_ref, e1_ref, r2_ref, e2_ref):
    sub8 = lax.broadcasted_iota(I32, (8, 1), 0).astype(F32)
    sub16 = lax.broadcasted_iota(I32, (PEER_TOPK, 1), 0).astype(F32)

    def count(mask):
        return jnp.sum(jnp.where(mask, 1.0, 0.0), axis=0, keepdims=True)

    def head_body(hh, carry):
        s1 = s1_ref[hh]
        s2 = s2_ref[hh]
        v1, r1 = _top16(s1, True)
        v2, r2 = _top16(s2, True)
        ranked = jnp.maximum(count(r1 < PEER_TOPK), count(r2 < PEER_TOPK))
        v1, r1, v2, r2 = lax.cond(
            jnp.max(ranked) > PEER_TOPK,
            lambda: _top16(s1, False) + _top16(s2, False),
            lambda: (v1, r1, v2, r2))
        pieces = [v1[0:1] + v2]
        for a in range(1, 8):
            piece = v1[a:a + 1] + v2[0:8]
            pieces.append(jnp.where(sub8 < PEER_TOPK // (a + 1), piece, -jnp.inf))
        pieces.append(v1[8:16] + v2[0:1])
        cand = jnp.concatenate(pieces, axis=0)
        rem = jnp.full((1, cand.shape[1]), float(PEER_TOPK), F32)
        tau = jnp.zeros((1, cand.shape[1]), F32)
        x = cand
        for _ in range(PEER_TOPK):
            m = jnp.max(x, axis=0, keepdims=True)
            hit = x == m
            tau = jnp.where(rem > 0, m, tau)
            rem = rem - count(hit)
            x = jnp.where(hit, -jnp.inf, x)
        need = PEER_TOPK - count(cand > tau)
        kept = []
        for a in range(PEER_TOPK):
            row = pieces[a] if a < 8 else pieces[8][a - 8:a - 7]
            ties = count(row == tau)
            kept.append(count(row > tau) + jnp.minimum(ties, jnp.maximum(need, 0.0)))
            need = need - ties
        smax = v1[0:1] + v2[0:1]
        z = jnp.sum(jnp.where(sub16 < kept[0], jnp.exp(pieces[0] - smax), 0.0), axis=0, keepdims=True)
        for a in range(1, 8):
            z = z + jnp.sum(jnp.where(sub8 < kept[a], jnp.exp(pieces[a] - smax), 0.0),
                            axis=0, keepdims=True)
        tail = jnp.concatenate(kept[8:], axis=0)
        z = z + jnp.sum(jnp.where(tail > 0, jnp.exp(pieces[8] - smax), 0.0), axis=0, keepdims=True)
        cnt = jnp.zeros(s1.shape, F32)
        for a in range(PEER_TOPK):
            cnt = jnp.where(r1 == a, kept[a], cnt)
        cnt_ref[hh] = cnt
        e1_ref[hh] = jnp.exp(s1 - v1[0:1]) * (1.0 / z)
        r2b = r2.astype(F32).astype(BF16)
        e2b = jnp.exp(s2 - v2[0:1]).astype(BF16)
        for c in range(s2.shape[1] // LANES):
            cs = slice(c * LANES, (c + 1) * LANES)
            r2_ref[hh, c] = _pack_rows(r2b[:, cs])
            e2_ref[hh, c] = _pack_rows(e2b[:, cs])
        return carry

    lax.fori_loop(0, PEER_HEADS, head_body, 0)


def _select(s1, s2):
    n = s1.shape[-1]
    tn = SEL_TILE
    blk = pl.BlockSpec((PEER_HEADS, N_KEYS, tn), lambda i: (0, 0, i))
    packed = jax.ShapeDtypeStruct((PEER_HEADS, n // LANES, N_KEYS // 2, LANES), jnp.uint32)
    pblk = pl.BlockSpec((PEER_HEADS, tn // LANES, N_KEYS // 2, LANES), lambda i: (0, i, 0, 0))
    return pl.pallas_call(
        _select_kernel,
        out_shape=(
            jax.ShapeDtypeStruct(s1.shape, F32),
            jax.ShapeDtypeStruct(s1.shape, F32),
            packed, packed,
        ),
        grid=(n // tn,),
        in_specs=[blk, blk],
        out_specs=(blk, blk, pblk, pblk),
        compiler_params=_cparams(("parallel",)), name="select",
    )(s1, s2)


def _peer_kernel(h2T_ref, u_ref, vT_ref, vTp_ref, cnt_ref, e1_ref, r2_ref, e2_ref, x2_ref,
                 y_ref, acc_ref, wa0_ref, wa1_ref, z0_ref, z1_ref, rowc_ref, rowe_ref):
    e = pl.program_id(1)
    zero = jnp.minimum(e, 0)
    z_refs = (z0_ref, z1_ref)
    wa_refs = (wa0_ref, wa1_ref)

    @pl.when(e == 0)
    def _():
        acc_ref[...] = jnp.zeros_like(acc_ref)
        wa_refs[(PEER_EB // PEER_GRP - 1) % 2][...] = jnp.zeros_like(wa1_ref)

    tn = h2T_ref.shape[1]
    nsub = PEER_EB // N_KEYS
    reps = N_KEYS // BF16_ROWS
    for sub in range(nsub):
        for hh in range(PEER_HEADS):
            rowc_ref[sub, hh] = _pack_rows(
                jnp.broadcast_to(cnt_ref[hh, sub:sub + 1, :], (BF16_ROWS, tn)).astype(BF16))
            rowe_ref[sub, hh] = _pack_rows(
                jnp.broadcast_to(e1_ref[hh, sub:sub + 1, :], (BF16_ROWS, tn)).astype(BF16))

    ngrp = PEER_EB // PEER_GRP

    def scores(g, slot):
        start = g * PEER_GRP if isinstance(g, int) else pl.multiple_of(g * PEER_GRP, PEER_GRP)
        z_refs[slot][0] = jnp.dot(u_ref[pl.ds(start, PEER_GRP), :], h2T_ref[...],
                                  preferred_element_type=F32)

    def weights(g, slot):
        for s in range(PEER_GRP // N_KEYS):
            sub = g * (PEER_GRP // N_KEYS) + s
            for c in range(tn // LANES):
                cs = slice(c * LANES, (c + 1) * LANES)
                zc = z_refs[slot][zero, s * N_KEYS:(s + 1) * N_KEYS, cs]
                a = (0.5 * zc * (1.0 + lax.erf(zc * (2.0 ** -0.5)))).astype(BF16)
                w = jnp.zeros((N_KEYS, LANES), BF16)
                for hh in range(PEER_HEADS):
                    cnt = jnp.concatenate([_unpack_rows(rowc_ref[sub + zero, hh, :, cs])] * reps, axis=0)
                    e1 = jnp.concatenate([_unpack_rows(rowe_ref[sub + zero, hh, :, cs])] * reps, axis=0)
                    w = w + jnp.where(_unpack_rows(r2_ref[hh, c]) < cnt,
                                      _unpack_rows(e2_ref[hh, c]) * e1, jnp.zeros((), BF16))
                wa_refs[slot][0, s * (N_KEYS // 2):(s + 1) * (N_KEYS // 2), cs] = _pack_rows(w * a)

    def down(vg, slot):
        acc_ref[...] += lax.dot_general(vg, _unpack_rows(wa_refs[slot][zero]),
                                        (((0,), (0,)), ((), ())), preferred_element_type=F32)

    last = (ngrp - 1) % 2
    scores(0, 0)
    down(vTp_ref[0], last)
    for g in range(ngrp):
        if g + 1 < ngrp:
            scores(g + 1, (g + 1) % 2)
        weights(g, g % 2)
        if g > 0:
            down(vT_ref[g - 1], (g - 1) % 2)

    @pl.when(e == pl.num_programs(1) - 1)
    def _():
        down(vT_ref[ngrp - 1], last)
        y_ref[...] = x2_ref[...] + acc_ref[...].T


def _peer(h2T, u, vT, cnt, e1, r2, e2, x2):
    n = x2.shape[0]
    tn = min(PEER_TOK, n)
    nsub = PEER_EB // N_KEYS
    tok3 = pl.BlockSpec((PEER_HEADS, tn // LANES, N_KEYS // 2, LANES), lambda i, e: (0, i, 0, 0))
    key1 = pl.BlockSpec((PEER_HEADS, nsub, tn), lambda i, e: (0, e, i))
    return pl.pallas_call(
        _peer_kernel,
        out_shape=jax.ShapeDtypeStruct((n, D_MODEL), F32),
        grid=(n // tn, N_EXPERTS // PEER_EB),
        in_specs=[
            pl.BlockSpec((D_MODEL, tn), lambda i, e: (0, i)),
            pl.BlockSpec((PEER_EB, D_MODEL), lambda i, e: (e, 0)),
            pl.BlockSpec((PEER_EB // PEER_GRP, PEER_GRP, D_MODEL), lambda i, e: (e, 0, 0)),
            pl.BlockSpec((1, PEER_GRP, D_MODEL),
                         lambda i, e: (jnp.maximum(e * (PEER_EB // PEER_GRP) - 1, 0), 0, 0)),
            key1, key1, tok3, tok3,
            pl.BlockSpec((tn, D_MODEL), lambda i, e: (i, 0)),
        ],
        out_specs=pl.BlockSpec((tn, D_MODEL), lambda i, e: (i, 0)),
        scratch_shapes=[
            pltpu.VMEM((D_MODEL, tn), F32),
            pltpu.VMEM((1, PEER_GRP // 2, tn), jnp.uint32),
            pltpu.VMEM((1, PEER_GRP // 2, tn), jnp.uint32),
            pltpu.VMEM((1, PEER_GRP, tn), F32),
            pltpu.VMEM((1, PEER_GRP, tn), F32),
            pltpu.VMEM((nsub, PEER_HEADS, BF16_ROWS // 2, tn), jnp.uint32),
            pltpu.VMEM((nsub, PEER_HEADS, BF16_ROWS // 2, tn), jnp.uint32),
        ],
        compiler_params=_cparams(("parallel", "arbitrary")), name="peer",
    )(h2T, u, vT, vT, cnt, e1, r2, e2, x2)


def _rope_tables(pos):
    posf = pos.astype(F32)[:, None]
    t = pos.shape[0]

    def tab(half, reps):
        inv = jnp.power(ROPE_THETA, -jnp.arange(half, dtype=F32) / half)
        ang = posf * inv[None, :]
        cos = jnp.cos(ang)
        sin = jnp.sin(ang)
        return (jnp.tile(jnp.concatenate([cos, cos], axis=1), (1, reps)),
                jnp.tile(jnp.concatenate([-sin, sin], axis=1), (1, reps)))

    cosq, sinq = tab(HEAD_DIM // 2, N_HEADS)
    cosi, sini = tab(IDX_DIM // 2, IDX_HEADS + 1)
    pad = IDX_W - cosi.shape[1]
    cosi = jnp.concatenate([cosi, jnp.ones((t, pad), F32)], axis=1)
    sini = jnp.concatenate([sini, jnp.zeros((t, pad), F32)], axis=1)
    return cosq, sinq, cosi, sini


def _band_bias(rel_bias):
    qi = np.arange(LANES)[:, None]
    kj = np.arange(BAND_KEYS)[None, :]
    ok = (kj // CHUNK >= qi // CHUNK) & (kj // CHUNK <= qi // CHUNK + BAND_PAST // CHUNK)
    n = BAND_KEYS + LANES - 1
    m = np.arange(n)
    diag = rel_bias.astype(F32)[:, np.clip(BAND_PAST + LANES - 1 - m, -REL_CLIP, REL_CLIP) + REL_CLIP]
    wrapped = jnp.tile(jnp.pad(diag, ((0, 0), (0, 1))), (1, LANES))[:, :LANES * n]
    toep = wrapped.reshape(-1, LANES, n)[:, :, LANES - 1:LANES - 1 + BAND_KEYS]
    return jnp.where(jnp.asarray(ok)[None], toep, NEG)


def _layer_weights(g_attn, w_in, g_qa, g_ka, g_ik, g_qb, g_kb, rel_bias, w_br_a, w_br_b,
                   w_out, g_ffn, w_pq, c1, c2, u, v):
    a0 = 3 * ATT_W
    i0 = a0 + IDX_HEADS * IDX_DIM + IDX_DIM + IDX_HEADS
    b0 = i0 + 3 * ATT_W
    idx_cols = i0 - a0
    w1 = jnp.concatenate(
        [w_in[:, :a0], w_in[:, i0:b0], w_in[:, a0:i0],
         jnp.zeros((D_MODEL, IDX_W - idx_cols), w_in.dtype)], axis=1).astype(BF16)
    wg = w_in[:, b0:].astype(BF16)
    hd = jnp.asarray(np.kron(np.eye(N_HEADS), np.ones((HEAD_DIM, HEAD_DIM))) / HEAD_DIM, BF16)
    gh = jnp.stack([jnp.tile(g, N_HEADS) for g in (g_qa, g_ka, g_qb, g_kb)]).astype(F32)
    gidx = jnp.concatenate(
        [g_ik.astype(F32), jnp.full((IDX_HEADS,), IDX_HEADS ** -0.5, F32),
         jnp.ones((LANES - IDX_DIM - IDX_HEADS,), F32)])[None, :]
    tri = jnp.asarray(np.tril(np.ones((KEY_TILE, KEY_TILE)), -1), BF16)
    return dict(
        w1=w1, wg=wg, hd=hd, gh=gh, gidx=gidx, tri=tri,
        gattn=g_attn.astype(F32)[None, :], gffn=g_ffn.astype(F32)[None, :],
        bias=_band_bias(rel_bias),
        wa=w_br_a.astype(BF16), wb=w_br_b.astype(BF16), wo=w_out.astype(BF16),
        wpqT=w_pq.T.astype(BF16), c1=c1.astype(BF16), c2=c2.astype(BF16),
        u=u.astype(BF16),
        vT=v.astype(BF16).reshape(N_EXPERTS // PEER_GRP, PEER_GRP, D_MODEL))


def _ffn(x, oa, ob, w):
    x2, h2T, s1, s2 = _merge(x, oa, ob, w["gattn"], w["wg"], w["wa"], w["wb"], w["wo"],
                             w["gffn"], w["wpqT"], w["c1"], w["c2"])
    cnt, e1, r2, e2 = _select(s1, s2)
    return _peer(h2T, w["u"], w["vT"], cnt, e1, r2, e2, x2)


def _prompt_layer(x, w):
    b, t, _ = x.shape
    xf = x.reshape(b * t, D_MODEL)
    tabs = _rope_tables(jnp.arange(t))
    (aqT, ak, akb, av, avT, iqT, ik, ikb, iwT, bq, bk, bkb, bv, bvb) = _proj(
        xf, b, t, w["w1"], w["gattn"], w["hd"], w["gh"], w["gidx"], *tabs)
    oa = _dsa(aqT, iqT, iwT, akb.reshape(b, t, ATT_W), avT, ikb.reshape(b, t, LANES),
              w["tri"], q0=0, n_valid=t)
    front = ((0, 0), (BAND_PAST, 0), (0, 0))
    ob = _band(bq.reshape(b, t, ATT_W), jnp.pad(bkb.reshape(b, t, ATT_W), front),
               jnp.pad(bvb.reshape(b, t, ATT_W), front), w["bias"],
               row_lo=BAND_PAST, row_hi=BAND_PAST + t)
    y = _ffn(xf, oa.reshape(b * t, ATT_W), ob.reshape(b * t, ATT_W), w)
    keep = min(BAND_PAST, t)
    heads = lambda a: a.reshape(b, t, N_HEADS, HEAD_DIM)
    return y.reshape(b, t, D_MODEL), (
        heads(ak), heads(av), ik.reshape(b, t, IDX_DIM),
        heads(bk)[:, t - keep:], heads(bv)[:, t - keep:])


def _sample_layer(x, ca_k, ca_v, ca_ik, cb_k, cb_v, w):
    b, t, _ = x.shape
    n = b * t
    past = ca_k.shape[1]
    xf = x.reshape(n, D_MODEL)
    tabs = _rope_tables(jnp.tile(past + jnp.arange(t), b))
    (aqT, ak, akb, av, avT, iqT, ik, ikb, iwT, bq, bk, bkb, bv, bvb) = _proj(
        xf, 1, n, w["w1"], w["gattn"], w["hd"], w["gh"], w["gidx"], *tabs)

    def per_seq_T(aT, rows):
        a = aT[0].T.reshape(b, t, rows)
        return jnp.pad(a, ((0, 0), (0, LANES - t), (0, 0))).transpose(0, 2, 1)

    lk = past + t
    lp = -(-lk // KEY_TILE) * KEY_TILE
    padk = ((0, 0), (0, lp - lk), (0, 0))
    k_all = jnp.pad(jnp.concatenate(
        [ca_k.reshape(b, past, ATT_W).astype(BF16), akb.reshape(b, t, ATT_W)], axis=1), padk)
    v_all = jnp.pad(jnp.concatenate(
        [ca_v.reshape(b, past, ATT_W).astype(BF16), av.astype(BF16).reshape(b, t, ATT_W)],
        axis=1), padk)
    vT_all = v_all.reshape(b, lp // KEY_TILE, KEY_TILE, ATT_W).transpose(0, 1, 3, 2)
    ik_all = jnp.pad(jnp.concatenate(
        [jnp.pad(ca_ik.astype(BF16), ((0, 0), (0, 0), (0, LANES - IDX_DIM))),
         ikb.reshape(b, t, LANES)], axis=1), padk)
    oa = _dsa(per_seq_T(aqT, ATT_W), per_seq_T(iqT, 2 * LANES), per_seq_T(iwT, IDX_HEADS),
              k_all, vT_all, ik_all, w["tri"], q0=past, n_valid=lk)[:, :t]

    pb = cb_k.shape[1]
    padb = ((0, 0), (BAND_PAST - pb, LANES - t), (0, 0))
    qpad = jnp.pad(bq.reshape(b, t, ATT_W), ((0, 0), (0, LANES - t), (0, 0)))
    kb_all = jnp.pad(jnp.concatenate(
        [cb_k.reshape(b, pb, ATT_W).astype(BF16), bkb.reshape(b, t, ATT_W)], axis=1), padb)
    vb_all = jnp.pad(jnp.concatenate(
        [cb_v.reshape(b, pb, ATT_W).astype(BF16), bvb.reshape(b, t, ATT_W)], axis=1), padb)
    ob = _band(qpad, kb_all, vb_all, w["bias"],
               row_lo=BAND_PAST - pb, row_hi=BAND_PAST + t)[:, :t]

    y = _ffn(xf, oa.reshape(n, ATT_W), ob.reshape(n, ATT_W), w)
    keep = min(BAND_PAST, t)
    heads = lambda a: a.reshape(b, t, N_HEADS, HEAD_DIM)
    return y.reshape(b, t, D_MODEL), (
        heads(ak), heads(av), ik.reshape(b, t, IDX_DIM),
        heads(bk)[:, t - keep:], heads(bv)[:, t - keep:])


def kernel(x_prompt, x_sample, cache_a_k, cache_a_v, cache_a_ik, cache_b_k, cache_b_v,
           g_attn, w_in, g_qa, g_ka, g_ik, g_qb, g_kb, rel_bias, w_br_a, w_br_b, w_out,
           g_ffn, w_pq, peer_c1, peer_c2, peer_u, peer_v):
    depth = w_in.shape[0]
    xp, xs = x_prompt, x_sample
    sp, ss = [], []
    for l in range(depth):
        w = _layer_weights(g_attn[l], w_in[l], g_qa[l], g_ka[l], g_ik[l], g_qb[l], g_kb[l],
                           rel_bias[l], w_br_a[l], w_br_b[l], w_out[l], g_ffn[l], w_pq[l],
                           peer_c1[l], peer_c2[l], peer_u[l], peer_v[l])
        xp, st_p = _prompt_layer(xp, w)
        xs, st_s = _sample_layer(xs, cache_a_k[l], cache_a_v[l], cache_a_ik[l],
                                 cache_b_k[l], cache_b_v[l], w)
        sp.append(st_p)
        ss.append(st_s)
    stack = lambda sts, i: jnp.stack([s[i] for s in sts])
    return (xp, xs) + tuple(stack(sp, i) for i in range(5)) + tuple(stack(ss, i) for i in range(5))
```

```python
import functools

import jax
import jax.numpy as jnp
import numpy as np
from jax import lax
from jax.experimental import pallas as pl
from jax.experimental.pallas import tpu as pltpu

F32 = jnp.float32
BF16 = jnp.bfloat16
I32 = jnp.int32

D_MODEL = 1024
HEAD_DIM = 64
N_HEADS = 8
ATT_W = N_HEADS * HEAD_DIM
IDX_HEADS = 8
IDX_DIM = 32
IDX_W = 384
CHUNK = 64
TOPK_MAX = 256
BAND_PAST = 512
BAND_KEYS = BAND_PAST + 128
REL_CLIP = 128
PEER_HEADS = 8
PEER_DQ = 256
N_KEYS = 128
N_EXPERTS = N_KEYS * N_KEYS
PEER_TOPK = 16
ROPE_THETA = 10000.0
EPS = 1e-6
NEG = -1e30
INT_MIN = -(2 ** 31)

LANES = 128
BF16_ROWS = 16
KEY_TILE = 256
ROW_TILE = 512
SEL_TILE = 256
PEER_TOK = 1024
PEER_EB = 1024
PEER_GRP = 256
VMEM_LIMIT = 56 * 1024 * 1024


def _cparams(sem):
    return pltpu.CompilerParams(dimension_semantics=sem, vmem_limit_bytes=VMEM_LIMIT)


def _rope_chunks(y, cos, sin, half):
    lane = lax.broadcasted_iota(I32, (1, LANES), 1)
    lo = (lane % (2 * half)) < half
    out = []
    for c in range(y.shape[1] // LANES):
        sl = slice(c * LANES, (c + 1) * LANES)
        yc = y[:, sl]
        partner = jnp.where(lo, pltpu.roll(yc, LANES - half, 1), pltpu.roll(yc, half, 1))
        out.append(yc * cos[:, sl] + partner * sin[:, sl])
    return jnp.concatenate(out, axis=1)


def _proj_kernel(x_ref, gattn_ref, w1_ref, hd_ref, gh_ref, gidx_ref,
                 cosq_ref, sinq_ref, cosi_ref, sini_ref,
                 aqT_ref, ak_ref, akb_ref, av_ref, avT_ref, iqT_ref, ik_ref, ikb_ref,
                 iwT_ref, bq_ref, bk_ref, bkb_ref, bv_ref, bvb_ref):
    x = x_ref[...]
    ms = jnp.mean(x * x, axis=-1, keepdims=True)
    h = (x * lax.rsqrt(ms + EPS) * gattn_ref[...]).astype(BF16)

    def seg(i):
        return jnp.dot(h, w1_ref[:, i * ATT_W:(i + 1) * ATT_W], preferred_element_type=F32)

    def headnorm(y, gi):
        hms = jnp.dot((y * y).astype(BF16), hd_ref[...], preferred_element_type=F32)
        return y * lax.rsqrt(hms + EPS) * gh_ref[gi:gi + 1, :]

    cosq = cosq_ref[...]
    sinq = sinq_ref[...]

    aq = _rope_chunks(headnorm(seg(0), 0), cosq, sinq, HEAD_DIM // 2) * (HEAD_DIM ** -0.5)
    aqT_ref[...] = aq.T.astype(BF16)

    ak = _rope_chunks(headnorm(seg(1), 1), cosq, sinq, HEAD_DIM // 2)
    ak_ref[...] = ak
    akb_ref[...] = ak.astype(BF16)

    av = seg(2)
    av_ref[...] = av
    for c in range(av.shape[0] // KEY_TILE):
        avT_ref[c] = av[c * KEY_TILE:(c + 1) * KEY_TILE, :].T.astype(BF16)

    bq_ref[...] = (headnorm(seg(3), 2) * (HEAD_DIM ** -0.5)).astype(BF16)
    bk = headnorm(seg(4), 3)
    bk_ref[...] = bk
    bkb_ref[...] = bk.astype(BF16)
    bv = seg(5)
    bv_ref[...] = bv
    bvb_ref[...] = bv.astype(BF16)

    s = jnp.dot(h, w1_ref[:, 6 * ATT_W:6 * ATT_W + IDX_W], preferred_element_type=F32)
    lane = lax.broadcasted_iota(I32, (1, LANES), 1)
    is_ik = lane < IDX_DIM
    c2 = s[:, 2 * LANES:3 * LANES]
    ikms = jnp.sum(jnp.where(is_ik, c2 * c2, 0.0), axis=-1, keepdims=True) * (1.0 / IDX_DIM)
    c2n = c2 * jnp.where(is_ik, lax.rsqrt(ikms + EPS), 1.0) * gidx_ref[...]
    sn = jnp.concatenate([s[:, :2 * LANES], c2n], axis=1)
    r = _rope_chunks(sn, cosi_ref[...], sini_ref[...], IDX_DIM // 2)
    iqT_ref[...] = r[:, :2 * LANES].T.astype(BF16)
    c2r = r[:, 2 * LANES:]
    ik_ref[...] = c2r[:, :IDX_DIM]
    ikb_ref[...] = jnp.where(is_ik, c2r, 0.0).astype(BF16)
    iwT_ref[...] = c2r.T[IDX_DIM:IDX_DIM + IDX_HEADS, :]


def _proj(x, nb, tb, w1, gattn, hd, gh, gidx, cosq, sinq, cosi, sini):
    n = x.shape[0]
    tm = ROW_TILE
    tpb = tb // tm
    row = lambda i: (i, 0)
    tab = lambda i: (i % tpb, 0)
    const = lambda i: (0, 0)
    grp = lambda i: (i // tpb, 0, i % tpb)
    out_shape = (
        jax.ShapeDtypeStruct((nb, ATT_W, tb), BF16),
        jax.ShapeDtypeStruct((n, ATT_W), F32),
        jax.ShapeDtypeStruct((n, ATT_W), BF16),
        jax.ShapeDtypeStruct((n, ATT_W), F32),
        jax.ShapeDtypeStruct((nb, tb // KEY_TILE, ATT_W, KEY_TILE), BF16),
        jax.ShapeDtypeStruct((nb, 2 * LANES, tb), BF16),
        jax.ShapeDtypeStruct((n, IDX_DIM), F32),
        jax.ShapeDtypeStruct((n, LANES), BF16),
        jax.ShapeDtypeStruct((nb, IDX_HEADS, tb), F32),
        jax.ShapeDtypeStruct((n, ATT_W), BF16),
        jax.ShapeDtypeStruct((n, ATT_W), F32),
        jax.ShapeDtypeStruct((n, ATT_W), BF16),
        jax.ShapeDtypeStruct((n, ATT_W), F32),
        jax.ShapeDtypeStruct((n, ATT_W), BF16),
    )
    out_specs = (
        pl.BlockSpec((None, ATT_W, tm), grp),
        pl.BlockSpec((tm, ATT_W), row),
        pl.BlockSpec((tm, ATT_W), row),
        pl.BlockSpec((tm, ATT_W), row),
        pl.BlockSpec((None, tm // KEY_TILE, ATT_W, KEY_TILE),
                     lambda i: (i // tpb, i % tpb, 0, 0)),
        pl.BlockSpec((None, 2 * LANES, tm), grp),
        pl.BlockSpec((tm, IDX_DIM), row),
        pl.BlockSpec((tm, LANES), row),
        pl.BlockSpec((None, IDX_HEADS, tm), grp),
        pl.BlockSpec((tm, ATT_W), row),
        pl.BlockSpec((tm, ATT_W), row),
        pl.BlockSpec((tm, ATT_W), row),
        pl.BlockSpec((tm, ATT_W), row),
        pl.BlockSpec((tm, ATT_W), row),
    )
    in_specs = [
        pl.BlockSpec((tm, D_MODEL), row),
        pl.BlockSpec((1, D_MODEL), const),
        pl.BlockSpec(w1.shape, const),
        pl.BlockSpec(hd.shape, const),
        pl.BlockSpec(gh.shape, const),
        pl.BlockSpec(gidx.shape, const),
        pl.BlockSpec((tm, ATT_W), tab),
        pl.BlockSpec((tm, ATT_W), tab),
        pl.BlockSpec((tm, IDX_W), tab),
        pl.BlockSpec((tm, IDX_W), tab),
    ]
    return pl.pallas_call(
        _proj_kernel, out_shape=out_shape, grid=(n // tm,),
        in_specs=in_specs, out_specs=out_specs,
        compiler_params=_cparams(("parallel",)), name="proj",
    )(x, gattn, w1, hd, gh, gidx, cosq, sinq, cosi, sini)


def _dsa_kernel(qT_ref, iqT_ref, iwT_ref, k_ref, vT_ref, ikb_ref, tri_ref,
                o_ref, sc_ref, am_ref, s_ref, oT_ref, acc_ref, *, q0, n_valid, n_key_tiles, n_sel):
    j = pl.program_id(1)
    qbase = q0 + j * LANES
    n_t = jnp.minimum(lax.shift_right_logical(qbase + LANES + KEY_TILE - 1, 8), n_key_tiles)
    qchunk = lax.shift_right_logical(
        qbase + lax.broadcasted_iota(I32, (1, LANES), 1), 6)

    def rows(t):
        return pl.ds(pl.multiple_of(t * KEY_TILE, KEY_TILE), KEY_TILE)

    def admissible(t):
        kidx = t * KEY_TILE + lax.broadcasted_iota(I32, (KEY_TILE, LANES), 0)
        return jnp.where(kidx < n_valid, lax.shift_right_logical(kidx, 6), 1 << 30) <= qchunk

    def fold8(v):
        return v.reshape(KEY_TILE // 8, 8, v.shape[-1])

    zpad = jnp.zeros((LANES - IDX_DIM, LANES), BF16)
    rhs_idx = jnp.concatenate(
        [jnp.concatenate([iqT_ref[h * IDX_DIM:(h + 1) * IDX_DIM, :], zpad], axis=0)
         for h in range(IDX_HEADS)], axis=1)

    def score_body(t, carry, width):
        both = pl.ds(pl.multiple_of(t * KEY_TILE, KEY_TILE), width * KEY_TILE)
        res = jnp.dot(ikb_ref[both, :], rhs_idx, preferred_element_type=F32)
        for w in range(width):
            part = res[w * KEY_TILE:(w + 1) * KEY_TILE]
            acc = jnp.zeros((KEY_TILE, LANES), F32)
            for h in range(IDX_HEADS):
                acc = acc + jnp.maximum(part[:, h * LANES:(h + 1) * LANES], 0.0) * iwT_ref[h:h + 1, :]
            sc_ref[rows(t + w), :] = jnp.where(admissible(t + w), acc, NEG)
        return carry

    n_pairs = lax.shift_right_logical(n_t, 1)
    lax.fori_loop(0, n_pairs, lambda i, c: score_body(2 * i, c, 2), 0)
    lax.fori_loop(2 * n_pairs, n_t, lambda t, c: score_body(t, c, 1), 0)
    sc_ref[rows(n_t), :] = jnp.full((KEY_TILE, LANES), -jnp.inf, F32)

    def count(pred):
        def body(t2, c8):
            two = pl.ds(pl.multiple_of(t2 * (2 * KEY_TILE), 2 * KEY_TILE), 2 * KEY_TILE)
            hits = jnp.where(pred(sc_ref[two, :]), 1, 0)
            return c8 + jnp.sum(hits.reshape(2 * KEY_TILE // 8, 8, LANES), axis=0)
        c8 = lax.fori_loop(0, lax.shift_right_logical(n_t + 1, 1), body, jnp.zeros((8, LANES), I32))
        return jnp.sum(c8, axis=0, keepdims=True)

    def bisect_body(it, p):
        c = p + lax.shift_left(jnp.int32(1), 31 - it)
        cf = pltpu.bitcast(jnp.where(c >= 0, c, c ^ 0x7FFFFFFF), F32)
        return jnp.where(count(lambda v: v >= cf) >= n_sel, c, p)

    p = lax.fori_loop(0, 32, bisect_body, jnp.full((1, LANES), INT_MIN, I32))
    thr = pltpu.bitcast(jnp.where(p >= 0, p, p ^ 0x7FFFFFFF), F32)

    def mask_simple():
        def body(t, carry):
            keep = jnp.where(sc_ref[rows(t), :] >= thr, 0.0, NEG)
            am_ref[rows(t), :] = jnp.where(admissible(t), keep, NEG)
            return carry
        lax.fori_loop(0, n_t, body, 0)

    def mask_ties():
        need = (n_sel - count(lambda v: v > thr)).astype(F32)

        def body(t, carry):
            v = sc_ref[rows(t), :]
            eq = v == thr
            eqf = jnp.where(eq, 1.0, 0.0)
            pre = jnp.dot(tri_ref[...], eqf.astype(BF16), preferred_element_type=F32) + carry
            tie = jnp.where(eq, jnp.where(pre < need, 0.0, NEG), NEG)
            keep = jnp.where(v > thr, 0.0, tie)
            am_ref[rows(t), :] = jnp.where(admissible(t), keep, NEG)
            return carry + jnp.sum(jnp.sum(fold8(eqf), axis=0), axis=0, keepdims=True)
        lax.fori_loop(0, n_t, body, jnp.zeros((1, LANES), F32))

    lax.cond(jnp.max(count(lambda v: v >= thr)) == n_sel, mask_simple, mask_ties)

    npair = N_HEADS // 2
    pw = 2 * LANES
    zq = jnp.zeros((HEAD_DIM, LANES), BF16)
    rhs = [jnp.concatenate(
        [jnp.concatenate([qT_ref[pr * LANES:pr * LANES + HEAD_DIM, :], zq], axis=1),
         jnp.concatenate([zq, qT_ref[pr * LANES + HEAD_DIM:(pr + 1) * LANES, :]], axis=1)], axis=0)
        for pr in range(npair)]

    n_two = lax.shift_right_logical(n_t, 1)

    def span(t, width):
        return pl.ds(pl.multiple_of(t * KEY_TILE, KEY_TILE), width * KEY_TILE)

    def logit_body(t, m8, width):
        am = am_ref[span(t, width), :]
        am2 = jnp.concatenate([am, am], axis=1)
        tops = []
        for pr in range(npair):
            s = jnp.dot(k_ref[span(t, width), pr * LANES:(pr + 1) * LANES], rhs[pr],
                        preferred_element_type=F32) + am2
            s_ref[span(t, width), pr * pw:(pr + 1) * pw] = s
            tops.append(jnp.max(s.reshape(width * KEY_TILE // 8, 8, pw), axis=0))
        return jnp.maximum(m8, jnp.concatenate(tops, axis=1))

    m8 = lax.fori_loop(0, n_two, lambda i, c: logit_body(2 * i, c, 2),
                       jnp.full((8, npair * pw), NEG, F32))
    m8 = lax.fori_loop(2 * n_two, n_t, lambda t, c: logit_body(t, c, 1), m8)
    m = jnp.max(m8, axis=0, keepdims=True)
    acc_ref[...] = jnp.zeros_like(acc_ref)

    def pv_body(t, l8, width):
        sums = []
        for pr in range(npair):
            cols = slice(pr * pw, (pr + 1) * pw)
            pexp = jnp.exp(s_ref[span(t, width), cols] - m[:, cols])
            sums.append(jnp.sum(pexp.reshape(width * KEY_TILE // 8, 8, pw), axis=0))
            pb = pexp.astype(BF16)
            out = acc_ref[pr]
            for w in range(width):
                out = out + jnp.dot(vT_ref[t + w, pr * LANES:(pr + 1) * LANES, :],
                                    pb[w * KEY_TILE:(w + 1) * KEY_TILE],
                                    preferred_element_type=F32)
            acc_ref[pr] = out
        return l8 + jnp.concatenate(sums, axis=1)

    l8 = lax.fori_loop(0, n_two, lambda i, c: pv_body(2 * i, c, 2),
                       jnp.zeros((8, npair * pw), F32))
    l8 = lax.fori_loop(2 * n_two, n_t, lambda t, c: pv_body(t, c, 1), l8)
    inv = 1.0 / jnp.sum(l8, axis=0, keepdims=True)
    for pr in range(npair):
        lo = pr * LANES
        acc = acc_ref[pr]
        oT_ref[lo:lo + HEAD_DIM, :] = acc[:HEAD_DIM, :LANES] * inv[:, pr * pw:pr * pw + LANES]
        oT_ref[lo + HEAD_DIM:lo + LANES, :] = acc[HEAD_DIM:, LANES:] * inv[:, pr * pw + LANES:(pr + 1) * pw]

    o_ref[...] = oT_ref[...].T.astype(BF16)


def _dsa(qT, iqT, iwT, k, vT, ikb, tri, *, q0, n_valid):
    nb, _, tq = qT.shape
    lp = k.shape[1]
    qblk = lambda b, j: (b, 0, j)
    seq3 = lambda b, j: (b, 0, 0)
    seq4 = lambda b, j: (b, 0, 0, 0)
    kern = functools.partial(_dsa_kernel, q0=q0, n_valid=n_valid, n_key_tiles=lp // KEY_TILE,
                             n_sel=min(TOPK_MAX, n_valid // 4))
    return pl.pallas_call(
        kern, out_shape=jax.ShapeDtypeStruct((nb, tq, ATT_W), BF16),
        grid=(nb, tq // LANES),
        in_specs=[
            pl.BlockSpec((None, ATT_W, LANES), qblk),
            pl.BlockSpec((None, 2 * LANES, LANES), qblk),
            pl.BlockSpec((None, IDX_HEADS, LANES), qblk),
            pl.BlockSpec((None, lp, ATT_W), seq3),
            pl.BlockSpec((None, lp // KEY_TILE, ATT_W, KEY_TILE), seq4),
            pl.BlockSpec((None, lp, LANES), seq3),
            pl.BlockSpec((KEY_TILE, KEY_TILE), lambda b, j: (0, 0)),
        ],
        out_specs=pl.BlockSpec((None, LANES, ATT_W), lambda b, j: (b, j, 0)),
        scratch_shapes=[
            pltpu.VMEM((lp + KEY_TILE, LANES), F32),
            pltpu.VMEM((lp, LANES), F32),
            pltpu.VMEM((lp, N_HEADS * LANES), F32),
            pltpu.VMEM((ATT_W, LANES), F32),
            pltpu.VMEM((N_HEADS // 2, LANES, 2 * LANES), F32),
        ],
        compiler_params=_cparams(("parallel", "arbitrary")), name="dsa",
    )(qT, iqT, iwT, k, vT, ikb, tri)


def _band_kernel(*refs, row_lo, row_hi, qpb):
    nkb = BAND_PAST // LANES + qpb
    q_ref = refs[0]
    k_refs = refs[1:1 + nkb]
    v_refs = refs[1 + nkb:1 + 2 * nkb]
    bias_ref, o_ref = refs[1 + 2 * nkb:]
    j = pl.program_id(1)
    kb = jnp.concatenate([r[...] for r in k_refs], axis=0)
    vb = jnp.concatenate([r[...] for r in v_refs], axis=0)
    lane = lax.broadcasted_iota(I32, (1, LANES), 1)
    lo_half = lane < HEAD_DIM
    zero = jnp.zeros((), BF16)
    for i in range(qpb):
        row = (j * qpb + i) * LANES + lax.broadcasted_iota(I32, (1, BAND_KEYS), 1)
        valid = jnp.where(row >= row_lo, row, row_hi) < row_hi
        qrows = slice(i * LANES, (i + 1) * LANES)
        krows = slice(i * LANES, i * LANES + BAND_KEYS)
        for pr in range(N_HEADS // 2):
            sl = slice(pr * LANES, (pr + 1) * LANES)
            qp = q_ref[qrows, sl]
            lhs = jnp.concatenate([jnp.where(lo_half, qp, zero), jnp.where(lo_half, zero, qp)], axis=0)
            s = lax.dot_general(lhs, kb[krows, sl], (((1,), (1,)), ((), ())),
                                preferred_element_type=F32)
            bias2 = jnp.concatenate([bias_ref[2 * pr], bias_ref[2 * pr + 1]], axis=0)
            s = jnp.where(valid, s + bias2, NEG)
            m = jnp.max(s, axis=-1, keepdims=True)
            pexp = jnp.exp(s - m)
            inv = 1.0 / jnp.sum(pexp, axis=-1, keepdims=True)
            o = jnp.dot(pexp.astype(BF16), vb[krows, sl], preferred_element_type=F32) * inv
            o_ref[qrows, sl] = jnp.where(lo_half, o[:LANES], o[LANES:]).astype(BF16)


def _band(q, kp, vp, bias, *, row_lo, row_hi):
    nb, tq, _ = q.shape
    qpb = 2 if tq % (2 * LANES) == 0 else 1
    nkb = BAND_PAST // LANES + qpb
    kspecs = [pl.BlockSpec((None, LANES, ATT_W), (lambda b, j, c=c: (b, j * qpb + c, 0)))
              for c in range(nkb)]
    kern = functools.partial(_band_kernel, row_lo=row_lo, row_hi=row_hi, qpb=qpb)
    qspec = pl.BlockSpec((None, qpb * LANES, ATT_W), lambda b, j: (b, j, 0))
    return pl.pallas_call(
        kern, out_shape=jax.ShapeDtypeStruct((nb, tq, ATT_W), BF16),
        grid=(nb, tq // (qpb * LANES)),
        in_specs=[qspec] + kspecs + kspecs + [pl.BlockSpec(bias.shape, lambda b, j: (0, 0, 0))],
        out_specs=qspec,
        compiler_params=_cparams(("parallel", "parallel")), name="band",
    )(q, *([kp] * nkb), *([vp] * nkb), bias)


def _merge_kernel(x_ref, oa_ref, ob_ref, gattn_ref, wg_ref, wa_ref, wb_ref, wo_ref,
                  gffn_ref, wpqT_ref, c1_ref, c2_ref,
                  x2_ref, h2T_ref, s1_ref, s2_ref):
    x = x_ref[...]
    ms = jnp.mean(x * x, axis=-1, keepdims=True)
    h = (x * lax.rsqrt(ms + EPS) * gattn_ref[...]).astype(BF16)
    ga = jax.nn.sigmoid(jnp.dot(h, wg_ref[:, :D_MODEL], preferred_element_type=F32))
    gb = jax.nn.sigmoid(jnp.dot(h, wg_ref[:, D_MODEL:], preferred_element_type=F32))
    ma = jnp.dot(oa_ref[...], wa_ref[...], preferred_element_type=F32)
    mb = jnp.dot(ob_ref[...], wb_ref[...], preferred_element_type=F32)
    merged = (ga * ma + gb * mb).astype(BF16)
    x2 = x + jnp.dot(merged, wo_ref[...], preferred_element_type=F32)
    x2_ref[...] = x2
    ms2 = jnp.mean(x2 * x2, axis=-1, keepdims=True)
    h2T = (x2 * lax.rsqrt(ms2 + EPS) * gffn_ref[...]).T.astype(BF16)
    h2T_ref[...] = h2T
    half = PEER_DQ // 2
    for hh in range(PEER_HEADS):
        qT = jnp.dot(wpqT_ref[hh * PEER_DQ:(hh + 1) * PEER_DQ, :], h2T,
                     preferred_element_type=F32).astype(BF16)
        s1_ref[hh] = jnp.dot(c1_ref[...], qT[:half], preferred_element_type=F32)
        s2_ref[hh] = jnp.dot(c2_ref[...], qT[half:], preferred_element_type=F32)


def _merge(x, oa, ob, gattn, wg, wa, wb, wo, gffn, wpqT, c1, c2):
    n = x.shape[0]
    tm = ROW_TILE
    row = lambda i: (i, 0)
    const = lambda i: (0, 0)
    return pl.pallas_call(
        _merge_kernel,
        out_shape=(
            jax.ShapeDtypeStruct((n, D_MODEL), F32),
            jax.ShapeDtypeStruct((D_MODEL, n), BF16),
            jax.ShapeDtypeStruct((PEER_HEADS, N_KEYS, n), F32),
            jax.ShapeDtypeStruct((PEER_HEADS, N_KEYS, n), F32),
        ),
        grid=(n // tm,),
        in_specs=[
            pl.BlockSpec((tm, D_MODEL), row),
            pl.BlockSpec((tm, ATT_W), row),
            pl.BlockSpec((tm, ATT_W), row),
            pl.BlockSpec((1, D_MODEL), const),
            pl.BlockSpec(wg.shape, const),
            pl.BlockSpec(wa.shape, const),
            pl.BlockSpec(wb.shape, const),
            pl.BlockSpec(wo.shape, const),
            pl.BlockSpec((1, D_MODEL), const),
            pl.BlockSpec(wpqT.shape, const),
            pl.BlockSpec(c1.shape, const),
            pl.BlockSpec(c2.shape, const),
        ],
        out_specs=(
            pl.BlockSpec((tm, D_MODEL), row),
            pl.BlockSpec((D_MODEL, tm), lambda i: (0, i)),
            pl.BlockSpec((PEER_HEADS, N_KEYS, tm), lambda i: (0, 0, i)),
            pl.BlockSpec((PEER_HEADS, N_KEYS, tm), lambda i: (0, 0, i)),
        ),
        compiler_params=_cparams(("parallel",)), name="merge",
    )(x, oa, ob, gattn, wg, wa, wb, wo, gffn, wpqT, c1, c2)


def _pack_rows(x):
    return pltpu.bitcast(x, jnp.uint32)


def _unpack_rows(x):
    return pltpu.bitcast(x, BF16)


def _top16(x, distinct):
    idx = lax.broadcasted_iota(I32, x.shape, 0)
    rank = jnp.full(x.shape, PEER_TOPK, I32)
    rows = []
    for r in range(PEER_TOPK):
        m = jnp.max(x, axis=0, keepdims=True)
        rows.append(m)
        hit = x == m
        if not distinct:
            hit = idx == jnp.min(jnp.where(hit, idx, N_KEYS), axis=0, keepdims=True)
        rank = jnp.where(hit, r, rank)
        x = jnp.where(hit, -jnp.inf, x)
    return jnp.concatenate(rows, axis=0), rank


def _select_kernel(s1_ref, s2_ref, cnt_ref, e1_ref, r2_ref, e2_ref):
    sub8 = lax.broadcasted_iota(I32, (8, 1), 0).astype(F32)
    sub16 = lax.broadcasted_iota(I32, (PEER_TOPK, 1), 0).astype(F32)

    def count(mask):
        return jnp.sum(jnp.where(mask, 1.0, 0.0), axis=0, keepdims=True)

    def head_body(hh, carry):
        s1 = s1_ref[hh]
        s2 = s2_ref[hh]
        v1, r1 = _top16(s1, True)
        v2, r2 = _top16(s2, True)
        ranked = jnp.maximum(count(r1 < PEER_TOPK), count(r2 < PEER_TOPK))
        v1, r1, v2, r2 = lax.cond(
            jnp.max(ranked) > PEER_TOPK,
            lambda: _top16(s1, False) + _top16(s2, False),
            lambda: (v1, r1, v2, r2))
        pieces = [v1[0:1] + v2]
        for a in range(1, 8):
            piece = v1[a:a + 1] + v2[0:8]
            pieces.append(jnp.where(sub8 < PEER_TOPK // (a + 1), piece, -jnp.inf))
        pieces.append(v1[8:16] + v2[0:1])
        cand = jnp.concatenate(pieces, axis=0)
        rem = jnp.full((1, cand.shape[1]), float(PEER_TOPK), F32)
        tau = jnp.zeros((1, cand.shape[1]), F32)
        x = cand
        for _ in range(PEER_TOPK):
            m = jnp.max(x, axis=0, keepdims=True)
            hit = x == m
            tau = jnp.where(rem > 0, m, tau)
            rem = rem - count(hit)
            x = jnp.where(hit, -jnp.inf, x)
        need = PEER_TOPK - count(cand > tau)
        kept = []
        for a in range(PEER_TOPK):
            row = pieces[a] if a < 8 else pieces[8][a - 8:a - 7]
            ties = count(row == tau)
            kept.append(count(row > tau) + jnp.minimum(ties, jnp.maximum(need, 0.0)))
            need = need - ties
        smax = v1[0:1] + v2[0:1]
        z = jnp.sum(jnp.where(sub16 < kept[0], jnp.exp(pieces[0] - smax), 0.0), axis=0, keepdims=True)
        for a in range(1, 8):
            z = z + jnp.sum(jnp.where(sub8 < kept[a], jnp.exp(pieces[a] - smax), 0.0),
                            axis=0, keepdims=True)
        tail = jnp.concatenate(kept[8:], axis=0)
        z = z + jnp.sum(jnp.where(tail > 0, jnp.exp(pieces[8] - smax), 0.0), axis=0, keepdims=True)
        cnt = jnp.zeros(s1.shape, F32)
        for a in range(PEER_TOPK):
            cnt = jnp.where(r1 == a, kept[a], cnt)
        cnt_ref[hh] = cnt
        e1_ref[hh] = jnp.exp(s1 - v1[0:1]) * (1.0 / z)
        r2b = r2.astype(F32).astype(BF16)
        e2b = jnp.exp(s2 - v2[0:1]).astype(BF16)
        for c in range(s2.shape[1] // LANES):
            cs = slice(c * LANES, (c + 1) * LANES)
            r2_ref[hh, c] = _pack_rows(r2b[:, cs])
            e2_ref[hh, c] = _pack_rows(e2b[:, cs])
        return carry

    lax.fori_loop(0, PEER_HEADS, head_body, 0)


def _select(s1, s2):
    n = s1.shape[-1]
    tn = SEL_TILE
    blk = pl.BlockSpec((PEER_HEADS, N_KEYS, tn), lambda i: (0, 0, i))
    packed = jax.ShapeDtypeStruct((PEER_HEADS, n // LANES, N_KEYS // 2, LANES), jnp.uint32)
    pblk = pl.BlockSpec((PEER_HEADS, tn // LANES, N_KEYS // 2, LANES), lambda i: (0, i, 0, 0))
    return pl.pallas_call(
        _select_kernel,
        out_shape=(
            jax.ShapeDtypeStruct(s1.shape, F32),
            jax.ShapeDtypeStruct(s1.shape, F32),
            packed, packed,
        ),
        grid=(n // tn,),
        in_specs=[blk, blk],
        out_specs=(blk, blk, pblk, pblk),
        compiler_params=_cparams(("parallel",)), name="select",
    )(s1, s2)


def _peer_kernel(h2T_ref, u_ref, vT_ref, vTp_ref, cnt_ref, e1_ref, r2_ref, e2_ref, x2_ref,
                 y_ref, acc_ref, wa0_ref, wa1_ref, z0_ref, z1_ref, rowc_ref, rowe_ref):
    e = pl.program_id(1)
    zero = jnp.minimum(e, 0)
    z_refs = (z0_ref, z1_ref)
    wa_refs = (wa0_ref, wa1_ref)

    @pl.when(e == 0)
    def _():
        acc_ref[...] = jnp.zeros_like(acc_ref)
        wa_refs[(PEER_EB // PEER_GRP - 1) % 2][...] = jnp.zeros_like(wa1_ref)

    tn = h2T_ref.shape[1]
    nsub = PEER_EB // N_KEYS
    reps = N_KEYS // BF16_ROWS
    for sub in range(nsub):
        for hh in range(PEER_HEADS):
            rowc_ref[sub, hh] = _pack_rows(
                jnp.broadcast_to(cnt_ref[hh, sub:sub + 1, :], (BF16_ROWS, tn)).astype(BF16))
            rowe_ref[sub, hh] = _pack_rows(
                jnp.broadcast_to(e1_ref[hh, sub:sub + 1, :], (BF16_ROWS, tn)).astype(BF16))

    ngrp = PEER_EB // PEER_GRP

    def scores(g, slot):
        start = g * PEER_GRP if isinstance(g, int) else pl.multiple_of(g * PEER_GRP, PEER_GRP)
        z_refs[slot][0] = jnp.dot(u_ref[pl.ds(start, PEER_GRP), :], h2T_ref[...],
                                  preferred_element_type=F32)

    def weights(g, slot):
        for s in range(PEER_GRP // N_KEYS):
            sub = g * (PEER_GRP // N_KEYS) + s
            for c in range(tn // LANES):
                cs = slice(c * LANES, (c + 1) * LANES)
                zc = z_refs[slot][zero, s * N_KEYS:(s + 1) * N_KEYS, cs]
                zb = zc.astype(BF16)
                a = (0.5 * zb) * (1.0 + lax.erf(zb * (2.0 ** -0.5)))
                w = jnp.zeros((N_KEYS, LANES), BF16)
                for hh in range(PEER_HEADS):
                    cnt = jnp.concatenate([_unpack_rows(rowc_ref[sub + zero, hh, :, cs])] * reps, axis=0)
                    e1 = jnp.concatenate([_unpack_rows(rowe_ref[sub + zero, hh, :, cs])] * reps, axis=0)
                    w = w + jnp.where(_unpack_rows(r2_ref[hh, c]) < cnt,
                                      _unpack_rows(e2_ref[hh, c]) * e1, jnp.zeros((), BF16))
                wa_refs[slot][0, s * (N_KEYS // 2):(s + 1) * (N_KEYS // 2), cs] = _pack_rows(w * a)

    def down(vt, slot):
        acc_ref[...] += jnp.dot(vt, _unpack_rows(wa_refs[slot][zero]), preferred_element_type=F32)

    last = (ngrp - 1) % 2
    scores(0, 0)
    down(vTp_ref[0], last)
    for g in range(ngrp):
        if g + 1 < ngrp:
            scores(g + 1, (g + 1) % 2)
        weights(g, g % 2)
        if g > 0:
            down(vT_ref[g - 1], (g - 1) % 2)

    @pl.when(e == pl.num_programs(1) - 1)
    def _():
        down(vT_ref[ngrp - 1], last)
        y_ref[...] = x2_ref[...] + acc_ref[...].T


def _peer(h2T, u, vT, cnt, e1, r2, e2, x2):
    n = x2.shape[0]
    tn = min(PEER_TOK, n)
    nsub = PEER_EB // N_KEYS
    tok3 = pl.BlockSpec((PEER_HEADS, tn // LANES, N_KEYS // 2, LANES), lambda i, e: (0, i, 0, 0))
    key1 = pl.BlockSpec((PEER_HEADS, nsub, tn), lambda i, e: (0, e, i))
    return pl.pallas_call(
        _peer_kernel,
        out_shape=jax.ShapeDtypeStruct((n, D_MODEL), F32),
        grid=(n // tn, N_EXPERTS // PEER_EB),
        in_specs=[
            pl.BlockSpec((D_MODEL, tn), lambda i, e: (0, i)),
            pl.BlockSpec((PEER_EB, D_MODEL), lambda i, e: (e, 0)),
            pl.BlockSpec((PEER_EB // PEER_GRP, D_MODEL, PEER_GRP), lambda i, e: (e, 0, 0)),
            pl.BlockSpec((1, D_MODEL, PEER_GRP),
                         lambda i, e: (jnp.maximum(e * (PEER_EB // PEER_GRP) - 1, 0), 0, 0)),
            key1, key1, tok3, tok3,
            pl.BlockSpec((tn, D_MODEL), lambda i, e: (i, 0)),
        ],
        out_specs=pl.BlockSpec((tn, D_MODEL), lambda i, e: (i, 0)),
        scratch_shapes=[
            pltpu.VMEM((D_MODEL, tn), F32),
            pltpu.VMEM((1, PEER_GRP // 2, tn), jnp.uint32),
            pltpu.VMEM((1, PEER_GRP // 2, tn), jnp.uint32),
            pltpu.VMEM((1, PEER_GRP, tn), F32),
            pltpu.VMEM((1, PEER_GRP, tn), F32),
            pltpu.VMEM((nsub, PEER_HEADS, BF16_ROWS // 2, tn), jnp.uint32),
            pltpu.VMEM((nsub, PEER_HEADS, BF16_ROWS // 2, tn), jnp.uint32),
        ],
        compiler_params=_cparams(("parallel", "arbitrary")), name="peer",
    )(h2T, u, vT, vT, cnt, e1, r2, e2, x2)


def _rope_tables(pos):
    posf = pos.astype(F32)[:, None]
    t = pos.shape[0]

    def tab(half, reps):
        inv = jnp.power(ROPE_THETA, -jnp.arange(half, dtype=F32) / half)
        ang = posf * inv[None, :]
        cos = jnp.cos(ang)
        sin = jnp.sin(ang)
        return (jnp.tile(jnp.concatenate([cos, cos], axis=1), (1, reps)),
                jnp.tile(jnp.concatenate([-sin, sin], axis=1), (1, reps)))

    cosq, sinq = tab(HEAD_DIM // 2, N_HEADS)
    cosi, sini = tab(IDX_DIM // 2, IDX_HEADS + 1)
    pad = IDX_W - cosi.shape[1]
    cosi = jnp.concatenate([cosi, jnp.ones((t, pad), F32)], axis=1)
    sini = jnp.concatenate([sini, jnp.zeros((t, pad), F32)], axis=1)
    return cosq, sinq, cosi, sini


def _band_bias(rel_bias):
    qi = np.arange(LANES)[:, None]
    kj = np.arange(BAND_KEYS)[None, :]
    ok = (kj // CHUNK >= qi // CHUNK) & (kj // CHUNK <= qi // CHUNK + BAND_PAST // CHUNK)
    n = BAND_KEYS + LANES - 1
    m = np.arange(n)
    diag = rel_bias.astype(F32)[:, np.clip(BAND_PAST + LANES - 1 - m, -REL_CLIP, REL_CLIP) + REL_CLIP]
    wrapped = jnp.tile(jnp.pad(diag, ((0, 0), (0, 1))), (1, LANES))[:, :LANES * n]
    toep = wrapped.reshape(-1, LANES, n)[:, :, LANES - 1:LANES - 1 + BAND_KEYS]
    return jnp.where(jnp.asarray(ok)[None], toep, NEG)


def _layer_weights(g_attn, w_in, g_qa, g_ka, g_ik, g_qb, g_kb, rel_bias, w_br_a, w_br_b,
                   w_out, g_ffn, w_pq, c1, c2, u, v):
    a0 = 3 * ATT_W
    i0 = a0 + IDX_HEADS * IDX_DIM + IDX_DIM + IDX_HEADS
    b0 = i0 + 3 * ATT_W
    idx_cols = i0 - a0
    w1 = jnp.concatenate(
        [w_in[:, :a0], w_in[:, i0:b0], w_in[:, a0:i0],
         jnp.zeros((D_MODEL, IDX_W - idx_cols), w_in.dtype)], axis=1).astype(BF16)
    wg = w_in[:, b0:].astype(BF16)
    hd = jnp.asarray(np.kron(np.eye(N_HEADS), np.ones((HEAD_DIM, HEAD_DIM))) / HEAD_DIM, BF16)
    gh = jnp.stack([jnp.tile(g, N_HEADS) for g in (g_qa, g_ka, g_qb, g_kb)]).astype(F32)
    gidx = jnp.concatenate(
        [g_ik.astype(F32), jnp.full((IDX_HEADS,), IDX_HEADS ** -0.5, F32),
         jnp.ones((LANES - IDX_DIM - IDX_HEADS,), F32)])[None, :]
    tri = jnp.asarray(np.tril(np.ones((KEY_TILE, KEY_TILE)), -1), BF16)
    return dict(
        w1=w1, wg=wg, hd=hd, gh=gh, gidx=gidx, tri=tri,
        gattn=g_attn.astype(F32)[None, :], gffn=g_ffn.astype(F32)[None, :],
        bias=_band_bias(rel_bias),
        wa=w_br_a.astype(BF16), wb=w_br_b.astype(BF16), wo=w_out.astype(BF16),
        wpqT=w_pq.T.astype(BF16), c1=c1.astype(BF16), c2=c2.astype(BF16),
        u=u.astype(BF16),
        vT=v.astype(BF16).reshape(N_EXPERTS // PEER_GRP, PEER_GRP, D_MODEL).transpose(0, 2, 1))


def _ffn(x, oa, ob, w):
    x2, h2T, s1, s2 = _merge(x, oa, ob, w["gattn"], w["wg"], w["wa"], w["wb"], w["wo"],
                             w["gffn"], w["wpqT"], w["c1"], w["c2"])
    cnt, e1, r2, e2 = _select(s1, s2)
    return _peer(h2T, w["u"], w["vT"], cnt, e1, r2, e2, x2)


def _prompt_layer(x, w):
    b, t, _ = x.shape
    xf = x.reshape(b * t, D_MODEL)
    tabs = _rope_tables(jnp.arange(t))
    (aqT, ak, akb, av, avT, iqT, ik, ikb, iwT, bq, bk, bkb, bv, bvb) = _proj(
        xf, b, t, w["w1"], w["gattn"], w["hd"], w["gh"], w["gidx"], *tabs)
    oa = _dsa(aqT, iqT, iwT, akb.reshape(b, t, ATT_W), avT, ikb.reshape(b, t, LANES),
              w["tri"], q0=0, n_valid=t)
    front = ((0, 0), (BAND_PAST, 0), (0, 0))
    ob = _band(bq.reshape(b, t, ATT_W), jnp.pad(bkb.reshape(b, t, ATT_W), front),
               jnp.pad(bvb.reshape(b, t, ATT_W), front), w["bias"],
               row_lo=BAND_PAST, row_hi=BAND_PAST + t)
    y = _ffn(xf, oa.reshape(b * t, ATT_W), ob.reshape(b * t, ATT_W), w)
    keep = min(BAND_PAST, t)
    heads = lambda a: a.reshape(b, t, N_HEADS, HEAD_DIM)
    return y.reshape(b, t, D_MODEL), (
        heads(ak), heads(av), ik.reshape(b, t, IDX_DIM),
        heads(bk)[:, t - keep:], heads(bv)[:, t - keep:])


def _sample_layer(x, ca_k, ca_v, ca_ik, cb_k, cb_v, w):
    b, t, _ = x.shape
    n = b * t
    past = ca_k.shape[1]
    xf = x.reshape(n, D_MODEL)
    tabs = _rope_tables(jnp.tile(past + jnp.arange(t), b))
    (aqT, ak, akb, av, avT, iqT, ik, ikb, iwT, bq, bk, bkb, bv, bvb) = _proj(
        xf, 1, n, w["w1"], w["gattn"], w["hd"], w["gh"], w["gidx"], *tabs)

    def per_seq_T(aT, rows):
        a = aT[0].T.reshape(b, t, rows)
        return jnp.pad(a, ((0, 0), (0, LANES - t), (0, 0))).transpose(0, 2, 1)

    lk = past + t
    lp = -(-lk // KEY_TILE) * KEY_TILE
    padk = ((0, 0), (0, lp - lk), (0, 0))
    k_all = jnp.pad(jnp.concatenate(
        [ca_k.reshape(b, past, ATT_W).astype(BF16), akb.reshape(b, t, ATT_W)], axis=1), padk)
    v_all = jnp.pad(jnp.concatenate(
        [ca_v.reshape(b, past, ATT_W).astype(BF16), av.astype(BF16).reshape(b, t, ATT_W)],
        axis=1), padk)
    vT_all = v_all.reshape(b, lp // KEY_TILE, KEY_TILE, ATT_W).transpose(0, 1, 3, 2)
    ik_all = jnp.pad(jnp.concatenate(
        [jnp.pad(ca_ik.astype(BF16), ((0, 0), (0, 0), (0, LANES - IDX_DIM))),
         ikb.reshape(b, t, LANES)], axis=1), padk)
    oa = _dsa(per_seq_T(aqT, ATT_W), per_seq_T(iqT, 2 * LANES), per_seq_T(iwT, IDX_HEADS),
              k_all, vT_all, ik_all, w["tri"], q0=past, n_valid=lk)[:, :t]

    pb = cb_k.shape[1]
    padb = ((0, 0), (BAND_PAST - pb, LANES - t), (0, 0))
    qpad = jnp.pad(bq.reshape(b, t, ATT_W), ((0, 0), (0, LANES - t), (0, 0)))
    kb_all = jnp.pad(jnp.concatenate(
        [cb_k.reshape(b, pb, ATT_W).astype(BF16), bkb.reshape(b, t, ATT_W)], axis=1), padb)
    vb_all = jnp.pad(jnp.concatenate(
        [cb_v.reshape(b, pb, ATT_W).astype(BF16), bvb.reshape(b, t, ATT_W)], axis=1), padb)
    ob = _band(qpad, kb_all, vb_all, w["bias"],
               row_lo=BAND_PAST - pb, row_hi=BAND_PAST + t)[:, :t]

    y = _ffn(xf, oa.reshape(n, ATT_W), ob.reshape(n, ATT_W), w)
    keep = min(BAND_PAST, t)
    heads = lambda a: a.reshape(b, t, N_HEADS, HEAD_DIM)
    return y.reshape(b, t, D_MODEL), (
        heads(ak), heads(av), ik.reshape(b, t, IDX_DIM),
        heads(bk)[:, t - keep:], heads(bv)[:, t - keep:])


def kernel(x_prompt, x_sample, cache_a_k, cache_a_v, cache_a_ik, cache_b_k, cache_b_v,
           g_attn, w_in, g_qa, g_ka, g_ik, g_qb, g_kb, rel_bias, w_br_a, w_br_b, w_out,
           g_ffn, w_pq, peer_c1, peer_c2, peer_u, peer_v):
    depth = w_in.shape[0]
    xp, xs = x_prompt, x_sample
    sp, ss = [], []
    for l in range(depth):
        w = _layer_weights(g_attn[l], w_in[l], g_qa[l], g_ka[l], g_ik[l], g_qb[l], g_kb[l],
                           rel_bias[l], w_br_a[l], w_br_b[l], w_out[l], g_ffn[l], w_pq[l],
                           peer_c1[l], peer_c2[l], peer_u[l], peer_v[l])
        xp, st_p = _prompt_layer(xp, w)
        xs, st_s = _sample_layer(xs, cache_a_k[l], cache_a_v[l], cache_a_ik[l],
                                 cache_b_k[l], cache_b_v[l], w)
        sp.append(st_p)
        ss.append(st_s)
    stack = lambda sts, i: jnp.stack([s[i] for s in sts])
    return (xp, xs) + tuple(stack(sp, i) for i in range(5)) + tuple(stack(ss, i) for i in range(5))
```

```python
import functools

import jax
import jax.numpy as jnp
import numpy as np
from jax import lax
from jax.experimental import pallas as pl
from jax.experimental.pallas import tpu as pltpu

F32 = jnp.float32
BF16 = jnp.bfloat16
I32 = jnp.int32

D_MODEL = 1024
HEAD_DIM = 64
N_HEADS = 8
ATT_W = N_HEADS * HEAD_DIM
IDX_HEADS = 8
IDX_DIM = 32
IDX_W = 384
CHUNK = 64
TOPK_MAX = 256
BAND_PAST = 512
BAND_KEYS = BAND_PAST + 128
REL_CLIP = 128
PEER_HEADS = 8
PEER_DQ = 256
N_KEYS = 128
N_EXPERTS = N_KEYS * N_KEYS
PEER_TOPK = 16
ROPE_THETA = 10000.0
EPS = 1e-6
NEG = -1e30
INT_MIN = -(2 ** 31)

LANES = 128
BF16_ROWS = 16
KEY_TILE = 256
ROW_TILE = 512
SEL_TILE = 256
PEER_TOK = 1024
PEER_EB = 2048
PEER_GRP = 256
VMEM_LIMIT = 56 * 1024 * 1024


def _cparams(sem):
    return pltpu.CompilerParams(dimension_semantics=sem, vmem_limit_bytes=VMEM_LIMIT)


def _rope_chunks(y, cos, sin, half):
    lane = lax.broadcasted_iota(I32, (1, LANES), 1)
    lo = (lane % (2 * half)) < half
    out = []
    for c in range(y.shape[1] // LANES):
        sl = slice(c * LANES, (c + 1) * LANES)
        yc = y[:, sl]
        partner = jnp.where(lo, pltpu.roll(yc, LANES - half, 1), pltpu.roll(yc, half, 1))
        out.append(yc * cos[:, sl] + partner * sin[:, sl])
    return jnp.concatenate(out, axis=1)


def _proj_kernel(x_ref, gattn_ref, w1_ref, hd_ref, gh_ref, gidx_ref,
                 cosq_ref, sinq_ref, cosi_ref, sini_ref,
                 aqT_ref, ak_ref, akb_ref, av_ref, avT_ref, iqT_ref, ik_ref, ikb_ref,
                 iwT_ref, bq_ref, bk_ref, bkb_ref, bv_ref, bvb_ref):
    x = x_ref[...]
    ms = jnp.mean(x * x, axis=-1, keepdims=True)
    h = (x * lax.rsqrt(ms + EPS) * gattn_ref[...]).astype(BF16)

    def seg(i):
        return jnp.dot(h, w1_ref[:, i * ATT_W:(i + 1) * ATT_W], preferred_element_type=F32)

    def headnorm(y, gi):
        hms = jnp.dot((y * y).astype(BF16), hd_ref[...], preferred_element_type=F32)
        return y * lax.rsqrt(hms + EPS) * gh_ref[gi:gi + 1, :]

    cosq = cosq_ref[...]
    sinq = sinq_ref[...]

    aq = _rope_chunks(headnorm(seg(0), 0), cosq, sinq, HEAD_DIM // 2) * (HEAD_DIM ** -0.5)
    aqT_ref[...] = aq.T.astype(BF16)

    ak = _rope_chunks(headnorm(seg(1), 1), cosq, sinq, HEAD_DIM // 2)
    ak_ref[...] = ak
    akb_ref[...] = ak.astype(BF16)

    av = seg(2)
    av_ref[...] = av
    for c in range(av.shape[0] // KEY_TILE):
        avT_ref[c] = av[c * KEY_TILE:(c + 1) * KEY_TILE, :].T.astype(BF16)

    bq_ref[...] = (headnorm(seg(3), 2) * (HEAD_DIM ** -0.5)).astype(BF16)
    bk = headnorm(seg(4), 3)
    bk_ref[...] = bk
    bkb_ref[...] = bk.astype(BF16)
    bv = seg(5)
    bv_ref[...] = bv
    bvb_ref[...] = bv.astype(BF16)

    s = jnp.dot(h, w1_ref[:, 6 * ATT_W:6 * ATT_W + IDX_W], preferred_element_type=F32)
    lane = lax.broadcasted_iota(I32, (1, LANES), 1)
    is_ik = lane < IDX_DIM
    c2 = s[:, 2 * LANES:3 * LANES]
    ikms = jnp.sum(jnp.where(is_ik, c2 * c2, 0.0), axis=-1, keepdims=True) * (1.0 / IDX_DIM)
    c2n = c2 * jnp.where(is_ik, lax.rsqrt(ikms + EPS), 1.0) * gidx_ref[...]
    sn = jnp.concatenate([s[:, :2 * LANES], c2n], axis=1)
    r = _rope_chunks(sn, cosi_ref[...], sini_ref[...], IDX_DIM // 2)
    iqT_ref[...] = r[:, :2 * LANES].T.astype(BF16)
    c2r = r[:, 2 * LANES:]
    ik_ref[...] = c2r[:, :IDX_DIM]
    ikb_ref[...] = jnp.where(is_ik, c2r, 0.0).astype(BF16)
    iwT_ref[...] = c2r.T[IDX_DIM:IDX_DIM + IDX_HEADS, :]


def _proj(x, nb, tb, w1, gattn, hd, gh, gidx, cosq, sinq, cosi, sini):
    n = x.shape[0]
    tm = ROW_TILE
    tpb = tb // tm
    row = lambda i: (i, 0)
    tab = lambda i: (i % tpb, 0)
    const = lambda i: (0, 0)
    grp = lambda i: (i // tpb, 0, i % tpb)
    out_shape = (
        jax.ShapeDtypeStruct((nb, ATT_W, tb), BF16),
        jax.ShapeDtypeStruct((n, ATT_W), F32),
        jax.ShapeDtypeStruct((n, ATT_W), BF16),
        jax.ShapeDtypeStruct((n, ATT_W), F32),
        jax.ShapeDtypeStruct((nb, tb // KEY_TILE, ATT_W, KEY_TILE), BF16),
        jax.ShapeDtypeStruct((nb, 2 * LANES, tb), BF16),
        jax.ShapeDtypeStruct((n, IDX_DIM), F32),
        jax.ShapeDtypeStruct((n, LANES), BF16),
        jax.ShapeDtypeStruct((nb, IDX_HEADS, tb), F32),
        jax.ShapeDtypeStruct((n, ATT_W), BF16),
        jax.ShapeDtypeStruct((n, ATT_W), F32),
        jax.ShapeDtypeStruct((n, ATT_W), BF16),
        jax.ShapeDtypeStruct((n, ATT_W), F32),
        jax.ShapeDtypeStruct((n, ATT_W), BF16),
    )
    out_specs = (
        pl.BlockSpec((None, ATT_W, tm), grp),
        pl.BlockSpec((tm, ATT_W), row),
        pl.BlockSpec((tm, ATT_W), row),
        pl.BlockSpec((tm, ATT_W), row),
        pl.BlockSpec((None, tm // KEY_TILE, ATT_W, KEY_TILE),
                     lambda i: (i // tpb, i % tpb, 0, 0)),
        pl.BlockSpec((None, 2 * LANES, tm), grp),
        pl.BlockSpec((tm, IDX_DIM), row),
        pl.BlockSpec((tm, LANES), row),
        pl.BlockSpec((None, IDX_HEADS, tm), grp),
        pl.BlockSpec((tm, ATT_W), row),
        pl.BlockSpec((tm, ATT_W), row),
        pl.BlockSpec((tm, ATT_W), row),
        pl.BlockSpec((tm, ATT_W), row),
        pl.BlockSpec((tm, ATT_W), row),
    )
    in_specs = [
        pl.BlockSpec((tm, D_MODEL), row),
        pl.BlockSpec((1, D_MODEL), const),
        pl.BlockSpec(w1.shape, const),
        pl.BlockSpec(hd.shape, const),
        pl.BlockSpec(gh.shape, const),
        pl.BlockSpec(gidx.shape, const),
        pl.BlockSpec((tm, ATT_W), tab),
        pl.BlockSpec((tm, ATT_W), tab),
        pl.BlockSpec((tm, IDX_W), tab),
        pl.BlockSpec((tm, IDX_W), tab),
    ]
    return pl.pallas_call(
        _proj_kernel, out_shape=out_shape, grid=(n // tm,),
        in_specs=in_specs, out_specs=out_specs,
        compiler_params=_cparams(("parallel",)), name="proj",
    )(x, gattn, w1, hd, gh, gidx, cosq, sinq, cosi, sini)


def _dsa_kernel(qT_ref, iqT_ref, iwT_ref, k_ref, vT_ref, ikb_ref, tri_ref,
                o_ref, sc_ref, am_ref, s_ref, oT_ref, acc_ref, *, q0, n_valid, n_key_tiles, n_sel):
    j = pl.program_id(1)
    qbase = q0 + j * LANES
    n_t = jnp.minimum(lax.shift_right_logical(qbase + LANES + KEY_TILE - 1, 8), n_key_tiles)
    qchunk = lax.shift_right_logical(
        qbase + lax.broadcasted_iota(I32, (1, LANES), 1), 6)

    def rows(t):
        return pl.ds(pl.multiple_of(t * KEY_TILE, KEY_TILE), KEY_TILE)

    def admissible(t):
        kidx = t * KEY_TILE + lax.broadcasted_iota(I32, (KEY_TILE, LANES), 0)
        return jnp.where(kidx < n_valid, lax.shift_right_logical(kidx, 6), 1 << 30) <= qchunk

    def fold8(v):
        return v.reshape(KEY_TILE // 8, 8, v.shape[-1])

    zpad = jnp.zeros((LANES - IDX_DIM, LANES), BF16)
    rhs_idx = jnp.concatenate(
        [jnp.concatenate([iqT_ref[h * IDX_DIM:(h + 1) * IDX_DIM, :], zpad], axis=0)
         for h in range(IDX_HEADS)], axis=1)

    def score_body(t, carry, width):
        both = pl.ds(pl.multiple_of(t * KEY_TILE, KEY_TILE), width * KEY_TILE)
        res = jnp.dot(ikb_ref[both, :], rhs_idx, preferred_element_type=F32)
        for w in range(width):
            part = res[w * KEY_TILE:(w + 1) * KEY_TILE]
            acc = jnp.zeros((KEY_TILE, LANES), F32)
            for h in range(IDX_HEADS):
                acc = acc + jnp.maximum(part[:, h * LANES:(h + 1) * LANES], 0.0) * iwT_ref[h:h + 1, :]
            sc_ref[rows(t + w), :] = jnp.where(admissible(t + w), acc, NEG)
        return carry

    n_pairs = lax.shift_right_logical(n_t, 1)
    lax.fori_loop(0, n_pairs, lambda i, c: score_body(2 * i, c, 2), 0)
    lax.fori_loop(2 * n_pairs, n_t, lambda t, c: score_body(t, c, 1), 0)
    sc_ref[rows(n_t), :] = jnp.full((KEY_TILE, LANES), -jnp.inf, F32)

    def count(pred):
        def body(t2, c8):
            two = pl.ds(pl.multiple_of(t2 * (2 * KEY_TILE), 2 * KEY_TILE), 2 * KEY_TILE)
            hits = jnp.where(pred(sc_ref[two, :]), 1, 0)
            return c8 + jnp.sum(hits.reshape(2 * KEY_TILE // 8, 8, LANES), axis=0)
        c8 = lax.fori_loop(0, lax.shift_right_logical(n_t + 1, 1), body, jnp.zeros((8, LANES), I32))
        return jnp.sum(c8, axis=0, keepdims=True)

    def bisect_body(it, p):
        c = p + lax.shift_left(jnp.int32(1), 31 - it)
        cf = pltpu.bitcast(jnp.where(c >= 0, c, c ^ 0x7FFFFFFF), F32)
        return jnp.where(count(lambda v: v >= cf) >= n_sel, c, p)

    p = lax.fori_loop(0, 32, bisect_body, jnp.full((1, LANES), INT_MIN, I32))
    thr = pltpu.bitcast(jnp.where(p >= 0, p, p ^ 0x7FFFFFFF), F32)

    def mask_simple():
        def body(t, carry):
            keep = jnp.where(sc_ref[rows(t), :] >= thr, 0.0, NEG)
            am_ref[rows(t), :] = jnp.where(admissible(t), keep, NEG)
            return carry
        lax.fori_loop(0, n_t, body, 0)

    def mask_ties():
        need = (n_sel - count(lambda v: v > thr)).astype(F32)

        def body(t, carry):
            v = sc_ref[rows(t), :]
            eq = v == thr
            eqf = jnp.where(eq, 1.0, 0.0)
            pre = jnp.dot(tri_ref[...], eqf.astype(BF16), preferred_element_type=F32) + carry
            tie = jnp.where(eq, jnp.where(pre < need, 0.0, NEG), NEG)
            keep = jnp.where(v > thr, 0.0, tie)
            am_ref[rows(t), :] = jnp.where(admissible(t), keep, NEG)
            return carry + jnp.sum(jnp.sum(fold8(eqf), axis=0), axis=0, keepdims=True)
        lax.fori_loop(0, n_t, body, jnp.zeros((1, LANES), F32))

    lax.cond(jnp.max(count(lambda v: v >= thr)) == n_sel, mask_simple, mask_ties)

    npair = N_HEADS // 2
    pw = 2 * LANES
    zq = jnp.zeros((HEAD_DIM, LANES), BF16)
    rhs = [jnp.concatenate(
        [jnp.concatenate([qT_ref[pr * LANES:pr * LANES + HEAD_DIM, :], zq], axis=1),
         jnp.concatenate([zq, qT_ref[pr * LANES + HEAD_DIM:(pr + 1) * LANES, :]], axis=1)], axis=0)
        for pr in range(npair)]

    n_two = lax.shift_right_logical(n_t, 1)

    def span(t, width):
        return pl.ds(pl.multiple_of(t * KEY_TILE, KEY_TILE), width * KEY_TILE)

    def logit_body(t, m8, width):
        am = am_ref[span(t, width), :]
        am2 = jnp.concatenate([am, am], axis=1)
        tops = []
        for pr in range(npair):
            s = jnp.dot(k_ref[span(t, width), pr * LANES:(pr + 1) * LANES], rhs[pr],
                        preferred_element_type=F32) + am2
            s_ref[span(t, width), pr * pw:(pr + 1) * pw] = s
            tops.append(jnp.max(s.reshape(width * KEY_TILE // 8, 8, pw), axis=0))
        return jnp.maximum(m8, jnp.concatenate(tops, axis=1))

    m8 = lax.fori_loop(0, n_two, lambda i, c: logit_body(2 * i, c, 2),
                       jnp.full((8, npair * pw), NEG, F32))
    m8 = lax.fori_loop(2 * n_two, n_t, lambda t, c: logit_body(t, c, 1), m8)
    m = jnp.max(m8, axis=0, keepdims=True)
    acc_ref[...] = jnp.zeros_like(acc_ref)

    def pv_body(t, l8, width):
        sums = []
        for pr in range(npair):
            cols = slice(pr * pw, (pr + 1) * pw)
            pexp = jnp.exp(s_ref[span(t, width), cols] - m[:, cols])
            sums.append(jnp.sum(pexp.reshape(width * KEY_TILE // 8, 8, pw), axis=0))
            pb = pexp.astype(BF16)
            out = acc_ref[pr]
            for w in range(width):
                out = out + jnp.dot(vT_ref[t + w, pr * LANES:(pr + 1) * LANES, :],
                                    pb[w * KEY_TILE:(w + 1) * KEY_TILE],
                                    preferred_element_type=F32)
            acc_ref[pr] = out
        return l8 + jnp.concatenate(sums, axis=1)

    l8 = lax.fori_loop(0, n_two, lambda i, c: pv_body(2 * i, c, 2),
                       jnp.zeros((8, npair * pw), F32))
    l8 = lax.fori_loop(2 * n_two, n_t, lambda t, c: pv_body(t, c, 1), l8)
    inv = 1.0 / jnp.sum(l8, axis=0, keepdims=True)
    for pr in range(npair):
        lo = pr * LANES
        acc = acc_ref[pr]
        oT_ref[lo:lo + HEAD_DIM, :] = acc[:HEAD_DIM, :LANES] * inv[:, pr * pw:pr * pw + LANES]
        oT_ref[lo + HEAD_DIM:lo + LANES, :] = acc[HEAD_DIM:, LANES:] * inv[:, pr * pw + LANES:(pr + 1) * pw]

    o_ref[...] = oT_ref[...].T.astype(BF16)


def _dsa(qT, iqT, iwT, k, vT, ikb, tri, *, q0, n_valid):
    nb, _, tq = qT.shape
    lp = k.shape[1]
    qblk = lambda b, j: (b, 0, j)
    seq3 = lambda b, j: (b, 0, 0)
    seq4 = lambda b, j: (b, 0, 0, 0)
    kern = functools.partial(_dsa_kernel, q0=q0, n_valid=n_valid, n_key_tiles=lp // KEY_TILE,
                             n_sel=min(TOPK_MAX, n_valid // 4))
    return pl.pallas_call(
        kern, out_shape=jax.ShapeDtypeStruct((nb, tq, ATT_W), BF16),
        grid=(nb, tq // LANES),
        in_specs=[
            pl.BlockSpec((None, ATT_W, LANES), qblk),
            pl.BlockSpec((None, 2 * LANES, LANES), qblk),
            pl.BlockSpec((None, IDX_HEADS, LANES), qblk),
            pl.BlockSpec((None, lp, ATT_W), seq3),
            pl.BlockSpec((None, lp // KEY_TILE, ATT_W, KEY_TILE), seq4),
            pl.BlockSpec((None, lp, LANES), seq3),
            pl.BlockSpec((KEY_TILE, KEY_TILE), lambda b, j: (0, 0)),
        ],
        out_specs=pl.BlockSpec((None, LANES, ATT_W), lambda b, j: (b, j, 0)),
        scratch_shapes=[
            pltpu.VMEM((lp + KEY_TILE, LANES), F32),
            pltpu.VMEM((lp, LANES), F32),
            pltpu.VMEM((lp, N_HEADS * LANES), F32),
            pltpu.VMEM((ATT_W, LANES), F32),
            pltpu.VMEM((N_HEADS // 2, LANES, 2 * LANES), F32),
        ],
        compiler_params=_cparams(("parallel", "arbitrary")), name="dsa",
    )(qT, iqT, iwT, k, vT, ikb, tri)


def _band_kernel(*refs, row_lo, row_hi, qpb):
    nkb = BAND_PAST // LANES + qpb
    q_ref = refs[0]
    k_refs = refs[1:1 + nkb]
    v_refs = refs[1 + nkb:1 + 2 * nkb]
    bias_ref, o_ref = refs[1 + 2 * nkb:]
    j = pl.program_id(1)
    kb = jnp.concatenate([r[...] for r in k_refs], axis=0)
    vb = jnp.concatenate([r[...] for r in v_refs], axis=0)
    lane = lax.broadcasted_iota(I32, (1, LANES), 1)
    lo_half = lane < HEAD_DIM
    zero = jnp.zeros((), BF16)
    for i in range(qpb):
        row = (j * qpb + i) * LANES + lax.broadcasted_iota(I32, (1, BAND_KEYS), 1)
        valid = jnp.where(row >= row_lo, row, row_hi) < row_hi
        qrows = slice(i * LANES, (i + 1) * LANES)
        krows = slice(i * LANES, i * LANES + BAND_KEYS)
        for pr in range(N_HEADS // 2):
            sl = slice(pr * LANES, (pr + 1) * LANES)
            qp = q_ref[qrows, sl]
            lhs = jnp.concatenate([jnp.where(lo_half, qp, zero), jnp.where(lo_half, zero, qp)], axis=0)
            s = lax.dot_general(lhs, kb[krows, sl], (((1,), (1,)), ((), ())),
                                preferred_element_type=F32)
            bias2 = jnp.concatenate([bias_ref[2 * pr], bias_ref[2 * pr + 1]], axis=0)
            s = jnp.where(valid, s + bias2, NEG)
            m = jnp.max(s, axis=-1, keepdims=True)
            pexp = jnp.exp(s - m)
            inv = 1.0 / jnp.sum(pexp, axis=-1, keepdims=True)
            o = jnp.dot(pexp.astype(BF16), vb[krows, sl], preferred_element_type=F32) * inv
            o_ref[qrows, sl] = jnp.where(lo_half, o[:LANES], o[LANES:]).astype(BF16)


def _band(q, kp, vp, bias, *, row_lo, row_hi):
    nb, tq, _ = q.shape
    qpb = 2 if tq % (2 * LANES) == 0 else 1
    nkb = BAND_PAST // LANES + qpb
    kspecs = [pl.BlockSpec((None, LANES, ATT_W), (lambda b, j, c=c: (b, j * qpb + c, 0)))
              for c in range(nkb)]
    kern = functools.partial(_band_kernel, row_lo=row_lo, row_hi=row_hi, qpb=qpb)
    qspec = pl.BlockSpec((None, qpb * LANES, ATT_W), lambda b, j: (b, j, 0))
    return pl.pallas_call(
        kern, out_shape=jax.ShapeDtypeStruct((nb, tq, ATT_W), BF16),
        grid=(nb, tq // (qpb * LANES)),
        in_specs=[qspec] + kspecs + kspecs + [pl.BlockSpec(bias.shape, lambda b, j: (0, 0, 0))],
        out_specs=qspec,
        compiler_params=_cparams(("parallel", "parallel")), name="band",
    )(q, *([kp] * nkb), *([vp] * nkb), bias)


def _merge_kernel(x_ref, oa_ref, ob_ref, gattn_ref, wg_ref, wa_ref, wb_ref, wo_ref,
                  gffn_ref, wpqT_ref, c1_ref, c2_ref,
                  x2_ref, h2T_ref, s1_ref, s2_ref):
    x = x_ref[...]
    ms = jnp.mean(x * x, axis=-1, keepdims=True)
    h = (x * lax.rsqrt(ms + EPS) * gattn_ref[...]).astype(BF16)
    ga = jax.nn.sigmoid(jnp.dot(h, wg_ref[:, :D_MODEL], preferred_element_type=F32))
    gb = jax.nn.sigmoid(jnp.dot(h, wg_ref[:, D_MODEL:], preferred_element_type=F32))
    ma = jnp.dot(oa_ref[...], wa_ref[...], preferred_element_type=F32)
    mb = jnp.dot(ob_ref[...], wb_ref[...], preferred_element_type=F32)
    merged = (ga * ma + gb * mb).astype(BF16)
    x2 = x + jnp.dot(merged, wo_ref[...], preferred_element_type=F32)
    x2_ref[...] = x2
    ms2 = jnp.mean(x2 * x2, axis=-1, keepdims=True)
    h2T = (x2 * lax.rsqrt(ms2 + EPS) * gffn_ref[...]).T.astype(BF16)
    h2T_ref[...] = h2T
    half = PEER_DQ // 2
    for hh in range(PEER_HEADS):
        qT = jnp.dot(wpqT_ref[hh * PEER_DQ:(hh + 1) * PEER_DQ, :], h2T,
                     preferred_element_type=F32).astype(BF16)
        s1_ref[hh] = jnp.dot(c1_ref[...], qT[:half], preferred_element_type=F32)
        s2_ref[hh] = jnp.dot(c2_ref[...], qT[half:], preferred_element_type=F32)


def _merge(x, oa, ob, gattn, wg, wa, wb, wo, gffn, wpqT, c1, c2):
    n = x.shape[0]
    tm = ROW_TILE
    row = lambda i: (i, 0)
    const = lambda i: (0, 0)
    return pl.pallas_call(
        _merge_kernel,
        out_shape=(
            jax.ShapeDtypeStruct((n, D_MODEL), F32),
            jax.ShapeDtypeStruct((D_MODEL, n), BF16),
            jax.ShapeDtypeStruct((PEER_HEADS, N_KEYS, n), F32),
            jax.ShapeDtypeStruct((PEER_HEADS, N_KEYS, n), F32),
        ),
        grid=(n // tm,),
        in_specs=[
            pl.BlockSpec((tm, D_MODEL), row),
            pl.BlockSpec((tm, ATT_W), row),
            pl.BlockSpec((tm, ATT_W), row),
            pl.BlockSpec((1, D_MODEL), const),
            pl.BlockSpec(wg.shape, const),
            pl.BlockSpec(wa.shape, const),
            pl.BlockSpec(wb.shape, const),
            pl.BlockSpec(wo.shape, const),
            pl.BlockSpec((1, D_MODEL), const),
            pl.BlockSpec(wpqT.shape, const),
            pl.BlockSpec(c1.shape, const),
            pl.BlockSpec(c2.shape, const),
        ],
        out_specs=(
            pl.BlockSpec((tm, D_MODEL), row),
            pl.BlockSpec((D_MODEL, tm), lambda i: (0, i)),
            pl.BlockSpec((PEER_HEADS, N_KEYS, tm), lambda i: (0, 0, i)),
            pl.BlockSpec((PEER_HEADS, N_KEYS, tm), lambda i: (0, 0, i)),
        ),
        compiler_params=_cparams(("parallel",)), name="merge",
    )(x, oa, ob, gattn, wg, wa, wb, wo, gffn, wpqT, c1, c2)


def _pack_rows(x):
    return pltpu.bitcast(x, jnp.uint32)


def _unpack_rows(x):
    return pltpu.bitcast(x, BF16)


def _top16(x, distinct):
    idx = lax.broadcasted_iota(I32, x.shape, 0)
    rank = jnp.full(x.shape, PEER_TOPK, I32)
    rows = []
    for r in range(PEER_TOPK):
        m = jnp.max(x, axis=0, keepdims=True)
        rows.append(m)
        hit = x == m
        if not distinct:
            hit = idx == jnp.min(jnp.where(hit, idx, N_KEYS), axis=0, keepdims=True)
        rank = jnp.where(hit, r, rank)
        x = jnp.where(hit, -jnp.inf, x)
    return jnp.concatenate(rows, axis=0), rank


def _select_kernel(s1_ref, s2_ref, cnt_ref, e1_ref, r2_ref, e2_ref):
    sub8 = lax.broadcasted_iota(I32, (8, 1), 0).astype(F32)
    sub16 = lax.broadcasted_iota(I32, (PEER_TOPK, 1), 0).astype(F32)

    def count(mask):
        return jnp.sum(jnp.where(mask, 1.0, 0.0), axis=0, keepdims=True)

    def head_body(hh, carry):
        s1 = s1_ref[hh]
        s2 = s2_ref[hh]
        v1, r1 = _top16(s1, True)
        v2, r2 = _top16(s2, True)
        ranked = jnp.maximum(count(r1 < PEER_TOPK), count(r2 < PEER_TOPK))
        v1, r1, v2, r2 = lax.cond(
            jnp.max(ranked) > PEER_TOPK,
            lambda: _top16(s1, False) + _top16(s2, False),
            lambda: (v1, r1, v2, r2))
        pieces = [v1[0:1] + v2]
        for a in range(1, 8):
            piece = v1[a:a + 1] + v2[0:8]
            pieces.append(jnp.where(sub8 < PEER_TOPK // (a + 1), piece, -jnp.inf))
        pieces.append(v1[8:16] + v2[0:1])
        cand = jnp.concatenate(pieces, axis=0)
        rem = jnp.full((1, cand.shape[1]), float(PEER_TOPK), F32)
        tau = jnp.zeros((1, cand.shape[1]), F32)
        x = cand
        for _ in range(PEER_TOPK):
            m = jnp.max(x, axis=0, keepdims=True)
            hit = x == m
            tau = jnp.where(rem > 0, m, tau)
            rem = rem - count(hit)
            x = jnp.where(hit, -jnp.inf, x)
        need = PEER_TOPK - count(cand > tau)
        kept = []
        for a in range(PEER_TOPK):
            row = pieces[a] if a < 8 else pieces[8][a - 8:a - 7]
            ties = count(row == tau)
            kept.append(count(row > tau) + jnp.minimum(ties, jnp.maximum(need, 0.0)))
            need = need - ties
        smax = v1[0:1] + v2[0:1]
        z = jnp.sum(jnp.where(sub16 < kept[0], jnp.exp(pieces[0] - smax), 0.0), axis=0, keepdims=True)
        for a in range(1, 8):
            z = z + jnp.sum(jnp.where(sub8 < kept[a], jnp.exp(pieces[a] - smax), 0.0),
                            axis=0, keepdims=True)
        tail = jnp.concatenate(kept[8:], axis=0)
        z = z + jnp.sum(jnp.where(tail > 0, jnp.exp(pieces[8] - smax), 0.0), axis=0, keepdims=True)
        cnt = jnp.zeros(s1.shape, F32)
        for a in range(PEER_TOPK):
            cnt = jnp.where(r1 == a, kept[a], cnt)
        cnt_ref[hh] = cnt
        e1_ref[hh] = jnp.exp(s1 - v1[0:1]) * (1.0 / z)
        r2b = r2.astype(F32).astype(BF16)
        e2b = jnp.exp(s2 - v2[0:1]).astype(BF16)
        for c in range(s2.shape[1] // LANES):
            cs = slice(c * LANES, (c + 1) * LANES)
            r2_ref[hh, c] = _pack_rows(r2b[:, cs])
            e2_ref[hh, c] = _pack_rows(e2b[:, cs])
        return carry

    lax.fori_loop(0, PEER_HEADS, head_body, 0)


def _select(s1, s2):
    n = s1.shape[-1]
    tn = SEL_TILE
    blk = pl.BlockSpec((PEER_HEADS, N_KEYS, tn), lambda i: (0, 0, i))
    packed = jax.ShapeDtypeStruct((PEER_HEADS, n // LANES, N_KEYS // 2, LANES), jnp.uint32)
    pblk = pl.BlockSpec((PEER_HEADS, tn // LANES, N_KEYS // 2, LANES), lambda i: (0, i, 0, 0))
    return pl.pallas_call(
        _select_kernel,
        out_shape=(
            jax.ShapeDtypeStruct(s1.shape, F32),
            jax.ShapeDtypeStruct(s1.shape, F32),
            packed, packed,
        ),
        grid=(n // tn,),
        in_specs=[blk, blk],
        out_specs=(blk, blk, pblk, pblk),
        compiler_params=_cparams(("parallel",)), name="select",
    )(s1, s2)


def _peer_kernel(h2T_ref, u_ref, vT_ref, vTp_ref, cnt_ref, e1_ref, r2_ref, e2_ref, x2_ref,
                 y_ref, acc_ref, wa0_ref, wa1_ref, z0_ref, z1_ref, rowc_ref, rowe_ref):
    e = pl.program_id(1)
    zero = jnp.minimum(e, 0)
    z_refs = (z0_ref, z1_ref)
    wa_refs = (wa0_ref, wa1_ref)

    @pl.when(e == 0)
    def _():
        acc_ref[...] = jnp.zeros_like(acc_ref)
        wa_refs[(PEER_EB // PEER_GRP - 1) % 2][...] = jnp.zeros_like(wa1_ref)

    tn = h2T_ref.shape[1]
    nsub = PEER_EB // N_KEYS
    reps = N_KEYS // BF16_ROWS
    for sub in range(nsub):
        for hh in range(PEER_HEADS):
            rowc_ref[sub, hh] = _pack_rows(
                jnp.broadcast_to(cnt_ref[hh, sub:sub + 1, :], (BF16_ROWS, tn)).astype(BF16))
            rowe_ref[sub, hh] = _pack_rows(
                jnp.broadcast_to(e1_ref[hh, sub:sub + 1, :], (BF16_ROWS, tn)).astype(BF16))

    ngrp = PEER_EB // PEER_GRP

    def scores(g, slot):
        start = g * PEER_GRP if isinstance(g, int) else pl.multiple_of(g * PEER_GRP, PEER_GRP)
        z_refs[slot][0] = jnp.dot(u_ref[pl.ds(start, PEER_GRP), :], h2T_ref[...],
                                  preferred_element_type=F32)

    def weights(g, slot):
        for s in range(PEER_GRP // N_KEYS):
            sub = g * (PEER_GRP // N_KEYS) + s
            for c in range(tn // LANES):
                cs = slice(c * LANES, (c + 1) * LANES)
                zc = z_refs[slot][zero, s * N_KEYS:(s + 1) * N_KEYS, cs]
                a = (0.5 * zc * (1.0 + lax.erf(zc * (2.0 ** -0.5)))).astype(BF16)
                w = jnp.zeros((N_KEYS, LANES), BF16)
                for hh in range(PEER_HEADS):
                    cnt = jnp.concatenate([_unpack_rows(rowc_ref[sub + zero, hh, :, cs])] * reps, axis=0)
                    e1 = jnp.concatenate([_unpack_rows(rowe_ref[sub + zero, hh, :, cs])] * reps, axis=0)
                    w = w + jnp.where(_unpack_rows(r2_ref[hh, c]) < cnt,
                                      _unpack_rows(e2_ref[hh, c]) * e1, jnp.zeros((), BF16))
                wa_refs[slot][0, s * (N_KEYS // 2):(s + 1) * (N_KEYS // 2), cs] = _pack_rows(w * a)

    def down(vt, slot):
        acc_ref[...] += jnp.dot(vt, _unpack_rows(wa_refs[slot][zero]), preferred_element_type=F32)

    last = (ngrp - 1) % 2
    scores(0, 0)
    down(vTp_ref[0], last)
    for g in range(ngrp):
        if g + 1 < ngrp:
            scores(g + 1, (g + 1) % 2)
        weights(g, g % 2)
        if g > 0:
            down(vT_ref[g - 1], (g - 1) % 2)

    @pl.when(e == pl.num_programs(1) - 1)
    def _():
        down(vT_ref[ngrp - 1], last)
        y_ref[...] = x2_ref[...] + acc_ref[...].T


def _peer(h2T, u, vT, cnt, e1, r2, e2, x2):
    n = x2.shape[0]
    tn = min(PEER_TOK, n)
    nsub = PEER_EB // N_KEYS
    tok3 = pl.BlockSpec((PEER_HEADS, tn // LANES, N_KEYS // 2, LANES), lambda i, e: (0, i, 0, 0),
                        pipeline_mode=pl.Buffered(1))
    key1 = pl.BlockSpec((PEER_HEADS, nsub, tn), lambda i, e: (0, e, i))
    return pl.pallas_call(
        _peer_kernel,
        out_shape=jax.ShapeDtypeStruct((n, D_MODEL), F32),
        grid=(n // tn, N_EXPERTS // PEER_EB),
        in_specs=[
            pl.BlockSpec((D_MODEL, tn), lambda i, e: (0, i), pipeline_mode=pl.Buffered(1)),
            pl.BlockSpec((PEER_EB, D_MODEL), lambda i, e: (e, 0)),
            pl.BlockSpec((PEER_EB // PEER_GRP, D_MODEL, PEER_GRP), lambda i, e: (e, 0, 0)),
            pl.BlockSpec((1, D_MODEL, PEER_GRP),
                         lambda i, e: (jnp.maximum(e * (PEER_EB // PEER_GRP) - 1, 0), 0, 0)),
            key1, key1, tok3, tok3,
            pl.BlockSpec((tn, D_MODEL), lambda i, e: (i, 0), pipeline_mode=pl.Buffered(1)),
        ],
        out_specs=pl.BlockSpec((tn, D_MODEL), lambda i, e: (i, 0)),
        scratch_shapes=[
            pltpu.VMEM((D_MODEL, tn), F32),
            pltpu.VMEM((1, PEER_GRP // 2, tn), jnp.uint32),
            pltpu.VMEM((1, PEER_GRP // 2, tn), jnp.uint32),
            pltpu.VMEM((1, PEER_GRP, tn), F32),
            pltpu.VMEM((1, PEER_GRP, tn), F32),
            pltpu.VMEM((nsub, PEER_HEADS, BF16_ROWS // 2, tn), jnp.uint32),
            pltpu.VMEM((nsub, PEER_HEADS, BF16_ROWS // 2, tn), jnp.uint32),
        ],
        compiler_params=_cparams(("parallel", "arbitrary")), name="peer",
    )(h2T, u, vT, vT, cnt, e1, r2, e2, x2)


def _rope_tables(pos):
    posf = pos.astype(F32)[:, None]
    t = pos.shape[0]

    def tab(half, reps):
        inv = jnp.power(ROPE_THETA, -jnp.arange(half, dtype=F32) / half)
        ang = posf * inv[None, :]
        cos = jnp.cos(ang)
        sin = jnp.sin(ang)
        return (jnp.tile(jnp.concatenate([cos, cos], axis=1), (1, reps)),
                jnp.tile(jnp.concatenate([-sin, sin], axis=1), (1, reps)))

    cosq, sinq = tab(HEAD_DIM // 2, N_HEADS)
    cosi, sini = tab(IDX_DIM // 2, IDX_HEADS + 1)
    pad = IDX_W - cosi.shape[1]
    cosi = jnp.concatenate([cosi, jnp.ones((t, pad), F32)], axis=1)
    sini = jnp.concatenate([sini, jnp.zeros((t, pad), F32)], axis=1)
    return cosq, sinq, cosi, sini


def _band_bias(rel_bias):
    qi = np.arange(LANES)[:, None]
    kj = np.arange(BAND_KEYS)[None, :]
    ok = (kj // CHUNK >= qi // CHUNK) & (kj // CHUNK <= qi // CHUNK + BAND_PAST // CHUNK)
    n = BAND_KEYS + LANES - 1
    m = np.arange(n)
    diag = rel_bias.astype(F32)[:, np.clip(BAND_PAST + LANES - 1 - m, -REL_CLIP, REL_CLIP) + REL_CLIP]
    wrapped = jnp.tile(jnp.pad(diag, ((0, 0), (0, 1))), (1, LANES))[:, :LANES * n]
    toep = wrapped.reshape(-1, LANES, n)[:, :, LANES - 1:LANES - 1 + BAND_KEYS]
    return jnp.where(jnp.asarray(ok)[None], toep, NEG)


def _layer_weights(g_attn, w_in, g_qa, g_ka, g_ik, g_qb, g_kb, rel_bias, w_br_a, w_br_b,
                   w_out, g_ffn, w_pq, c1, c2, u, v):
    a0 = 3 * ATT_W
    i0 = a0 + IDX_HEADS * IDX_DIM + IDX_DIM + IDX_HEADS
    b0 = i0 + 3 * ATT_W
    idx_cols = i0 - a0
    w1 = jnp.concatenate(
        [w_in[:, :a0], w_in[:, i0:b0], w_in[:, a0:i0],
         jnp.zeros((D_MODEL, IDX_W - idx_cols), w_in.dtype)], axis=1).astype(BF16)
    wg = w_in[:, b0:].astype(BF16)
    hd = jnp.asarray(np.kron(np.eye(N_HEADS), np.ones((HEAD_DIM, HEAD_DIM))) / HEAD_DIM, BF16)
    gh = jnp.stack([jnp.tile(g, N_HEADS) for g in (g_qa, g_ka, g_qb, g_kb)]).astype(F32)
    gidx = jnp.concatenate(
        [g_ik.astype(F32), jnp.full((IDX_HEADS,), IDX_HEADS ** -0.5, F32),
         jnp.ones((LANES - IDX_DIM - IDX_HEADS,), F32)])[None, :]
    tri = jnp.asarray(np.tril(np.ones((KEY_TILE, KEY_TILE)), -1), BF16)
    return dict(
        w1=w1, wg=wg, hd=hd, gh=gh, gidx=gidx, tri=tri,
        gattn=g_attn.astype(F32)[None, :], gffn=g_ffn.astype(F32)[None, :],
        bias=_band_bias(rel_bias),
        wa=w_br_a.astype(BF16), wb=w_br_b.astype(BF16), wo=w_out.astype(BF16),
        wpqT=w_pq.T.astype(BF16), c1=c1.astype(BF16), c2=c2.astype(BF16),
        u=u.astype(BF16),
        vT=v.astype(BF16).reshape(N_EXPERTS // PEER_GRP, PEER_GRP, D_MODEL).transpose(0, 2, 1))


def _ffn(x, oa, ob, w):
    x2, h2T, s1, s2 = _merge(x, oa, ob, w["gattn"], w["wg"], w["wa"], w["wb"], w["wo"],
                             w["gffn"], w["wpqT"], w["c1"], w["c2"])
    cnt, e1, r2, e2 = _select(s1, s2)
    return _peer(h2T, w["u"], w["vT"], cnt, e1, r2, e2, x2)


def _prompt_layer(x, w):
    b, t, _ = x.shape
    xf = x.reshape(b * t, D_MODEL)
    tabs = _rope_tables(jnp.arange(t))
    (aqT, ak, akb, av, avT, iqT, ik, ikb, iwT, bq, bk, bkb, bv, bvb) = _proj(
        xf, b, t, w["w1"], w["gattn"], w["hd"], w["gh"], w["gidx"], *tabs)
    oa = _dsa(aqT, iqT, iwT, akb.reshape(b, t, ATT_W), avT, ikb.reshape(b, t, LANES),
              w["tri"], q0=0, n_valid=t)
    front = ((0, 0), (BAND_PAST, 0), (0, 0))
    ob = _band(bq.reshape(b, t, ATT_W), jnp.pad(bkb.reshape(b, t, ATT_W), front),
               jnp.pad(bvb.reshape(b, t, ATT_W), front), w["bias"],
               row_lo=BAND_PAST, row_hi=BAND_PAST + t)
    y = _ffn(xf, oa.reshape(b * t, ATT_W), ob.reshape(b * t, ATT_W), w)
    keep = min(BAND_PAST, t)
    heads = lambda a: a.reshape(b, t, N_HEADS, HEAD_DIM)
    return y.reshape(b, t, D_MODEL), (
        heads(ak), heads(av), ik.reshape(b, t, IDX_DIM),
        heads(bk)[:, t - keep:], heads(bv)[:, t - keep:])


def _sample_layer(x, ca_k, ca_v, ca_ik, cb_k, cb_v, w):
    b, t, _ = x.shape
    n = b * t
    past = ca_k.shape[1]
    xf = x.reshape(n, D_MODEL)
    tabs = _rope_tables(jnp.tile(past + jnp.arange(t), b))
    (aqT, ak, akb, av, avT, iqT, ik, ikb, iwT, bq, bk, bkb, bv, bvb) = _proj(
        xf, 1, n, w["w1"], w["gattn"], w["hd"], w["gh"], w["gidx"], *tabs)

    def per_seq_T(aT, rows):
        a = aT[0].T.reshape(b, t, rows)
        return jnp.pad(a, ((0, 0), (0, LANES - t), (0, 0))).transpose(0, 2, 1)

    lk = past + t
    lp = -(-lk // KEY_TILE) * KEY_TILE
    padk = ((0, 0), (0, lp - lk), (0, 0))
    k_all = jnp.pad(jnp.concatenate(
        [ca_k.reshape(b, past, ATT_W).astype(BF16), akb.reshape(b, t, ATT_W)], axis=1), padk)
    v_all = jnp.pad(jnp.concatenate(
        [ca_v.reshape(b, past, ATT_W).astype(BF16), av.astype(BF16).reshape(b, t, ATT_W)],
        axis=1), padk)
    vT_all = v_all.reshape(b, lp // KEY_TILE, KEY_TILE, ATT_W).transpose(0, 1, 3, 2)
    ik_all = jnp.pad(jnp.concatenate(
        [jnp.pad(ca_ik.astype(BF16), ((0, 0), (0, 0), (0, LANES - IDX_DIM))),
         ikb.reshape(b, t, LANES)], axis=1), padk)
    oa = _dsa(per_seq_T(aqT, ATT_W), per_seq_T(iqT, 2 * LANES), per_seq_T(iwT, IDX_HEADS),
              k_all, vT_all, ik_all, w["tri"], q0=past, n_valid=lk)[:, :t]

    pb = cb_k.shape[1]
    padb = ((0, 0), (BAND_PAST - pb, LANES - t), (0, 0))
    qpad = jnp.pad(bq.reshape(b, t, ATT_W), ((0, 0), (0, LANES - t), (0, 0)))
    kb_all = jnp.pad(jnp.concatenate(
        [cb_k.reshape(b, pb, ATT_W).astype(BF16), bkb.reshape(b, t, ATT_W)], axis=1), padb)
    vb_all = jnp.pad(jnp.concatenate(
        [cb_v.reshape(b, pb, ATT_W).astype(BF16), bvb.reshape(b, t, ATT_W)], axis=1), padb)
    ob = _band(qpad, kb_all, vb_all, w["bias"],
               row_lo=BAND_PAST - pb, row_hi=BAND_PAST + t)[:, :t]

    y = _ffn(xf, oa.reshape(n, ATT_W), ob.reshape(n, ATT_W), w)
    keep = min(BAND_PAST, t)
    heads = lambda a: a.reshape(b, t, N_HEADS, HEAD_DIM)
    return y.reshape(b, t, D_MODEL), (
        heads(ak), heads(av), ik.reshape(b, t, IDX_DIM),
        heads(bk)[:, t - keep:], heads(bv)[:, t - keep:])


def kernel(x_prompt, x_sample, cache_a_k, cache_a_v, cache_a_ik, cache_b_k, cache_b_v,
           g_attn, w_in, g_qa, g_ka, g_ik, g_qb, g_kb, rel_bias, w_br_a, w_br_b, w_out,
           g_ffn, w_pq, peer_c1, peer_c2, peer_u, peer_v):
    depth = w_in.shape[0]
    xp, xs = x_prompt, x_sample
    sp, ss = [], []
    for l in range(depth):
        w = _layer_weights(g_attn[l], w_in[l], g_qa[l], g_ka[l], g_ik[l], g_qb[l], g_kb[l],
                           rel_bias[l], w_br_a[l], w_br_b[l], w_out[l], g_ffn[l], w_pq[l],
                           peer_c1[l], peer_c2[l], peer_u[l], peer_v[l])
        xp, st_p = _prompt_layer(xp, w)
        xs, st_s = _sample_layer(xs, cache_a_k[l], cache_a_v[l], cache_a_ik[l],
                                 cache_b_k[l], cache_b_v[l], w)
        sp.append(st_p)
        ss.append(st_s)
    stack = lambda sts, i: jnp.stack([s[i] for s in sts])
    return (xp, xs) + tuple(stack(sp, i) for i in range(5)) + tuple(stack(ss, i) for i in range(5))
```
